```python
import math
import jax, jax.numpy as jnp
from jax import lax
import numpy as np

D_MODEL = 1024
BATCH = 8
SEQ = 2048
DEPTH = 1
DEC_BATCH = 128
DEC_SEQ = 4
PAST_LEN = 16384
PAGE_SIZE = 128

A_HEADS = 4
A_HEAD_DIM = D_MODEL // (2 * A_HEADS)
A_WIDTH = A_HEADS * A_HEAD_DIM
CHUNK = 128
SSM_WIDTH = D_MODEL - A_WIDTH
SSM_GROUP = 16
SSM_GROUPS = SSM_WIDTH // SSM_GROUP
SSM_STATE = 64
DT_MIN = 1e-3
DT_MAX = 1e-1
D_FF = 11 * D_MODEL // 4
CONV_W = 3
EPS = 1e-6

kernel_name = 'hymba_sgu_s5_convffn_step'


def _rmsnorm(x, g):
    xf = x.astype(jnp.float32)
    xf = xf * lax.rsqrt(jnp.mean(xf * xf, axis=-1, keepdims=True) + EPS)
    return xf.astype(x.dtype) * g


def _chunk_sgu(u, v, w_s, b_s):
    bt, L, H, Dh = v.shape
    w = jnp.tril(w_s)
    if L <= CHUNK:
        s = jnp.einsum('hts,bshd->bthd', w[:, :L, :L], v) + jnp.transpose(b_s[:, :L])[None, :, :, None]
    else:
        n = -(-L // CHUNK)
        vp = jnp.pad(v, ((0, 0), (0, n * CHUNK - L), (0, 0), (0, 0))).reshape(bt, n, CHUNK, H, Dh)
        s = jnp.einsum('hts,bnshd->bnthd', w, vp) + jnp.transpose(b_s)[None, None, :, :, None]
        s = s.reshape(bt, n * CHUNK, H, Dh)[:, :L]
    return u * s


def _cplx_combine(e1, e2):
    a1r, a1i, b1r, b1i = e1
    a2r, a2i, b2r, b2i = e2
    return (a2r * a1r - a2i * a1i,
            a2r * a1i + a2i * a1r,
            a2r * b1r - a2i * b1i + b2r,
            a2r * b1i + a2i * b1r + b2i)


def _ssm_scan(xs, h0_re, h0_im, lam_re, lam_im, log_dt, b_re, b_im, c_re, c_im, d_skip):
    bt, L, _ = xs.shape
    f32 = jnp.float32
    xf = xs.astype(f32).reshape(bt, L, SSM_GROUPS, SSM_GROUP)
    lr = lam_re.astype(f32)
    li = lam_im.astype(f32)
    dt = jnp.exp(log_dt.astype(f32))[:, None]
    mag = jnp.exp(lr * dt)
    ar = mag * jnp.cos(li * dt)
    ai = mag * jnp.sin(li * dt)
    den = lr * lr + li * li
    fr = ((ar - 1.0) * lr + ai * li) / den
    fi = (ai * lr - (ar - 1.0) * li) / den
    bre = b_re.astype(f32)
    bim = b_im.astype(f32)
    bbr = fr[..., None] * bre - fi[..., None] * bim
    bbi = fr[..., None] * bim + fi[..., None] * bre
    ur = jnp.einsum('blgc,gpc->blgp', xf, bbr)
    ui = jnp.einsum('blgc,gpc->blgp', xf, bbi)
    h0r = h0_re.astype(f32)
    h0i = h0_im.astype(f32)
    ur = ur.at[:, 0].add(ar * h0r - ai * h0i)
    ui = ui.at[:, 0].add(ar * h0i + ai * h0r)
    a_r = jnp.broadcast_to(ar, ur.shape)
    a_i = jnp.broadcast_to(ai, ui.shape)
    _, _, hr, hi = lax.associative_scan(_cplx_combine, (a_r, a_i, ur, ui), axis=1)
    y = (jnp.einsum('blgp,gcp->blgc', hr, c_re.astype(f32))
         - jnp.einsum('blgp,gcp->blgc', hi, c_im.astype(f32))
         + d_skip.astype(f32).reshape(SSM_GROUPS, SSM_GROUP) * xf)
    return (y.reshape(bt, L, SSM_WIDTH).astype(xs.dtype),
            hr[:, -1].astype(h0_re.dtype), hi[:, -1].astype(h0_im.dtype))


def _layer(x, h0_re, h0_im, conv_buf, p):
    bt, L, _ = x.shape
    n1 = _rmsnorm(x, p['g_mix'])
    proj = n1 @ p['w_in']
    u, v, xs = jnp.split(proj, [A_WIDTH, 2 * A_WIDTH], axis=-1)
    u = u.reshape(bt, L, A_HEADS, A_HEAD_DIM)
    v = _rmsnorm(v.reshape(bt, L, A_HEADS, A_HEAD_DIM), p['g_v'])
    a_out = _chunk_sgu(u, v, p['w_s'], p['b_s']).reshape(bt, L, A_WIDTH)
    ys, hr, hi = _ssm_scan(xs, h0_re, h0_im, p['lam_re'], p['lam_im'], p['log_dt'],
                           p['b_re'], p['b_im'], p['c_re'], p['c_im'], p['d_skip'])
    g = jax.nn.gelu(ys)
    b_out = g * jax.nn.sigmoid(g @ p['w_glu'] + p['b_glu'])
    mix = jnp.concatenate([_rmsnorm(a_out, p['g_out_a']), _rmsnorm(b_out, p['g_out_b'])], axis=-1)
    h = x + mix @ p['w_out']
    n2 = _rmsnorm(h, p['g_ffn'])
    up = n2 @ p['w_ffn_in']
    padded = jnp.concatenate([conv_buf.astype(up.dtype), up], axis=1)
    conv = p['conv_b'] + sum(p['conv_w'][k] * padded[:, k:k + L] for k in range(CONV_W))
    gate, val = jnp.split(conv, 2, axis=-1)
    h = h + (jax.nn.gelu(gate) * val) @ p['w_ffn_out']
    y = _rmsnorm(h, p['g_final'])
    return y, v, hr, hi, padded[:, -(CONV_W - 1):]


def setup_inputs(seed: int = 0) -> dict:
    key = jax.random.key(seed)
    ks = jax.random.split(key, 32)
    nrm = lambda k, s, sc: jax.random.normal(k, s, jnp.float32) * sc
    d_in = 2 * A_WIDTH + SSM_WIDTH
    lam_im = jnp.float32(math.pi) * jnp.arange(SSM_STATE, dtype=jnp.float32)[None, :] + nrm(ks[11], (SSM_GROUPS, SSM_STATE), 0.01)
    return {
        'x_prompt': nrm(ks[0], (BATCH, SEQ, D_MODEL), 1.0),
        'x_sample': nrm(ks[1], (DEC_BATCH, DEC_SEQ, D_MODEL), 1.0),
        'state_ssm_re': nrm(ks[2], (DEC_BATCH, SSM_GROUPS, SSM_STATE), 0.1),
        'state_ssm_im': nrm(ks[3], (DEC_BATCH, SSM_GROUPS, SSM_STATE), 0.1),
        'state_conv': nrm(ks[4], (DEC_BATCH, CONV_W - 1, 2 * D_FF), 1.0),
        'g_mix': 1.0 + nrm(ks[5], (D_MODEL,), 0.01),
        'w_in': nrm(ks[6], (D_MODEL, d_in), D_MODEL ** -0.5),
        'g_v': 1.0 + nrm(ks[7], (A_HEADS, A_HEAD_DIM), 0.01),
        'w_s': nrm(ks[8], (A_HEADS, CHUNK, CHUNK), CHUNK ** -0.5),
        'b_s': 1.0 + nrm(ks[9], (A_HEADS, CHUNK), 0.01),
        'lam_re': -0.5 + nrm(ks[10], (SSM_GROUPS, SSM_STATE), 0.01),
        'lam_im': lam_im,
        'log_dt': jax.random.uniform(ks[12], (SSM_GROUPS,), jnp.float32, math.log(DT_MIN), math.log(DT_MAX)),
        'b_re': nrm(ks[13], (SSM_GROUPS, SSM_STATE, SSM_GROUP), (2 * SSM_GROUP) ** -0.5),
        'b_im': nrm(ks[14], (SSM_GROUPS, SSM_STATE, SSM_GROUP), (2 * SSM_GROUP) ** -0.5),
        'c_re': nrm(ks[15], (SSM_GROUPS, SSM_GROUP, SSM_STATE), (2 * SSM_STATE) ** -0.5),
        'c_im': nrm(ks[16], (SSM_GROUPS, SSM_GROUP, SSM_STATE), (2 * SSM_STATE) ** -0.5),
        'd_skip': nrm(ks[17], (SSM_WIDTH,), 1.0),
        'w_glu': nrm(ks[18], (SSM_WIDTH, SSM_WIDTH), SSM_WIDTH ** -0.5),
        'b_glu': nrm(ks[19], (SSM_WIDTH,), 0.01),
        'g_out_a': 1.0 + nrm(ks[20], (A_WIDTH,), 0.01),
        'g_out_b': 1.0 + nrm(ks[21], (SSM_WIDTH,), 0.01),
        'w_out': nrm(ks[22], (A_WIDTH + SSM_WIDTH, D_MODEL), (A_WIDTH + SSM_WIDTH) ** -0.5),
        'g_ffn': 1.0 + nrm(ks[23], (D_MODEL,), 0.01),
        'w_ffn_in': nrm(ks[24], (D_MODEL, 2 * D_FF), D_MODEL ** -0.5),
        'conv_w': nrm(ks[25], (CONV_W, 2 * D_FF), CONV_W ** -0.5),
        'conv_b': nrm(ks[26], (2 * D_FF,), 0.01),
        'w_ffn_out': nrm(ks[27], (D_FF, D_MODEL), D_FF ** -0.5),
        'g_final': 1.0 + nrm(ks[28], (D_MODEL,), 0.01),
    }


def reference(x_prompt, x_sample, state_ssm_re, state_ssm_im, state_conv,
              g_mix, w_in, g_v, w_s, b_s, lam_re, lam_im, log_dt, b_re, b_im, c_re, c_im,
              d_skip, w_glu, b_glu, g_out_a, g_out_b, w_out, g_ffn, w_ffn_in, conv_w, conv_b,
              w_ffn_out, g_final):
    p = dict(g_mix=g_mix, w_in=w_in, g_v=g_v, w_s=w_s, b_s=b_s, lam_re=lam_re, lam_im=lam_im,
             log_dt=log_dt, b_re=b_re, b_im=b_im, c_re=c_re, c_im=c_im, d_skip=d_skip,
             w_glu=w_glu, b_glu=b_glu, g_out_a=g_out_a, g_out_b=g_out_b, w_out=w_out,
             g_ffn=g_ffn, w_ffn_in=w_ffn_in, conv_w=conv_w, conv_b=conv_b,
             w_ffn_out=w_ffn_out, g_final=g_final)
    h = x_prompt
    s_re = jnp.zeros((BATCH, SSM_GROUPS, SSM_STATE), state_ssm_re.dtype)
    s_im = jnp.zeros((BATCH, SSM_GROUPS, SSM_STATE), state_ssm_im.dtype)
    c_buf = jnp.zeros((BATCH, CONV_W - 1, 2 * D_FF), state_conv.dtype)
    for _ in range(DEPTH):
        h, _v_p, s_re, s_im, c_buf = _layer(h, s_re, s_im, c_buf, p)
    y_prompt, ssm_re_prompt, ssm_im_prompt, conv_prompt = h, s_re, s_im, c_buf
    h = x_sample
    s_re, s_im, c_buf = state_ssm_re, state_ssm_im, state_conv
    for _ in range(DEPTH):
        h, v_sample, s_re, s_im, c_buf = _layer(h, s_re, s_im, c_buf, p)
    y_sample = h
    return (y_prompt, y_sample, v_sample, ssm_re_prompt, ssm_im_prompt, conv_prompt, s_re, s_im, c_buf)
```

```python
import functools
import math

import jax
import jax.numpy as jnp
from jax import lax
from jax.experimental import pallas as pl
from jax.experimental.pallas import tpu as pltpu

D_MODEL = 1024
A_HEADS = 4
A_HEAD_DIM = 128
A_WIDTH = 512
CHUNK = 128
SSM_WIDTH = 512
SSM_GROUP = 16
SSM_GROUPS = 32
SSM_STATE = 64
D_FF = 2816
CONV_W = 3
EPS = 1e-6

SSM_HALVES = 2
HALF_GROUPS = SSM_GROUPS // SSM_HALVES
HALF_CH = HALF_GROUPS * SSM_GROUP
HALF_ST = HALF_GROUPS * SSM_STATE
STATE_W = SSM_HALVES * 2 * HALF_ST

FF_CHUNK = 256
FF_NCHUNK = D_FF // FF_CHUNK

SUBLANES = 8
SCAN_LANES = 512
VMEM_LIMIT_BYTES = 60 * 1024 * 1024

BF16 = jnp.bfloat16
F32 = jnp.float32


def _rms(x, g):
    ms = jnp.mean(x * x, axis=-1, keepdims=True)
    return x * lax.rsqrt(ms + EPS) * g


def _gelu(x):
    c = math.sqrt(2.0 / math.pi)
    return x * (0.5 * (1.0 + jnp.tanh(c * (x + 0.044715 * (x * x * x)))))


def _dot(a, b):
    return jnp.dot(a, b, preferred_element_type=F32)


def _scan_inplace(u_scr, hst_ref, ar_ref, ai_ref, k, n_batch, n_t):
    base = k * 2 * HALF_ST

    def row_chunk(rc, _):
        r0 = pl.multiple_of(rc * SUBLANES, SUBLANES)
        for c in range(HALF_ST // SCAN_LANES):
            lo = c * SCAN_LANES
            re_l = slice(lo, lo + SCAN_LANES)
            im_l = slice(HALF_ST + lo, HALF_ST + lo + SCAN_LANES)
            ar = jnp.broadcast_to(ar_ref[k:k + 1, re_l], (SUBLANES, SCAN_LANES))
            ai = jnp.broadcast_to(ai_ref[k:k + 1, re_l], (SUBLANES, SCAN_LANES))
            hr0 = hst_ref[pl.ds(r0, SUBLANES), base + lo:base + lo + SCAN_LANES]
            hi0 = hst_ref[pl.ds(r0, SUBLANES), base + HALF_ST + lo:base + HALF_ST + lo + SCAN_LANES]

            def step(t, carry):
                hr, hi = carry
                row = pl.multiple_of(t * n_batch + r0, SUBLANES)
                ur = u_scr[pl.ds(row, SUBLANES), re_l]
                ui = u_scr[pl.ds(row, SUBLANES), im_l]
                nhr = ar * hr - ai * hi + ur
                nhi = ar * hi + ai * hr + ui
                u_scr[pl.ds(row, SUBLANES), re_l] = nhr
                u_scr[pl.ds(row, SUBLANES), im_l] = nhi
                return nhr, nhi

            hr, hi = lax.fori_loop(0, n_t, step, (hr0, hi0), unroll=min(n_t, 4))
            hst_ref[pl.ds(r0, SUBLANES), base + lo:base + lo + SCAN_LANES] = hr
            hst_ref[pl.ds(r0, SUBLANES), base + HALF_ST + lo:base + HALF_ST + lo + SCAN_LANES] = hi
        return 0

    lax.fori_loop(0, n_batch // SUBLANES, row_chunk, 0)


def _mixer_kernel(n_batch, n_t, n_steps, with_v,
                  x_ref, h0_ref, gmix_ref, win_ref, gv_ref, ks_ref, bs_ref,
                  ar_ref, ai_ref, bsub_ref, csub_ref, dskip_ref,
                  wglu_ref, bglu_ref, goa_ref, gob_ref, wout_ref,
                  *rest):
    if with_v:
        hout_ref, st_ref, v_ref, u_scr, hst_ref = rest
    else:
        hout_ref, st_ref, u_scr, hst_ref = rest
        v_ref = None
    i = pl.program_id(0)

    @pl.when(i == 0)
    def _():
        hst_ref[...] = h0_ref[...]

    x = x_ref[...]
    n1 = _rms(x, gmix_ref[...]).astype(BF16)
    proj = _dot(n1, win_ref[...])
    xs = proj[:, 2 * A_WIDTH:]

    a_parts = []
    for h in range(A_HEADS):
        hs = slice(h * A_HEAD_DIM, (h + 1) * A_HEAD_DIM)
        vn = _rms(proj[:, A_WIDTH + h * A_HEAD_DIM:A_WIDTH + (h + 1) * A_HEAD_DIM], gv_ref[h:h + 1, :])
        if with_v:
            v_ref[:, hs] = vn
        s = _dot(ks_ref[h], vn.astype(BF16)) + bs_ref[:, hs]
        a_parts.append(proj[:, hs] * s)
    an = _rms(jnp.concatenate(a_parts, axis=-1), goa_ref[...])

    xs_b = xs.astype(BF16)
    y_parts = []
    for k in range(SSM_HALVES):
        u_scr[...] = _dot(xs_b[:, k * HALF_CH:(k + 1) * HALF_CH], bsub_ref[k])
        _scan_inplace(u_scr, hst_ref, ar_ref, ai_ref, k, n_batch, n_t)
        y_parts.append(_dot(u_scr[...].astype(BF16), csub_ref[k]))
    ys = jnp.concatenate(y_parts, axis=-1) + dskip_ref[...] * xs
    g = _gelu(ys)
    z = _dot(g.astype(BF16), wglu_ref[...]) + bglu_ref[...]
    bn = _rms(g * jax.nn.sigmoid(z), gob_ref[...])

    mix = jnp.concatenate([an, bn], axis=-1).astype(BF16)
    hout_ref[...] = x + _dot(mix, wout_ref[...])

    @pl.when(i == n_steps - 1)
    def _():
        st_ref[...] = hst_ref[...]


def _ffn_kernel(n_batch, n_t, n_steps,
                h_ref, conv0_ref, gffn_ref, wfi_ref, cw_ref, cb_ref, wfo_ref, gfin_ref,
                y_ref, convout_ref, pad_scr, carry_scr):
    rows = n_t * n_batch
    halo = (CONV_W - 1) * n_batch
    i = pl.program_id(0)

    @pl.when(i == 0)
    def _():
        carry_scr[...] = conv0_ref[...]

    h = h_ref[...]
    n2 = _rms(h, gffn_ref[...]).astype(BF16)
    acc = jnp.zeros((rows, D_MODEL), F32)
    for j in range(FF_NCHUNK):
        up = _dot(n2, wfi_ref[j])
        pad_scr[0:halo, :] = carry_scr[j]
        pad_scr[halo:halo + rows, :] = up
        carry_scr[j] = up[rows - halo:rows, :]
        cw = cw_ref[j]
        conv = cb_ref[j]
        for tap in range(CONV_W):
            conv = conv + cw[tap:tap + 1, :] * pad_scr[tap * n_batch:tap * n_batch + rows, :]
        act = _gelu(conv[:, :FF_CHUNK]) * conv[:, FF_CHUNK:]
        acc = acc + _dot(act.astype(BF16), wfo_ref[j])
    y_ref[...] = _rms(h + acc, gfin_ref[...])

    @pl.when(i == n_steps - 1)
    def _():
        convout_ref[...] = carry_scr[...]


def _const_spec(shape):
    zeros = (0,) * len(shape)
    return pl.BlockSpec(shape, lambda i: zeros, pipeline_mode=pl.Buffered(1))


def _mixer_call(x_tb, h0, p, n_batch, n_t, with_v):
    rows_total = x_tb.shape[0]
    rows = n_t * n_batch
    n_steps = rows_total // rows
    row_spec = pl.BlockSpec((rows, D_MODEL), lambda i: (i, 0))
    consts = [h0, p['g_mix'], p['w_in'], p['g_v'], p['ks'], p['bs'], p['ar'], p['ai'],
              p['bsub'], p['csub'], p['d_skip'], p['w_glu'], p['b_glu'],
              p['g_out_a'], p['g_out_b'], p['w_out']]
    out_shape = [jax.ShapeDtypeStruct((rows_total, D_MODEL), F32),
                 jax.ShapeDtypeStruct((n_batch, STATE_W), F32)]
    out_specs = [row_spec, pl.BlockSpec((n_batch, STATE_W), lambda i: (0, 0))]
    if with_v:
        out_shape.append(jax.ShapeDtypeStruct((rows_total, A_WIDTH), F32))
        out_specs.append(pl.BlockSpec((rows, A_WIDTH), lambda i: (i, 0)))
    return pl.pallas_call(
        functools.partial(_mixer_kernel, n_batch, n_t, n_steps, with_v),
        grid=(n_steps,),
        in_specs=[row_spec] + [_const_spec(c.shape) for c in consts],
        out_specs=out_specs,
        out_shape=out_shape,
        scratch_shapes=[pltpu.VMEM((rows, 2 * HALF_ST), F32),
                        pltpu.VMEM((n_batch, STATE_W), F32)],
        compiler_params=pltpu.CompilerParams(
            dimension_semantics=("arbitrary",), vmem_limit_bytes=VMEM_LIMIT_BYTES),
        name="mixer_v" if with_v else "mixer",
    )(x_tb, *consts)


def _ffn_call(h_tb, conv0, p, n_batch, n_t):
    rows_total = h_tb.shape[0]
    rows = n_t * n_batch
    n_steps = rows_total // rows
    halo = (CONV_W - 1) * n_batch
    row_spec = pl.BlockSpec((rows, D_MODEL), lambda i: (i, 0))
    consts = [conv0, p['g_ffn'], p['w_ffn_in'], p['conv_w'], p['conv_b'], p['w_ffn_out'], p['g_final']]
    carry_shape = (FF_NCHUNK, halo, 2 * FF_CHUNK)
    return pl.pallas_call(
        functools.partial(_ffn_kernel, n_batch, n_t, n_steps),
        grid=(n_steps,),
        in_specs=[row_spec] + [_const_spec(c.shape) for c in consts],
        out_specs=[row_spec, pl.BlockSpec(carry_shape, lambda i: (0, 0, 0))],
        out_shape=[jax.ShapeDtypeStruct((rows_total, D_MODEL), F32),
                   jax.ShapeDtypeStruct(carry_shape, F32)],
        scratch_shapes=[pltpu.VMEM((rows + halo, 2 * FF_CHUNK), F32),
                        pltpu.VMEM(carry_shape, F32)],
        compiler_params=pltpu.CompilerParams(
            dimension_semantics=("arbitrary",), vmem_limit_bytes=VMEM_LIMIT_BYTES),
        name="ffn",
    )(h_tb, *consts)


def _prep_params(g_mix, w_in, g_v, b_s_unused, lam_re, lam_im, log_dt, b_re, b_im, c_re, c_im,
                 d_skip, w_glu, b_glu, g_out_a, g_out_b, w_out, g_ffn, w_ffn_in, conv_w, conv_b,
                 w_ffn_out, g_final):
    lr = lam_re.astype(F32)
    li = lam_im.astype(F32)
    dt = jnp.exp(log_dt.astype(F32))[:, None]
    mag = jnp.exp(lr * dt)
    ar = mag * jnp.cos(li * dt)
    ai = mag * jnp.sin(li * dt)
    den = lr * lr + li * li
    fr = ((ar - 1.0) * lr + ai * li) / den
    fi = (ai * lr - (ar - 1.0) * li) / den
    bre = b_re.astype(F32)
    bim = b_im.astype(F32)
    bbr = fr[..., None] * bre - fi[..., None] * bim
    bbi = fr[..., None] * bim + fi[..., None] * bre
    eye = jnp.eye(HALF_GROUPS, dtype=F32)

    def blockdiag_in(m):
        return jnp.einsum('gpc,gh->gchp', m, eye).reshape(HALF_CH, HALF_ST)

    def blockdiag_out(m):
        return jnp.einsum('gcp,gh->gphc', m, eye).reshape(HALF_ST, HALF_CH)

    bsub, csub, ars, ais = [], [], [], []
    for k in range(SSM_HALVES):
        gs = slice(k * HALF_GROUPS, (k + 1) * HALF_GROUPS)
        bsub.append(jnp.concatenate([blockdiag_in(bbr[gs]), blockdiag_in(bbi[gs])], axis=1))
        csub.append(jnp.concatenate([blockdiag_out(c_re[gs].astype(F32)),
                                     -blockdiag_out(c_im[gs].astype(F32))], axis=0))
        ars.append(ar[gs].reshape(1, HALF_ST))
        ais.append(ai[gs].reshape(1, HALF_ST))

    def chunked_cols(m):
        lead = m.shape[:-1]
        m = m.reshape(lead + (2, FF_NCHUNK, FF_CHUNK))
        m = jnp.moveaxis(m, -2, 0)
        return m.reshape((FF_NCHUNK,) + lead + (2 * FF_CHUNK,))

    return dict(
        g_mix=g_mix.reshape(1, D_MODEL), w_in=w_in.astype(BF16), g_v=g_v,
        ar=jnp.concatenate(ars, axis=0), ai=jnp.concatenate(ais, axis=0),
        bsub=jnp.stack(bsub).astype(BF16), csub=jnp.stack(csub).astype(BF16),
        d_skip=d_skip.reshape(1, SSM_WIDTH), w_glu=w_glu.astype(BF16), b_glu=b_glu.reshape(1, SSM_WIDTH),
        g_out_a=g_out_a.reshape(1, A_WIDTH), g_out_b=g_out_b.reshape(1, SSM_WIDTH),
        w_out=w_out.astype(BF16), g_ffn=g_ffn.reshape(1, D_MODEL),
        w_ffn_in=chunked_cols(w_ffn_in.astype(BF16)),
        conv_w=chunked_cols(conv_w), conv_b=chunked_cols(conv_b.reshape(1, 2 * D_FF)),
        w_ffn_out=w_ffn_out.astype(BF16).reshape(FF_NCHUNK, FF_CHUNK, D_MODEL),
        g_final=g_final.reshape(1, D_MODEL), chunked_cols=chunked_cols)


def _state_in(s_re, s_im):
    nb = s_re.shape[0]
    parts = []
    for k in range(SSM_HALVES):
        gs = slice(k * HALF_GROUPS, (k + 1) * HALF_GROUPS)
        parts += [s_re[:, gs].reshape(nb, HALF_ST), s_im[:, gs].reshape(nb, HALF_ST)]
    return jnp.concatenate(parts, axis=1)


def _state_out(st):
    nb = st.shape[0]
    st = st.reshape(nb, SSM_HALVES, 2, HALF_GROUPS, SSM_STATE)
    return (st[:, :, 0].reshape(nb, SSM_GROUPS, SSM_STATE),
            st[:, :, 1].reshape(nb, SSM_GROUPS, SSM_STATE))


def _run_group(x, s_re, s_im, conv_buf, p, w_s, b_s, t_mixer, t_ffn, with_v):
    nb, nl, _ = x.shape
    tril = jnp.tril(w_s.astype(F32))[:, :t_mixer, :t_mixer]
    eye = jnp.eye(nb, dtype=F32)
    ks = jnp.einsum('hts,bc->htbsc', tril, eye).reshape(A_HEADS, t_mixer * nb, t_mixer * nb).astype(BF16)
    bs = jnp.broadcast_to(jnp.transpose(b_s[:, :t_mixer])[:, None, :, None],
                          (t_mixer, nb, A_HEADS, A_HEAD_DIM)).reshape(t_mixer * nb, A_WIDTH)
    pm = dict(p, ks=ks, bs=bs)

    x_tb = jnp.transpose(x, (1, 0, 2)).reshape(nl * nb, D_MODEL)
    res = _mixer_call(x_tb, _state_in(s_re, s_im), pm, nb, t_mixer, with_v)
    h_tb, st = res[0], res[1]
    conv0 = p['chunked_cols'](jnp.transpose(conv_buf, (1, 0, 2)).reshape((CONV_W - 1) * nb, 2 * D_FF))
    y_tb, conv_out = _ffn_call(h_tb, conv0, p, nb, t_ffn)
    y = jnp.transpose(y_tb.reshape(nl, nb, D_MODEL), (1, 0, 2))
    conv_out = conv_out.reshape(FF_NCHUNK, CONV_W - 1, nb, 2, FF_CHUNK)
    conv_out = jnp.transpose(conv_out, (2, 1, 3, 0, 4)).reshape(nb, CONV_W - 1, 2 * D_FF)
    new_re, new_im = _state_out(st)
    v = None
    if with_v:
        v = jnp.transpose(res[2].reshape(nl, nb, A_HEADS, A_HEAD_DIM), (1, 0, 2, 3))
    return y, v, new_re, new_im, conv_out


def kernel(x_prompt, x_sample, state_ssm_re, state_ssm_im, state_conv, g_mix, w_in, g_v, w_s, b_s,
           lam_re, lam_im, log_dt, b_re, b_im, c_re, c_im, d_skip, w_glu, b_glu, g_out_a, g_out_b,
           w_out, g_ffn, w_ffn_in, conv_w, conv_b, w_ffn_out, g_final):
    p = _prep_params(g_mix, w_in, g_v, b_s, lam_re, lam_im, log_dt, b_re, b_im, c_re, c_im, d_skip,
                     w_glu, b_glu, g_out_a, g_out_b, w_out, g_ffn, w_ffn_in, conv_w, conv_b,
                     w_ffn_out, g_final)
    nbp = x_prompt.shape[0]
    zero_state = jnp.zeros((nbp, SSM_GROUPS, SSM_STATE), state_ssm_re.dtype)
    zero_conv = jnp.zeros((nbp, CONV_W - 1, 2 * D_FF), state_conv.dtype)
    y_p, _, re_p, im_p, conv_p = _run_group(
        x_prompt, zero_state, zero_state, zero_conv, p, w_s, b_s, CHUNK, 64, False)
    nls = x_sample.shape[1]
    y_s, v_s, re_s, im_s, conv_s = _run_group(
        x_sample, state_ssm_re, state_ssm_im, state_conv, p, w_s, b_s, nls, nls, True)
    return (y_p, y_s, v_s, re_p, im_p, conv_p, re_s, im_s, conv_s)
```

```python
import functools
import math

import jax
import jax.numpy as jnp
from jax import lax
from jax.experimental import pallas as pl
from jax.experimental.pallas import tpu as pltpu

D_MODEL = 1024
A_HEADS = 4
A_HEAD_DIM = 128
A_WIDTH = 512
CHUNK = 128
SSM_WIDTH = 512
SSM_GROUP = 16
SSM_GROUPS = 32
SSM_STATE = 64
D_FF = 2816
CONV_W = 3
EPS = 1e-6

SSM_HALVES = 2
HALF_GROUPS = SSM_GROUPS // SSM_HALVES
HALF_CH = HALF_GROUPS * SSM_GROUP
HALF_ST = HALF_GROUPS * SSM_STATE
STATE_W = SSM_HALVES * 2 * HALF_ST

FF_CHUNK = 256
FF_NCHUNK = D_FF // FF_CHUNK
FFN_PROMPT_STEPS = 64

SUBLANES = 8
SCAN_LANES = 512
VMEM_LIMIT_BYTES = 60 * 1024 * 1024

BF16 = jnp.bfloat16
F32 = jnp.float32


def _rms(x, g):
    ms = jnp.mean(x * x, axis=-1, keepdims=True)
    return x * lax.rsqrt(ms + EPS) * g


def _gelu(x):
    c = math.sqrt(2.0 / math.pi)
    return x * (0.5 * (1.0 + jnp.tanh(c * (x + 0.044715 * (x * x * x)))))


def _dot(a, b):
    return jnp.dot(a, b, preferred_element_type=F32)


def _to_time_major(x, n_batch, n_t):
    w = x.shape[-1]
    return jnp.swapaxes(x.reshape(n_batch, n_t, w), 0, 1).reshape(n_t * n_batch, w)


def _to_batch_major(x, n_batch, n_t):
    w = x.shape[-1]
    return jnp.swapaxes(x.reshape(n_t, n_batch, w), 0, 1)


def _scan_inplace(u_scr, hst_ref, ar_ref, ai_ref, k, n_batch, n_t):
    base = k * 2 * HALF_ST

    def row_chunk(rc, _):
        r0 = pl.multiple_of(rc * SUBLANES, SUBLANES)
        for c in range(HALF_ST // SCAN_LANES):
            lo = c * SCAN_LANES
            re_l = slice(lo, lo + SCAN_LANES)
            im_l = slice(HALF_ST + lo, HALF_ST + lo + SCAN_LANES)
            ar = jnp.broadcast_to(ar_ref[k:k + 1, re_l], (SUBLANES, SCAN_LANES))
            ai = jnp.broadcast_to(ai_ref[k:k + 1, re_l], (SUBLANES, SCAN_LANES))
            hr0 = hst_ref[pl.ds(r0, SUBLANES), base + lo:base + lo + SCAN_LANES]
            hi0 = hst_ref[pl.ds(r0, SUBLANES), base + HALF_ST + lo:base + HALF_ST + lo + SCAN_LANES]

            def step(t, carry):
                hr, hi = carry
                row = pl.multiple_of(t * n_batch + r0, SUBLANES)
                ur = u_scr[pl.ds(row, SUBLANES), re_l]
                ui = u_scr[pl.ds(row, SUBLANES), im_l]
                nhr = ar * hr - ai * hi + ur
                nhi = ar * hi + ai * hr + ui
                u_scr[pl.ds(row, SUBLANES), re_l] = nhr
                u_scr[pl.ds(row, SUBLANES), im_l] = nhi
                return nhr, nhi

            hr, hi = lax.fori_loop(0, n_t, step, (hr0, hi0), unroll=min(n_t, 4))
            hst_ref[pl.ds(r0, SUBLANES), base + lo:base + lo + SCAN_LANES] = hr
            hst_ref[pl.ds(r0, SUBLANES), base + HALF_ST + lo:base + HALF_ST + lo + SCAN_LANES] = hi
        return 0

    lax.fori_loop(0, n_batch // SUBLANES, row_chunk, 0)


def _ssm_glu(xs, n_batch, n_t, u_scr, hst_ref, ar_ref, ai_ref, bsub_ref, csub_ref, dskip_ref,
             wglu_ref, bglu_ref, gob_ref):
    xs_b = xs.astype(BF16)
    y_parts = []
    for k in range(SSM_HALVES):
        u_scr[...] = _dot(xs_b[:, k * HALF_CH:(k + 1) * HALF_CH], bsub_ref[k])
        _scan_inplace(u_scr, hst_ref, ar_ref, ai_ref, k, n_batch, n_t)
        y_parts.append(_dot(u_scr[...].astype(BF16), csub_ref[k]))
    ys = jnp.concatenate(y_parts, axis=-1) + dskip_ref[...] * xs
    g = _gelu(ys)
    z = _dot(g.astype(BF16), wglu_ref[...]) + bglu_ref[...]
    return _rms(g * jax.nn.sigmoid(z), gob_ref[...])


def _mixer_prompt_kernel(n_batch, n_steps,
                         x_ref, h0_ref, gmix_ref, win_ref, gv_ref, ws_ref, bs_ref,
                         ar_ref, ai_ref, bsub_ref, csub_ref, dskip_ref,
                         wglu_ref, bglu_ref, goa_ref, gob_ref, wout_ref,
                         hout_ref, st_ref, u_scr, hst_ref):
    n_t = CHUNK
    rows = n_batch * n_t
    i = pl.program_id(0)

    @pl.when(i == 0)
    def _():
        hst_ref[...] = h0_ref[...]

    x = x_ref[...].reshape(rows, D_MODEL)
    n1 = _rms(x, gmix_ref[...]).astype(BF16)
    proj = _dot(n1, win_ref[...])

    a_heads = []
    for h in range(A_HEADS):
        hs = slice(h * A_HEAD_DIM, (h + 1) * A_HEAD_DIM)
        vn = _rms(proj[:, A_WIDTH + h * A_HEAD_DIM:A_WIDTH + (h + 1) * A_HEAD_DIM],
                  gv_ref[h:h + 1, :]).astype(BF16)
        v_wide = jnp.concatenate([vn[b * n_t:(b + 1) * n_t, :] for b in range(n_batch)], axis=-1)
        s_wide = _dot(ws_ref[h], v_wide)
        bias = bs_ref[:, hs]
        s = jnp.concatenate([s_wide[:, b * A_HEAD_DIM:(b + 1) * A_HEAD_DIM] + bias
                             for b in range(n_batch)], axis=0)
        a_heads.append(proj[:, hs] * s)
    an = _rms(jnp.concatenate(a_heads, axis=-1), goa_ref[...])

    xs = _to_time_major(proj[:, 2 * A_WIDTH:], n_batch, n_t)
    bn_tm = _ssm_glu(xs, n_batch, n_t, u_scr, hst_ref, ar_ref, ai_ref, bsub_ref, csub_ref,
                     dskip_ref, wglu_ref, bglu_ref, gob_ref)
    bn = _to_batch_major(bn_tm, n_batch, n_t).reshape(rows, SSM_WIDTH)

    mix = jnp.concatenate([an, bn], axis=-1).astype(BF16)
    hout_ref[...] = (x + _dot(mix, wout_ref[...])).reshape(n_batch, n_t, D_MODEL)

    @pl.when(i == n_steps - 1)
    def _():
        st_ref[...] = hst_ref[...]


def _mixer_sample_kernel(n_batch, n_t,
                         x_ref, h0_ref, gmix_ref, win_ref, gv_ref, ws_ref, bs_ref,
                         ar_ref, ai_ref, bsub_ref, csub_ref, dskip_ref,
                         wglu_ref, bglu_ref, goa_ref, gob_ref, wout_ref,
                         hout_ref, st_ref, v_ref, u_scr, hst_ref):
    hst_ref[...] = h0_ref[...]
    x = x_ref[...]
    n1 = _rms(x, gmix_ref[...]).astype(BF16)
    proj = _dot(n1, win_ref[...])

    a_heads = []
    for h in range(A_HEADS):
        hs = slice(h * A_HEAD_DIM, (h + 1) * A_HEAD_DIM)
        vn = _rms(proj[:, A_WIDTH + h * A_HEAD_DIM:A_WIDTH + (h + 1) * A_HEAD_DIM], gv_ref[h:h + 1, :])
        v_ref[:, hs] = vn
        slabs = []
        for t in range(n_t):
            s = jnp.broadcast_to(bs_ref[h * n_t + t:h * n_t + t + 1, :], (n_batch, A_HEAD_DIM))
            for src in range(t + 1):
                r = (h * n_t + t) * n_t + src
                s = s + ws_ref[r:r + 1, :] * vn[src * n_batch:(src + 1) * n_batch, :]
            slabs.append(proj[t * n_batch:(t + 1) * n_batch, hs] * s)
        a_heads.append(jnp.concatenate(slabs, axis=0))
    an = _rms(jnp.concatenate(a_heads, axis=-1), goa_ref[...])

    bn = _ssm_glu(proj[:, 2 * A_WIDTH:], n_batch, n_t, u_scr, hst_ref, ar_ref, ai_ref, bsub_ref,
                  csub_ref, dskip_ref, wglu_ref, bglu_ref, gob_ref)
    mix = jnp.concatenate([an, bn], axis=-1).astype(BF16)
    hout_ref[...] = x + _dot(mix, wout_ref[...])
    st_ref[...] = hst_ref[...]


def _ffn_kernel(n_batch, n_t, n_steps, batch_major,
                h_ref, conv0_ref, gffn_ref, wfi_ref, cw_ref, cb_ref, wfo_ref, gfin_ref,
                y_ref, convout_ref, carry_scr):
    rows = n_t * n_batch
    halo = (CONV_W - 1) * n_batch
    i = pl.program_id(0)

    @pl.when(i == 0)
    def _():
        carry_scr[...] = conv0_ref[...]

    if batch_major:
        h = _to_time_major(h_ref[...].reshape(rows, D_MODEL), n_batch, n_t)
    else:
        h = h_ref[...]
    n2 = _rms(h, gffn_ref[...]).astype(BF16)
    acc = jnp.zeros((rows, D_MODEL), F32)
    for j in range(FF_NCHUNK):
        conv = []
        for c0 in (j * FF_CHUNK, D_FF + j * FF_CHUNK):
            cs = slice(c0, c0 + FF_CHUNK)
            up = _dot(n2, wfi_ref[:, cs])
            padded = jnp.concatenate([carry_scr[:, cs], up], axis=0)
            carry_scr[:, cs] = up[rows - halo:rows, :]
            out = cb_ref[:, cs]
            for tap in range(CONV_W):
                out = out + cw_ref[tap:tap + 1, cs] * padded[tap * n_batch:tap * n_batch + rows, :]
            conv.append(out)
        act = _gelu(conv[0]) * conv[1]
        acc = acc + _dot(act.astype(BF16), wfo_ref[j * FF_CHUNK:(j + 1) * FF_CHUNK, :])
    y = _rms(h + acc, gfin_ref[...])
    if batch_major:
        y_ref[...] = _to_batch_major(y, n_batch, n_t)
    else:
        y_ref[...] = y

    @pl.when(i == n_steps - 1)
    def _():
        convout_ref[...] = carry_scr[...]


def _const_spec(shape):
    zeros = (0,) * len(shape)
    return pl.BlockSpec(shape, lambda i: zeros, pipeline_mode=pl.Buffered(1))


def _params():
    return pltpu.CompilerParams(dimension_semantics=("arbitrary",), vmem_limit_bytes=VMEM_LIMIT_BYTES)


def _mixer_consts(h0, p, ws, bs):
    return [h0, p['g_mix'], p['w_in'], p['g_v'], ws, bs, p['ar'], p['ai'],
            p['bsub'], p['csub'], p['d_skip'], p['w_glu'], p['b_glu'],
            p['g_out_a'], p['g_out_b'], p['w_out']]


def _mixer_prompt_call(x, h0, p, ws, bs):
    n_batch, n_l, _ = x.shape
    n_steps = n_l // CHUNK
    blk = pl.BlockSpec((n_batch, CHUNK, D_MODEL), lambda i: (0, i, 0))
    consts = _mixer_consts(h0, p, ws, bs)
    return pl.pallas_call(
        functools.partial(_mixer_prompt_kernel, n_batch, n_steps),
        grid=(n_steps,),
        in_specs=[blk] + [_const_spec(c.shape) for c in consts],
        out_specs=[blk, pl.BlockSpec((n_batch, STATE_W), lambda i: (0, 0))],
        out_shape=[jax.ShapeDtypeStruct(x.shape, F32),
                   jax.ShapeDtypeStruct((n_batch, STATE_W), F32)],
        scratch_shapes=[pltpu.VMEM((n_batch * CHUNK, 2 * HALF_ST), F32),
                        pltpu.VMEM((n_batch, STATE_W), F32)],
        compiler_params=_params(),
        name="mixer_prompt",
    )(x, *consts)


def _mixer_sample_call(x_tm, h0, p, ws, bs, n_batch, n_t):
    rows = n_batch * n_t
    consts = _mixer_consts(h0, p, ws, bs)
    return pl.pallas_call(
        functools.partial(_mixer_sample_kernel, n_batch, n_t),
        grid=(1,),
        in_specs=[_const_spec(x_tm.shape)] + [_const_spec(c.shape) for c in consts],
        out_specs=[pl.BlockSpec((rows, D_MODEL), lambda i: (0, 0)),
                   pl.BlockSpec((n_batch, STATE_W), lambda i: (0, 0)),
                   pl.BlockSpec((rows, A_WIDTH), lambda i: (0, 0))],
        out_shape=[jax.ShapeDtypeStruct((rows, D_MODEL), F32),
                   jax.ShapeDtypeStruct((n_batch, STATE_W), F32),
                   jax.ShapeDtypeStruct((rows, A_WIDTH), F32)],
        scratch_shapes=[pltpu.VMEM((rows, 2 * HALF_ST), F32),
                        pltpu.VMEM((n_batch, STATE_W), F32)],
        compiler_params=_params(),
        name="mixer_sample",
    )(x_tm, *consts)


def _ffn_call(h, conv0, p, n_batch, n_t, batch_major):
    if batch_major:
        n_steps = h.shape[1] // n_t
        blk = pl.BlockSpec((n_batch, n_t, D_MODEL), lambda i: (0, i, 0))
    else:
        n_steps = h.shape[0] // (n_t * n_batch)
        blk = pl.BlockSpec((n_t * n_batch, D_MODEL), lambda i: (i, 0))
    halo = (CONV_W - 1) * n_batch
    consts = [conv0, p['g_ffn'], p['w_ffn_in'], p['conv_w'], p['conv_b'], p['w_ffn_out'], p['g_final']]
    return pl.pallas_call(
        functools.partial(_ffn_kernel, n_batch, n_t, n_steps, batch_major),
        grid=(n_steps,),
        in_specs=[blk] + [_const_spec(c.shape) for c in consts],
        out_specs=[blk, pl.BlockSpec((halo, 2 * D_FF), lambda i: (0, 0))],
        out_shape=[jax.ShapeDtypeStruct(h.shape, F32),
                   jax.ShapeDtypeStruct((halo, 2 * D_FF), F32)],
        scratch_shapes=[pltpu.VMEM((halo, 2 * D_FF), F32)],
        compiler_params=_params(),
        name="ffn_prompt" if batch_major else "ffn_sample",
    )(h, *consts)


def _prep_params(g_mix, w_in, g_v, lam_re, lam_im, log_dt, b_re, b_im, c_re, c_im,
                 d_skip, w_glu, b_glu, g_out_a, g_out_b, w_out, g_ffn, w_ffn_in, conv_w, conv_b,
                 w_ffn_out, g_final):
    lr = lam_re.astype(F32)
    li = lam_im.astype(F32)
    dt = jnp.exp(log_dt.astype(F32))[:, None]
    mag = jnp.exp(lr * dt)
    ar = mag * jnp.cos(li * dt)
    ai = mag * jnp.sin(li * dt)
    den = lr * lr + li * li
    fr = ((ar - 1.0) * lr + ai * li) / den
    fi = (ai * lr - (ar - 1.0) * li) / den
    bre = b_re.astype(F32)
    bim = b_im.astype(F32)
    bbr = fr[..., None] * bre - fi[..., None] * bim
    bbi = fr[..., None] * bim + fi[..., None] * bre
    eye = jnp.eye(HALF_GROUPS, dtype=F32)

    def blockdiag_in(m):
        return jnp.einsum('gpc,gh->gchp', m, eye).reshape(HALF_CH, HALF_ST)

    def blockdiag_out(m):
        return jnp.einsum('gcp,gh->gphc', m, eye).reshape(HALF_ST, HALF_CH)

    bsub, csub, ars, ais = [], [], [], []
    for k in range(SSM_HALVES):
        gs = slice(k * HALF_GROUPS, (k + 1) * HALF_GROUPS)
        bsub.append(jnp.concatenate([blockdiag_in(bbr[gs]), blockdiag_in(bbi[gs])], axis=1))
        csub.append(jnp.concatenate([blockdiag_out(c_re[gs].astype(F32)),
                                     -blockdiag_out(c_im[gs].astype(F32))], axis=0))
        ars.append(ar[gs].reshape(1, HALF_ST))
        ais.append(ai[gs].reshape(1, HALF_ST))

    return dict(
        g_mix=g_mix.reshape(1, D_MODEL), w_in=w_in.astype(BF16), g_v=g_v,
        ar=jnp.concatenate(ars, axis=0), ai=jnp.concatenate(ais, axis=0),
        bsub=jnp.stack(bsub).astype(BF16), csub=jnp.stack(csub).astype(BF16),
        d_skip=d_skip.reshape(1, SSM_WIDTH), w_glu=w_glu.astype(BF16), b_glu=b_glu.reshape(1, SSM_WIDTH),
        g_out_a=g_out_a.reshape(1, A_WIDTH), g_out_b=g_out_b.reshape(1, SSM_WIDTH),
        w_out=w_out.astype(BF16), g_ffn=g_ffn.reshape(1, D_MODEL),
        w_ffn_in=w_ffn_in.astype(BF16), conv_w=conv_w, conv_b=conv_b.reshape(1, 2 * D_FF),
        w_ffn_out=w_ffn_out.astype(BF16), g_final=g_final.reshape(1, D_MODEL))


def _state_in(s_re, s_im):
    nb = s_re.shape[0]
    parts = []
    for k in range(SSM_HALVES):
        gs = slice(k * HALF_GROUPS, (k + 1) * HALF_GROUPS)
        parts += [s_re[:, gs].reshape(nb, HALF_ST), s_im[:, gs].reshape(nb, HALF_ST)]
    return jnp.concatenate(parts, axis=1)


def _state_out(st):
    nb = st.shape[0]
    st = st.reshape(nb, SSM_HALVES, 2, HALF_GROUPS, SSM_STATE)
    return (st[:, :, 0].reshape(nb, SSM_GROUPS, SSM_STATE),
            st[:, :, 1].reshape(nb, SSM_GROUPS, SSM_STATE))


def _conv_state_in(conv_buf):
    nb = conv_buf.shape[0]
    return jnp.transpose(conv_buf, (1, 0, 2)).reshape((CONV_W - 1) * nb, 2 * D_FF)


def _conv_state_out(rows_tm, nb):
    return jnp.transpose(rows_tm.reshape(CONV_W - 1, nb, 2 * D_FF), (1, 0, 2))


def kernel(x_prompt, x_sample, state_ssm_re, state_ssm_im, state_conv, g_mix, w_in, g_v, w_s, b_s,
           lam_re, lam_im, log_dt, b_re, b_im, c_re, c_im, d_skip, w_glu, b_glu, g_out_a, g_out_b,
           w_out, g_ffn, w_ffn_in, conv_w, conv_b, w_ffn_out, g_final):
    p = _prep_params(g_mix, w_in, g_v, lam_re, lam_im, log_dt, b_re, b_im, c_re, c_im, d_skip,
                     w_glu, b_glu, g_out_a, g_out_b, w_out, g_ffn, w_ffn_in, conv_w, conv_b,
                     w_ffn_out, g_final)
    tril = jnp.tril(w_s.astype(F32))

    nbp = x_prompt.shape[0]
    zero_state = jnp.zeros((nbp, SSM_GROUPS, SSM_STATE), state_ssm_re.dtype)
    zero_conv = jnp.zeros((nbp, CONV_W - 1, 2 * D_FF), state_conv.dtype)
    bs_p = jnp.broadcast_to(jnp.transpose(b_s)[:, :, None], (CHUNK, A_HEADS, A_HEAD_DIM)).reshape(CHUNK, A_WIDTH)
    h_p, st_p = _mixer_prompt_call(x_prompt, _state_in(zero_state, zero_state), p, tril.astype(BF16), bs_p)
    y_p, conv_p = _ffn_call(h_p, _conv_state_in(zero_conv), p, nbp, FFN_PROMPT_STEPS, True)
    re_p, im_p = _state_out(st_p)
    conv_p = _conv_state_out(conv_p, nbp)

    nbs, nls, _ = x_sample.shape
    x_tm = jnp.transpose(x_sample, (1, 0, 2)).reshape(nls * nbs, D_MODEL)
    ws_s = jnp.broadcast_to(tril[:, :nls, :nls].reshape(A_HEADS * nls * nls, 1), (A_HEADS * nls * nls, A_HEAD_DIM))
    bs_s = jnp.broadcast_to(b_s[:, :nls].reshape(A_HEADS * nls, 1), (A_HEADS * nls, A_HEAD_DIM))
    h_s, st_s, v_s = _mixer_sample_call(x_tm, _state_in(state_ssm_re, state_ssm_im), p, ws_s, bs_s, nbs, nls)
    y_s, conv_s = _ffn_call(h_s, _conv_state_in(state_conv), p, nbs, nls, False)
    y_s = jnp.transpose(y_s.reshape(nls, nbs, D_MODEL), (1, 0, 2))
    v_s = jnp.transpose(v_s.reshape(nls, nbs, A_HEADS, A_HEAD_DIM), (1, 0, 2, 3))
    re_s, im_s = _state_out(st_s)
    conv_s = _conv_state_out(conv_s, nbs)
    return (y_p, y_s, v_s, re_p, im_p, conv_p, re_s, im_s, conv_s)
```

```python
import functools
import math

import jax
import jax.numpy as jnp
from jax import lax
from jax.experimental import pallas as pl
from jax.experimental.pallas import tpu as pltpu

D_MODEL = 1024
A_HEADS = 4
A_HEAD_DIM = 128
A_WIDTH = 512
CHUNK = 128
SSM_WIDTH = 512
SSM_GROUP = 16
SSM_GROUPS = 32
SSM_STATE = 64
D_FF = 2816
CONV_W = 3
EPS = 1e-6

SSM_HALVES = 2
HALF_GROUPS = SSM_GROUPS // SSM_HALVES
HALF_CH = HALF_GROUPS * SSM_GROUP
HALF_ST = HALF_GROUPS * SSM_STATE
STATE_W = SSM_HALVES * 2 * HALF_ST

FF_CHUNK = 256
FF_NCHUNK = D_FF // FF_CHUNK
FFN_PROMPT_STEPS = 32

SUBLANES = 8
SCAN_LANES = 512
VMEM_LIMIT_BYTES = 60 * 1024 * 1024

BF16 = jnp.bfloat16
F32 = jnp.float32


def _rms(x, g):
    ms = jnp.mean(x * x, axis=-1, keepdims=True)
    return x * lax.rsqrt(ms + EPS) * g


def _gelu(x):
    c = math.sqrt(2.0 / math.pi)
    return x * (0.5 * (1.0 + jnp.tanh(c * (x + 0.044715 * (x * x * x)))))


def _dot(a, b):
    return jnp.dot(a, b, preferred_element_type=F32)


def _to_time_major(x, n_batch, n_t):
    w = x.shape[-1]
    return jnp.swapaxes(x.reshape(n_batch, n_t, w), 0, 1).reshape(n_t * n_batch, w)


def _to_batch_major(x, n_batch, n_t):
    w = x.shape[-1]
    return jnp.swapaxes(x.reshape(n_t, n_batch, w), 0, 1)


def _scan_inplace(u_scr, hst_ref, ar_ref, ai_ref, k, n_batch, n_t):
    base = k * 2 * HALF_ST

    def row_chunk(rc, _):
        r0 = pl.multiple_of(rc * SUBLANES, SUBLANES)
        for c in range(HALF_ST // SCAN_LANES):
            lo = c * SCAN_LANES
            re_l = slice(lo, lo + SCAN_LANES)
            im_l = slice(HALF_ST + lo, HALF_ST + lo + SCAN_LANES)
            ar = jnp.broadcast_to(ar_ref[k:k + 1, re_l], (SUBLANES, SCAN_LANES))
            ai = jnp.broadcast_to(ai_ref[k:k + 1, re_l], (SUBLANES, SCAN_LANES))
            hr0 = hst_ref[pl.ds(r0, SUBLANES), base + lo:base + lo + SCAN_LANES]
            hi0 = hst_ref[pl.ds(r0, SUBLANES), base + HALF_ST + lo:base + HALF_ST + lo + SCAN_LANES]

            def step(t, carry):
                hr, hi = carry
                row = pl.multiple_of(t * n_batch + r0, SUBLANES)
                ur = u_scr[pl.ds(row, SUBLANES), re_l]
                ui = u_scr[pl.ds(row, SUBLANES), im_l]
                nhr = ar * hr - ai * hi + ur
                nhi = ar * hi + ai * hr + ui
                u_scr[pl.ds(row, SUBLANES), re_l] = nhr
                u_scr[pl.ds(row, SUBLANES), im_l] = nhi
                return nhr, nhi

            hr, hi = lax.fori_loop(0, n_t, step, (hr0, hi0), unroll=min(n_t, 4))
            hst_ref[pl.ds(r0, SUBLANES), base + lo:base + lo + SCAN_LANES] = hr
            hst_ref[pl.ds(r0, SUBLANES), base + HALF_ST + lo:base + HALF_ST + lo + SCAN_LANES] = hi
        return 0

    lax.fori_loop(0, n_batch // SUBLANES, row_chunk, 0)


def _ssm_glu(xs, n_batch, n_t, u_scr, hst_ref, ar_ref, ai_ref, bsub_ref, csub_ref, dskip_ref,
             wglu_ref, bglu_ref, gob_ref):
    xs_b = xs.astype(BF16)
    y_parts = []
    for k in range(SSM_HALVES):
        u_scr[...] = _dot(xs_b[:, k * HALF_CH:(k + 1) * HALF_CH], bsub_ref[k])
        _scan_inplace(u_scr, hst_ref, ar_ref, ai_ref, k, n_batch, n_t)
        y_parts.append(_dot(u_scr[...].astype(BF16), csub_ref[k]))
    ys = jnp.concatenate(y_parts, axis=-1) + dskip_ref[...] * xs
    g = _gelu(ys)
    z = _dot(g.astype(BF16), wglu_ref[...]) + bglu_ref[...]
    return _rms(g * jax.nn.sigmoid(z), gob_ref[...])


def _mixer_prompt_kernel(n_batch, n_steps,
                         x_ref, h0_ref, gmix_ref, win_ref, gv_ref, ws_ref, bs_ref,
                         ar_ref, ai_ref, bsub_ref, csub_ref, dskip_ref,
                         wglu_ref, bglu_ref, goa_ref, gob_ref, wout_ref,
                         hout_ref, st_ref, u_scr, hst_ref):
    n_t = CHUNK
    rows = n_batch * n_t
    i = pl.program_id(0)

    @pl.when(i == 0)
    def _():
        hst_ref[...] = h0_ref[...]

    x = x_ref[...].reshape(rows, D_MODEL)
    n1 = _rms(x, gmix_ref[...]).astype(BF16)
    proj = _dot(n1, win_ref[...])

    a_heads = []
    for h in range(A_HEADS):
        hs = slice(h * A_HEAD_DIM, (h + 1) * A_HEAD_DIM)
        vn = _rms(proj[:, A_WIDTH + h * A_HEAD_DIM:A_WIDTH + (h + 1) * A_HEAD_DIM],
                  gv_ref[h:h + 1, :]).astype(BF16)
        v_wide = jnp.concatenate([vn[b * n_t:(b + 1) * n_t, :] for b in range(n_batch)], axis=-1)
        s_wide = _dot(ws_ref[h], v_wide)
        bias = bs_ref[:, hs]
        s = jnp.concatenate([s_wide[:, b * A_HEAD_DIM:(b + 1) * A_HEAD_DIM] + bias
                             for b in range(n_batch)], axis=0)
        a_heads.append(proj[:, hs] * s)
    an = _rms(jnp.concatenate(a_heads, axis=-1), goa_ref[...])

    xs = _to_time_major(proj[:, 2 * A_WIDTH:], n_batch, n_t)
    bn_tm = _ssm_glu(xs, n_batch, n_t, u_scr, hst_ref, ar_ref, ai_ref, bsub_ref, csub_ref,
                     dskip_ref, wglu_ref, bglu_ref, gob_ref)
    bn = _to_batch_major(bn_tm, n_batch, n_t).reshape(rows, SSM_WIDTH)

    mix = jnp.concatenate([an, bn], axis=-1).astype(BF16)
    hout_ref[...] = (x + _dot(mix, wout_ref[...])).reshape(n_batch, n_t, D_MODEL)

    @pl.when(i == n_steps - 1)
    def _():
        st_ref[...] = hst_ref[...]


def _mixer_sample_kernel(n_batch, n_t,
                         x_ref, h0_ref, gmix_ref, win_ref, gv_ref, ws_ref, bs_ref,
                         ar_ref, ai_ref, bsub_ref, csub_ref, dskip_ref,
                         wglu_ref, bglu_ref, goa_ref, gob_ref, wout_ref,
                         hout_ref, st_ref, v_ref, u_scr, hst_ref):
    hst_ref[...] = h0_ref[...]
    x = x_ref[...]
    n1 = _rms(x, gmix_ref[...]).astype(BF16)
    proj = _dot(n1, win_ref[...])

    a_heads = []
    for h in range(A_HEADS):
        hs = slice(h * A_HEAD_DIM, (h + 1) * A_HEAD_DIM)
        vn = _rms(proj[:, A_WIDTH + h * A_HEAD_DIM:A_WIDTH + (h + 1) * A_HEAD_DIM], gv_ref[h:h + 1, :])
        v_ref[:, hs] = vn
        slabs = []
        for t in range(n_t):
            s = jnp.broadcast_to(bs_ref[h * n_t + t:h * n_t + t + 1, :], (n_batch, A_HEAD_DIM))
            for src in range(t + 1):
                r = (h * n_t + t) * n_t + src
                s = s + ws_ref[r:r + 1, :] * vn[src * n_batch:(src + 1) * n_batch, :]
            slabs.append(proj[t * n_batch:(t + 1) * n_batch, hs] * s)
        a_heads.append(jnp.concatenate(slabs, axis=0))
    an = _rms(jnp.concatenate(a_heads, axis=-1), goa_ref[...])

    bn = _ssm_glu(proj[:, 2 * A_WIDTH:], n_batch, n_t, u_scr, hst_ref, ar_ref, ai_ref, bsub_ref,
                  csub_ref, dskip_ref, wglu_ref, bglu_ref, gob_ref)
    mix = jnp.concatenate([an, bn], axis=-1).astype(BF16)
    hout_ref[...] = x + _dot(mix, wout_ref[...])
    st_ref[...] = hst_ref[...]


def _ffn_kernel(n_batch, n_t, n_steps, batch_major,
                h_ref, conv0_ref, gffn_ref, wfi_ref, cw_ref, cb_ref, wfo_ref, gfin_ref,
                y_ref, convout_ref, carry_scr):
    rows = n_t * n_batch
    halo = (CONV_W - 1) * n_batch
    i = pl.program_id(0)

    @pl.when(i == 0)
    def _():
        carry_scr[...] = conv0_ref[...]

    if batch_major:
        h = _to_time_major(h_ref[...].reshape(rows, D_MODEL), n_batch, n_t)
    else:
        h = h_ref[...]
    n2 = _rms(h, gffn_ref[...]).astype(BF16)
    def col_slices(j):
        return [slice(c0, c0 + FF_CHUNK) for c0 in (j * FF_CHUNK, D_FF + j * FF_CHUNK)]

    def up_proj(j):
        return [jnp.dot(n2, wfi_ref[:, cs], preferred_element_type=F32) for cs in col_slices(j)]

    acc = jnp.zeros((rows, D_MODEL), F32)
    ups_next = up_proj(0)
    for j in range(FF_NCHUNK):
        ups = ups_next
        if j + 1 < FF_NCHUNK:
            ups_next = up_proj(j + 1)
        conv = []
        for cs, up in zip(col_slices(j), ups):
            padded = jnp.concatenate([carry_scr[:, cs], up], axis=0)
            carry_scr[:, cs] = up[rows - halo:rows, :]
            out = cb_ref[:, cs]
            for tap in range(CONV_W):
                out = out + cw_ref[tap:tap + 1, cs] * padded[tap * n_batch:tap * n_batch + rows, :]
            conv.append(out)
        act = _gelu(conv[0]) * conv[1]
        acc = acc + jnp.dot(act.astype(BF16), wfo_ref[j * FF_CHUNK:(j + 1) * FF_CHUNK, :],
                            preferred_element_type=F32)
    y = _rms(h + acc, gfin_ref[...])
    if batch_major:
        y_ref[...] = _to_batch_major(y, n_batch, n_t)
    else:
        y_ref[...] = y

    @pl.when(i == n_steps - 1)
    def _():
        convout_ref[...] = carry_scr[...]


def _const_spec(shape):
    zeros = (0,) * len(shape)
    return pl.BlockSpec(shape, lambda i: zeros, pipeline_mode=pl.Buffered(1))


def _params():
    return pltpu.CompilerParams(dimension_semantics=("arbitrary",), vmem_limit_bytes=VMEM_LIMIT_BYTES)


def _mixer_consts(h0, p, ws, bs):
    return [h0, p['g_mix'], p['w_in'], p['g_v'], ws, bs, p['ar'], p['ai'],
            p['bsub'], p['csub'], p['d_skip'], p['w_glu'], p['b_glu'],
            p['g_out_a'], p['g_out_b'], p['w_out']]


def _mixer_prompt_call(x, h0, p, ws, bs):
    n_batch, n_l, _ = x.shape
    n_steps = n_l // CHUNK
    blk = pl.BlockSpec((n_batch, CHUNK, D_MODEL), lambda i: (0, i, 0))
    consts = _mixer_consts(h0, p, ws, bs)
    return pl.pallas_call(
        functools.partial(_mixer_prompt_kernel, n_batch, n_steps),
        grid=(n_steps,),
        in_specs=[blk] + [_const_spec(c.shape) for c in consts],
        out_specs=[blk, pl.BlockSpec((n_batch, STATE_W), lambda i: (0, 0))],
        out_shape=[jax.ShapeDtypeStruct(x.shape, F32),
                   jax.ShapeDtypeStruct((n_batch, STATE_W), F32)],
        scratch_shapes=[pltpu.VMEM((n_batch * CHUNK, 2 * HALF_ST), F32),
                        pltpu.VMEM((n_batch, STATE_W), F32)],
        compiler_params=_params(),
        name="mixer_prompt",
    )(x, *consts)


def _mixer_sample_call(x_tm, h0, p, ws, bs, n_batch, n_t):
    rows = n_batch * n_t
    consts = _mixer_consts(h0, p, ws, bs)
    return pl.pallas_call(
        functools.partial(_mixer_sample_kernel, n_batch, n_t),
        grid=(1,),
        in_specs=[_const_spec(x_tm.shape)] + [_const_spec(c.shape) for c in consts],
        out_specs=[pl.BlockSpec((rows, D_MODEL), lambda i: (0, 0)),
                   pl.BlockSpec((n_batch, STATE_W), lambda i: (0, 0)),
                   pl.BlockSpec((rows, A_WIDTH), lambda i: (0, 0))],
        out_shape=[jax.ShapeDtypeStruct((rows, D_MODEL), F32),
                   jax.ShapeDtypeStruct((n_batch, STATE_W), F32),
                   jax.ShapeDtypeStruct((rows, A_WIDTH), F32)],
        scratch_shapes=[pltpu.VMEM((rows, 2 * HALF_ST), F32),
                        pltpu.VMEM((n_batch, STATE_W), F32)],
        compiler_params=_params(),
        name="mixer_sample",
    )(x_tm, *consts)


def _ffn_call(h, conv0, p, n_batch, n_t, batch_major):
    if batch_major:
        n_steps = h.shape[1] // n_t
        blk = pl.BlockSpec((n_batch, n_t, D_MODEL), lambda i: (0, i, 0))
    else:
        n_steps = h.shape[0] // (n_t * n_batch)
        blk = pl.BlockSpec((n_t * n_batch, D_MODEL), lambda i: (i, 0))
    halo = (CONV_W - 1) * n_batch
    consts = [conv0, p['g_ffn'], p['w_ffn_in'], p['conv_w'], p['conv_b'], p['w_ffn_out'], p['g_final']]
    return pl.pallas_call(
        functools.partial(_ffn_kernel, n_batch, n_t, n_steps, batch_major),
        grid=(n_steps,),
        in_specs=[blk] + [_const_spec(c.shape) for c in consts],
        out_specs=[blk, pl.BlockSpec((halo, 2 * D_FF), lambda i: (0, 0))],
        out_shape=[jax.ShapeDtypeStruct(h.shape, F32),
                   jax.ShapeDtypeStruct((halo, 2 * D_FF), F32)],
        scratch_shapes=[pltpu.VMEM((halo, 2 * D_FF), F32)],
        compiler_params=_params(),
        name="ffn_prompt" if batch_major else "ffn_sample",
    )(h, *consts)


def _prep_params(g_mix, w_in, g_v, lam_re, lam_im, log_dt, b_re, b_im, c_re, c_im,
                 d_skip, w_glu, b_glu, g_out_a, g_out_b, w_out, g_ffn, w_ffn_in, conv_w, conv_b,
                 w_ffn_out, g_final):
    lr = lam_re.astype(F32)
    li = lam_im.astype(F32)
    dt = jnp.exp(log_dt.astype(F32))[:, None]
    mag = jnp.exp(lr * dt)
    ar = mag * jnp.cos(li * dt)
    ai = mag * jnp.sin(li * dt)
    den = lr * lr + li * li
    fr = ((ar - 1.0) * lr + ai * li) / den
    fi = (ai * lr - (ar - 1.0) * li) / den
    bre = b_re.astype(F32)
    bim = b_im.astype(F32)
    bbr = fr[..., None] * bre - fi[..., None] * bim
    bbi = fr[..., None] * bim + fi[..., None] * bre
    eye = jnp.eye(HALF_GROUPS, dtype=F32)

    def blockdiag_in(m):
        return jnp.einsum('gpc,gh->gchp', m, eye).reshape(HALF_CH, HALF_ST)

    def blockdiag_out(m):
        return jnp.einsum('gcp,gh->gphc', m, eye).reshape(HALF_ST, HALF_CH)

    bsub, csub, ars, ais = [], [], [], []
    for k in range(SSM_HALVES):
        gs = slice(k * HALF_GROUPS, (k + 1) * HALF_GROUPS)
        bsub.append(jnp.concatenate([blockdiag_in(bbr[gs]), blockdiag_in(bbi[gs])], axis=1))
        csub.append(jnp.concatenate([blockdiag_out(c_re[gs].astype(F32)),
                                     -blockdiag_out(c_im[gs].astype(F32))], axis=0))
        ars.append(ar[gs].reshape(1, HALF_ST))
        ais.append(ai[gs].reshape(1, HALF_ST))

    return dict(
        g_mix=g_mix.reshape(1, D_MODEL), w_in=w_in.astype(BF16), g_v=g_v,
        ar=jnp.concatenate(ars, axis=0), ai=jnp.concatenate(ais, axis=0),
        bsub=jnp.stack(bsub).astype(BF16), csub=jnp.stack(csub).astype(BF16),
        d_skip=d_skip.reshape(1, SSM_WIDTH), w_glu=w_glu.astype(BF16), b_glu=b_glu.reshape(1, SSM_WIDTH),
        g_out_a=g_out_a.reshape(1, A_WIDTH), g_out_b=g_out_b.reshape(1, SSM_WIDTH),
        w_out=w_out.astype(BF16), g_ffn=g_ffn.reshape(1, D_MODEL),
        w_ffn_in=w_ffn_in.astype(BF16), conv_w=conv_w, conv_b=conv_b.reshape(1, 2 * D_FF),
        w_ffn_out=w_ffn_out.astype(BF16), g_final=g_final.reshape(1, D_MODEL))


def _state_in(s_re, s_im):
    nb = s_re.shape[0]
    parts = []
    for k in range(SSM_HALVES):
        gs = slice(k * HALF_GROUPS, (k + 1) * HALF_GROUPS)
        parts += [s_re[:, gs].reshape(nb, HALF_ST), s_im[:, gs].reshape(nb, HALF_ST)]
    return jnp.concatenate(parts, axis=1)


def _state_out(st):
    nb = st.shape[0]
    st = st.reshape(nb, SSM_HALVES, 2, HALF_GROUPS, SSM_STATE)
    return (st[:, :, 0].reshape(nb, SSM_GROUPS, SSM_STATE),
            st[:, :, 1].reshape(nb, SSM_GROUPS, SSM_STATE))


def _conv_state_in(conv_buf):
    nb = conv_buf.shape[0]
    return jnp.transpose(conv_buf, (1, 0, 2)).reshape((CONV_W - 1) * nb, 2 * D_FF)


def _conv_state_out(rows_tm, nb):
    return jnp.transpose(rows_tm.reshape(CONV_W - 1, nb, 2 * D_FF), (1, 0, 2))


def kernel(x_prompt, x_sample, state_ssm_re, state_ssm_im, state_conv, g_mix, w_in, g_v, w_s, b_s,
           lam_re, lam_im, log_dt, b_re, b_im, c_re, c_im, d_skip, w_glu, b_glu, g_out_a, g_out_b,
           w_out, g_ffn, w_ffn_in, conv_w, conv_b, w_ffn_out, g_final):
    p = _prep_params(g_mix, w_in, g_v, lam_re, lam_im, log_dt, b_re, b_im, c_re, c_im, d_skip,
                     w_glu, b_glu, g_out_a, g_out_b, w_out, g_ffn, w_ffn_in, conv_w, conv_b,
                     w_ffn_out, g_final)
    tril = jnp.tril(w_s.astype(F32))

    nbp = x_prompt.shape[0]
    zero_state = jnp.zeros((nbp, SSM_GROUPS, SSM_STATE), state_ssm_re.dtype)
    zero_conv = jnp.zeros((nbp, CONV_W - 1, 2 * D_FF), state_conv.dtype)
    bs_p = jnp.broadcast_to(jnp.transpose(b_s)[:, :, None], (CHUNK, A_HEADS, A_HEAD_DIM)).reshape(CHUNK, A_WIDTH)
    h_p, st_p = _mixer_prompt_call(x_prompt, _state_in(zero_state, zero_state), p, tril.astype(BF16), bs_p)
    y_p, conv_p = _ffn_call(h_p, _conv_state_in(zero_conv), p, nbp, FFN_PROMPT_STEPS, True)
    re_p, im_p = _state_out(st_p)
    conv_p = _conv_state_out(conv_p, nbp)

    nbs, nls, _ = x_sample.shape
    x_tm = jnp.transpose(x_sample, (1, 0, 2)).reshape(nls * nbs, D_MODEL)
    ws_s = jnp.broadcast_to(tril[:, :nls, :nls].reshape(A_HEADS * nls * nls, 1), (A_HEADS * nls * nls, A_HEAD_DIM))
    bs_s = jnp.broadcast_to(b_s[:, :nls].reshape(A_HEADS * nls, 1), (A_HEADS * nls, A_HEAD_DIM))
    h_s, st_s, v_s = _mixer_sample_call(x_tm, _state_in(state_ssm_re, state_ssm_im), p, ws_s, bs_s, nbs, nls)
    y_s, conv_s = _ffn_call(h_s, _conv_state_in(state_conv), p, nbs, nls, False)
    y_s = jnp.transpose(y_s.reshape(nls, nbs, D_MODEL), (1, 0, 2))
    v_s = jnp.transpose(v_s.reshape(nls, nbs, A_HEADS, A_HEAD_DIM), (1, 0, 2, 3))
    re_s, im_s = _state_out(st_s)
    conv_s = _conv_state_out(conv_s, nbs)
    return (y_p, y_s, v_s, re_p, im_p, conv_p, re_s, im_s, conv_s)
```

```python
import functools
import math

import jax
import jax.numpy as jnp
from jax import lax
from jax.experimental import pallas as pl
from jax.experimental.pallas import tpu as pltpu

D_MODEL = 1024
A_HEADS = 4
A_HEAD_DIM = 128
A_WIDTH = 512
CHUNK = 128
SSM_WIDTH = 512
SSM_GROUP = 16
SSM_GROUPS = 32
SSM_STATE = 64
D_FF = 2816
CONV_W = 3
EPS = 1e-6

SSM_HALVES = 2
HALF_GROUPS = SSM_GROUPS // SSM_HALVES
HALF_CH = HALF_GROUPS * SSM_GROUP
HALF_ST = HALF_GROUPS * SSM_STATE
STATE_W = SSM_HALVES * 2 * HALF_ST

FF_CHUNK = 256
FF_NCHUNK = D_FF // FF_CHUNK
FFN_PROMPT_STEPS = 64
FFN_SUB_ROWS = 256

SUBLANES = 8
SCAN_LANES = 512
VMEM_LIMIT_BYTES = 60 * 1024 * 1024

BF16 = jnp.bfloat16
F32 = jnp.float32


def _rms(x, g):
    ms = jnp.mean(x * x, axis=-1, keepdims=True)
    return x * lax.rsqrt(ms + EPS) * g


def _gelu(x):
    c = math.sqrt(2.0 / math.pi)
    return x * (0.5 * (1.0 + jnp.tanh(c * (x + 0.044715 * (x * x * x)))))


def _dot(a, b):
    return jnp.dot(a, b, preferred_element_type=F32)


def _to_time_major(x, n_batch, n_t):
    w = x.shape[-1]
    return jnp.swapaxes(x.reshape(n_batch, n_t, w), 0, 1).reshape(n_t * n_batch, w)


def _to_batch_major(x, n_batch, n_t):
    w = x.shape[-1]
    return jnp.swapaxes(x.reshape(n_t, n_batch, w), 0, 1)


def _scan_inplace(u_scr, hst_ref, ar_ref, ai_ref, k, n_batch, n_t):
    base = k * 2 * HALF_ST

    def row_chunk(rc, _):
        r0 = pl.multiple_of(rc * SUBLANES, SUBLANES)
        for c in range(HALF_ST // SCAN_LANES):
            lo = c * SCAN_LANES
            re_l = slice(lo, lo + SCAN_LANES)
            im_l = slice(HALF_ST + lo, HALF_ST + lo + SCAN_LANES)
            ar = jnp.broadcast_to(ar_ref[k:k + 1, re_l], (SUBLANES, SCAN_LANES))
            ai = jnp.broadcast_to(ai_ref[k:k + 1, re_l], (SUBLANES, SCAN_LANES))
            hr0 = hst_ref[pl.ds(r0, SUBLANES), base + lo:base + lo + SCAN_LANES]
            hi0 = hst_ref[pl.ds(r0, SUBLANES), base + HALF_ST + lo:base + HALF_ST + lo + SCAN_LANES]

            def step(t, carry):
                hr, hi = carry
                row = pl.multiple_of(t * n_batch + r0, SUBLANES)
                ur = u_scr[pl.ds(row, SUBLANES), re_l]
                ui = u_scr[pl.ds(row, SUBLANES), im_l]
                nhr = ar * hr - ai * hi + ur
                nhi = ar * hi + ai * hr + ui
                u_scr[pl.ds(row, SUBLANES), re_l] = nhr
                u_scr[pl.ds(row, SUBLANES), im_l] = nhi
                return nhr, nhi

            hr, hi = lax.fori_loop(0, n_t, step, (hr0, hi0), unroll=min(n_t, 4))
            hst_ref[pl.ds(r0, SUBLANES), base + lo:base + lo + SCAN_LANES] = hr
            hst_ref[pl.ds(r0, SUBLANES), base + HALF_ST + lo:base + HALF_ST + lo + SCAN_LANES] = hi
        return 0

    lax.fori_loop(0, n_batch // SUBLANES, row_chunk, 0)


def _ssm_glu(xs, n_batch, n_t, u_scr, hst_ref, ar_ref, ai_ref, bsub_ref, csub_ref, dskip_ref,
             wglu_ref, bglu_ref, gob_ref):
    xs_b = xs.astype(BF16)
    y_parts = []
    for k in range(SSM_HALVES):
        u_scr[...] = _dot(xs_b[:, k * HALF_CH:(k + 1) * HALF_CH], bsub_ref[k])
        _scan_inplace(u_scr, hst_ref, ar_ref, ai_ref, k, n_batch, n_t)
        y_parts.append(_dot(u_scr[...].astype(BF16), csub_ref[k]))
    ys = jnp.concatenate(y_parts, axis=-1) + dskip_ref[...] * xs
    g = _gelu(ys)
    z = _dot(g.astype(BF16), wglu_ref[...]) + bglu_ref[...]
    return _rms(g * jax.nn.sigmoid(z), gob_ref[...])


def _mixer_prompt_kernel(n_batch, n_steps,
                         x_ref, h0_ref, gmix_ref, win_ref, gv_ref, ws_ref, bs_ref,
                         ar_ref, ai_ref, bsub_ref, csub_ref, dskip_ref,
                         wglu_ref, bglu_ref, goa_ref, gob_ref, wout_ref,
                         hout_ref, st_ref, u_scr, hst_ref):
    n_t = CHUNK
    rows = n_batch * n_t
    i = pl.program_id(0)

    @pl.when(i == 0)
    def _():
        hst_ref[...] = h0_ref[...]

    x = x_ref[...].reshape(rows, D_MODEL)
    n1 = _rms(x, gmix_ref[...]).astype(BF16)
    proj = _dot(n1, win_ref[...])

    a_heads = []
    for h in range(A_HEADS):
        hs = slice(h * A_HEAD_DIM, (h + 1) * A_HEAD_DIM)
        vn = _rms(proj[:, A_WIDTH + h * A_HEAD_DIM:A_WIDTH + (h + 1) * A_HEAD_DIM],
                  gv_ref[h:h + 1, :]).astype(BF16)
        v_wide = jnp.concatenate([vn[b * n_t:(b + 1) * n_t, :] for b in range(n_batch)], axis=-1)
        s_wide = _dot(ws_ref[h], v_wide)
        bias = bs_ref[:, hs]
        s = jnp.concatenate([s_wide[:, b * A_HEAD_DIM:(b + 1) * A_HEAD_DIM] + bias
                             for b in range(n_batch)], axis=0)
        a_heads.append(proj[:, hs] * s)
    an = _rms(jnp.concatenate(a_heads, axis=-1), goa_ref[...])

    xs = _to_time_major(proj[:, 2 * A_WIDTH:], n_batch, n_t)
    bn_tm = _ssm_glu(xs, n_batch, n_t, u_scr, hst_ref, ar_ref, ai_ref, bsub_ref, csub_ref,
                     dskip_ref, wglu_ref, bglu_ref, gob_ref)
    bn = _to_batch_major(bn_tm, n_batch, n_t).reshape(rows, SSM_WIDTH)

    mix = jnp.concatenate([an, bn], axis=-1).astype(BF16)
    hout_ref[...] = (x + _dot(mix, wout_ref[...])).reshape(n_batch, n_t, D_MODEL)

    @pl.when(i == n_steps - 1)
    def _():
        st_ref[...] = hst_ref[...]


def _mixer_sample_kernel(n_batch, n_t,
                         x_ref, h0_ref, gmix_ref, win_ref, gv_ref, ws_ref, bs_ref,
                         ar_ref, ai_ref, bsub_ref, csub_ref, dskip_ref,
                         wglu_ref, bglu_ref, goa_ref, gob_ref, wout_ref,
                         hout_ref, st_ref, v_ref, u_scr, hst_ref):
    hst_ref[...] = h0_ref[...]
    x = x_ref[...]
    n1 = _rms(x, gmix_ref[...]).astype(BF16)
    proj = _dot(n1, win_ref[...])

    a_heads = []
    for h in range(A_HEADS):
        hs = slice(h * A_HEAD_DIM, (h + 1) * A_HEAD_DIM)
        vn = _rms(proj[:, A_WIDTH + h * A_HEAD_DIM:A_WIDTH + (h + 1) * A_HEAD_DIM], gv_ref[h:h + 1, :])
        v_ref[:, hs] = vn
        slabs = []
        for t in range(n_t):
            s = jnp.broadcast_to(bs_ref[h * n_t + t:h * n_t + t + 1, :], (n_batch, A_HEAD_DIM))
            for src in range(t + 1):
                r = (h * n_t + t) * n_t + src
                s = s + ws_ref[r:r + 1, :] * vn[src * n_batch:(src + 1) * n_batch, :]
            slabs.append(proj[t * n_batch:(t + 1) * n_batch, hs] * s)
        a_heads.append(jnp.concatenate(slabs, axis=0))
    an = _rms(jnp.concatenate(a_heads, axis=-1), goa_ref[...])

    bn = _ssm_glu(proj[:, 2 * A_WIDTH:], n_batch, n_t, u_scr, hst_ref, ar_ref, ai_ref, bsub_ref,
                  csub_ref, dskip_ref, wglu_ref, bglu_ref, gob_ref)
    mix = jnp.concatenate([an, bn], axis=-1).astype(BF16)
    hout_ref[...] = x + _dot(mix, wout_ref[...])
    st_ref[...] = hst_ref[...]


def _ffn_kernel(n_batch, n_t, n_steps, batch_major,
                h_ref, conv0_ref, gffn_ref, wfi_ref, cw_ref, cb_ref, wfo_ref, gfin_ref,
                y_ref, convout_ref, carry_scr):
    rows = n_t * n_batch
    halo = (CONV_W - 1) * n_batch
    i = pl.program_id(0)

    @pl.when(i == 0)
    def _():
        carry_scr[...] = conv0_ref[...]

    n_sub = rows // FFN_SUB_ROWS
    sub_t = FFN_SUB_ROWS // n_batch
    hs, n2s = [], []
    for s in range(n_sub):
        if batch_major:
            h_s = _to_time_major(h_ref[:, s * sub_t:(s + 1) * sub_t, :].reshape(FFN_SUB_ROWS, D_MODEL),
                                 n_batch, sub_t)
        else:
            h_s = h_ref[s * FFN_SUB_ROWS:(s + 1) * FFN_SUB_ROWS, :]
        hs.append(h_s)
        n2s.append(_rms(h_s, gffn_ref[...]).astype(BF16))

    def col_slices(j):
        return [slice(c0, c0 + FF_CHUNK) for c0 in (j * FF_CHUNK, D_FF + j * FF_CHUNK)]

    def up_proj(j, s):
        return [jnp.dot(n2s[s], wfi_ref[:, cs], preferred_element_type=F32) for cs in col_slices(j)]

    order = [(j, s) for j in range(FF_NCHUNK) for s in range(n_sub)]
    accs = [jnp.zeros((FFN_SUB_ROWS, D_MODEL), F32) for _ in range(n_sub)]
    tails = [None, None]
    ups_next = up_proj(*order[0])
    for idx, (j, s) in enumerate(order):
        ups = ups_next
        if idx + 1 < len(order):
            ups_next = up_proj(*order[idx + 1])
        conv = []
        for ci, (cs, up) in enumerate(zip(col_slices(j), ups)):
            tail = carry_scr[:, cs] if s == 0 else tails[ci]
            padded = jnp.concatenate([tail, up], axis=0)
            tails[ci] = padded[FFN_SUB_ROWS:FFN_SUB_ROWS + halo, :]
            if s == n_sub - 1:
                carry_scr[:, cs] = tails[ci]
            out = cb_ref[:, cs]
            for tap in range(CONV_W):
                out = out + cw_ref[tap:tap + 1, cs] * padded[tap * n_batch:tap * n_batch + FFN_SUB_ROWS, :]
            conv.append(out)
        act = _gelu(conv[0]) * conv[1]
        accs[s] = accs[s] + jnp.dot(act.astype(BF16), wfo_ref[j * FF_CHUNK:(j + 1) * FF_CHUNK, :],
                                    preferred_element_type=F32)
    for s in range(n_sub):
        y_s = _rms(hs[s] + accs[s], gfin_ref[...])
        if batch_major:
            y_ref[:, s * sub_t:(s + 1) * sub_t, :] = _to_batch_major(y_s, n_batch, sub_t)
        else:
            y_ref[s * FFN_SUB_ROWS:(s + 1) * FFN_SUB_ROWS, :] = y_s

    @pl.when(i == n_steps - 1)
    def _():
        convout_ref[...] = carry_scr[...]


def _const_spec(shape):
    zeros = (0,) * len(shape)
    return pl.BlockSpec(shape, lambda i: zeros, pipeline_mode=pl.Buffered(1))


def _params():
    return pltpu.CompilerParams(dimension_semantics=("arbitrary",), vmem_limit_bytes=VMEM_LIMIT_BYTES)


def _mixer_consts(h0, p, ws, bs):
    return [h0, p['g_mix'], p['w_in'], p['g_v'], ws, bs, p['ar'], p['ai'],
            p['bsub'], p['csub'], p['d_skip'], p['w_glu'], p['b_glu'],
            p['g_out_a'], p['g_out_b'], p['w_out']]


def _mixer_prompt_call(x, h0, p, ws, bs):
    n_batch, n_l, _ = x.shape
    n_steps = n_l // CHUNK
    blk = pl.BlockSpec((n_batch, CHUNK, D_MODEL), lambda i: (0, i, 0))
    consts = _mixer_consts(h0, p, ws, bs)
    return pl.pallas_call(
        functools.partial(_mixer_prompt_kernel, n_batch, n_steps),
        grid=(n_steps,),
        in_specs=[blk] + [_const_spec(c.shape) for c in consts],
        out_specs=[blk, pl.BlockSpec((n_batch, STATE_W), lambda i: (0, 0))],
        out_shape=[jax.ShapeDtypeStruct(x.shape, F32),
                   jax.ShapeDtypeStruct((n_batch, STATE_W), F32)],
        scratch_shapes=[pltpu.VMEM((n_batch * CHUNK, 2 * HALF_ST), F32),
                        pltpu.VMEM((n_batch, STATE_W), F32)],
        compiler_params=_params(),
        name="mixer_prompt",
    )(x, *consts)


def _mixer_sample_call(x_tm, h0, p, ws, bs, n_batch, n_t):
    rows = n_batch * n_t
    consts = _mixer_consts(h0, p, ws, bs)
    return pl.pallas_call(
        functools.partial(_mixer_sample_kernel, n_batch, n_t),
        grid=(1,),
        in_specs=[_const_spec(x_tm.shape)] + [_const_spec(c.shape) for c in consts],
        out_specs=[pl.BlockSpec((rows, D_MODEL), lambda i: (0, 0)),
                   pl.BlockSpec((n_batch, STATE_W), lambda i: (0, 0)),
                   pl.BlockSpec((rows, A_WIDTH), lambda i: (0, 0))],
        out_shape=[jax.ShapeDtypeStruct((rows, D_MODEL), F32),
                   jax.ShapeDtypeStruct((n_batch, STATE_W), F32),
                   jax.ShapeDtypeStruct((rows, A_WIDTH), F32)],
        scratch_shapes=[pltpu.VMEM((rows, 2 * HALF_ST), F32),
                        pltpu.VMEM((n_batch, STATE_W), F32)],
        compiler_params=_params(),
        name="mixer_sample",
    )(x_tm, *consts)


def _ffn_call(h, conv0, p, n_batch, n_t, batch_major):
    if batch_major:
        n_steps = h.shape[1] // n_t
        blk = pl.BlockSpec((n_batch, n_t, D_MODEL), lambda i: (0, i, 0))
    else:
        n_steps = h.shape[0] // (n_t * n_batch)
        blk = pl.BlockSpec((n_t * n_batch, D_MODEL), lambda i: (i, 0))
    halo = (CONV_W - 1) * n_batch
    consts = [conv0, p['g_ffn'], p['w_ffn_in'], p['conv_w'], p['conv_b'], p['w_ffn_out'], p['g_final']]
    return pl.pallas_call(
        functools.partial(_ffn_kernel, n_batch, n_t, n_steps, batch_major),
        grid=(n_steps,),
        in_specs=[blk] + [_const_spec(c.shape) for c in consts],
        out_specs=[blk, pl.BlockSpec((halo, 2 * D_FF), lambda i: (0, 0))],
        out_shape=[jax.ShapeDtypeStruct(h.shape, F32),
                   jax.ShapeDtypeStruct((halo, 2 * D_FF), F32)],
        scratch_shapes=[pltpu.VMEM((halo, 2 * D_FF), F32)],
        compiler_params=_params(),
        name="ffn_prompt" if batch_major else "ffn_sample",
    )(h, *consts)


def _prep_params(g_mix, w_in, g_v, lam_re, lam_im, log_dt, b_re, b_im, c_re, c_im,
                 d_skip, w_glu, b_glu, g_out_a, g_out_b, w_out, g_ffn, w_ffn_in, conv_w, conv_b,
                 w_ffn_out, g_final):
    lr = lam_re.astype(F32)
    li = lam_im.astype(F32)
    dt = jnp.exp(log_dt.astype(F32))[:, None]
    mag = jnp.exp(lr * dt)
    ar = mag * jnp.cos(li * dt)
    ai = mag * jnp.sin(li * dt)
    den = lr * lr + li * li
    fr = ((ar - 1.0) * lr + ai * li) / den
    fi = (ai * lr - (ar - 1.0) * li) / den
    bre = b_re.astype(F32)
    bim = b_im.astype(F32)
    bbr = fr[..., None] * bre - fi[..., None] * bim
    bbi = fr[..., None] * bim + fi[..., None] * bre
    eye = jnp.eye(HALF_GROUPS, dtype=F32)

    def blockdiag_in(m):
        return jnp.einsum('gpc,gh->gchp', m, eye).reshape(HALF_CH, HALF_ST)

    def blockdiag_out(m):
        return jnp.einsum('gcp,gh->gphc', m, eye).reshape(HALF_ST, HALF_CH)

    bsub, csub, ars, ais = [], [], [], []
    for k in range(SSM_HALVES):
        gs = slice(k * HALF_GROUPS, (k + 1) * HALF_GROUPS)
        bsub.append(jnp.concatenate([blockdiag_in(bbr[gs]), blockdiag_in(bbi[gs])], axis=1))
        csub.append(jnp.concatenate([blockdiag_out(c_re[gs].astype(F32)),
                                     -blockdiag_out(c_im[gs].astype(F32))], axis=0))
        ars.append(ar[gs].reshape(1, HALF_ST))
        ais.append(ai[gs].reshape(1, HALF_ST))

    return dict(
        g_mix=g_mix.reshape(1, D_MODEL), w_in=w_in.astype(BF16), g_v=g_v,
        ar=jnp.concatenate(ars, axis=0), ai=jnp.concatenate(ais, axis=0),
        bsub=jnp.stack(bsub).astype(BF16), csub=jnp.stack(csub).astype(BF16),
        d_skip=d_skip.reshape(1, SSM_WIDTH), w_glu=w_glu.astype(BF16), b_glu=b_glu.reshape(1, SSM_WIDTH),
        g_out_a=g_out_a.reshape(1, A_WIDTH), g_out_b=g_out_b.reshape(1, SSM_WIDTH),
        w_out=w_out.astype(BF16), g_ffn=g_ffn.reshape(1, D_MODEL),
        w_ffn_in=w_ffn_in.astype(BF16), conv_w=conv_w, conv_b=conv_b.reshape(1, 2 * D_FF),
        w_ffn_out=w_ffn_out.astype(BF16), g_final=g_final.reshape(1, D_MODEL))


def _state_in(s_re, s_im):
    nb = s_re.shape[0]
    parts = []
    for k in range(SSM_HALVES):
        gs = slice(k * HALF_GROUPS, (k + 1) * HALF_GROUPS)
        parts += [s_re[:, gs].reshape(nb, HALF_ST), s_im[:, gs].reshape(nb, HALF_ST)]
    return jnp.concatenate(parts, axis=1)


def _state_out(st):
    nb = st.shape[0]
    st = st.reshape(nb, SSM_HALVES, 2, HALF_GROUPS, SSM_STATE)
    return (st[:, :, 0].reshape(nb, SSM_GROUPS, SSM_STATE),
            st[:, :, 1].reshape(nb, SSM_GROUPS, SSM_STATE))


def _conv_state_in(conv_buf):
    nb = conv_buf.shape[0]
    return jnp.transpose(conv_buf, (1, 0, 2)).reshape((CONV_W - 1) * nb, 2 * D_FF)


def _conv_state_out(rows_tm, nb):
    return jnp.transpose(rows_tm.reshape(CONV_W - 1, nb, 2 * D_FF), (1, 0, 2))


def kernel(x_prompt, x_sample, state_ssm_re, state_ssm_im, state_conv, g_mix, w_in, g_v, w_s, b_s,
           lam_re, lam_im, log_dt, b_re, b_im, c_re, c_im, d_skip, w_glu, b_glu, g_out_a, g_out_b,
           w_out, g_ffn, w_ffn_in, conv_w, conv_b, w_ffn_out, g_final):
    p = _prep_params(g_mix, w_in, g_v, lam_re, lam_im, log_dt, b_re, b_im, c_re, c_im, d_skip,
                     w_glu, b_glu, g_out_a, g_out_b, w_out, g_ffn, w_ffn_in, conv_w, conv_b,
                     w_ffn_out, g_final)
    tril = jnp.tril(w_s.astype(F32))

    nbp = x_prompt.shape[0]
    zero_state = jnp.zeros((nbp, SSM_GROUPS, SSM_STATE), state_ssm_re.dtype)
    zero_conv = jnp.zeros((nbp, CONV_W - 1, 2 * D_FF), state_conv.dtype)
    bs_p = jnp.broadcast_to(jnp.transpose(b_s)[:, :, None], (CHUNK, A_HEADS, A_HEAD_DIM)).reshape(CHUNK, A_WIDTH)
    h_p, st_p = _mixer_prompt_call(x_prompt, _state_in(zero_state, zero_state), p, tril.astype(BF16), bs_p)
    y_p, conv_p = _ffn_call(h_p, _conv_state_in(zero_conv), p, nbp, FFN_PROMPT_STEPS, True)
    re_p, im_p = _state_out(st_p)
    conv_p = _conv_state_out(conv_p, nbp)

    nbs, nls, _ = x_sample.shape
    x_tm = jnp.transpose(x_sample, (1, 0, 2)).reshape(nls * nbs, D_MODEL)
    ws_s = jnp.broadcast_to(tril[:, :nls, :nls].reshape(A_HEADS * nls * nls, 1), (A_HEADS * nls * nls, A_HEAD_DIM))
    bs_s = jnp.broadcast_to(b_s[:, :nls].reshape(A_HEADS * nls, 1), (A_HEADS * nls, A_HEAD_DIM))
    h_s, st_s, v_s = _mixer_sample_call(x_tm, _state_in(state_ssm_re, state_ssm_im), p, ws_s, bs_s, nbs, nls)
    y_s, conv_s = _ffn_call(h_s, _conv_state_in(state_conv), p, nbs, nls, False)
    y_s = jnp.transpose(y_s.reshape(nls, nbs, D_MODEL), (1, 0, 2))
    v_s = jnp.transpose(v_s.reshape(nls, nbs, A_HEADS, A_HEAD_DIM), (1, 0, 2, 3))
    re_s, im_s = _state_out(st_s)
    conv_s = _conv_state_out(conv_s, nbs)
    return (y_p, y_s, v_s, re_p, im_p, conv_p, re_s, im_s, conv_s)
```

```python
import functools
import math

import jax
import jax.numpy as jnp
from jax import lax
from jax.experimental import pallas as pl
from jax.experimental.pallas import tpu as pltpu

D_MODEL = 1024
A_HEADS = 4
A_HEAD_DIM = 128
A_WIDTH = 512
CHUNK = 128
SSM_WIDTH = 512
SSM_GROUP = 16
SSM_GROUPS = 32
SSM_STATE = 64
D_FF = 2816
CONV_W = 3
EPS = 1e-6

SSM_HALVES = 2
HALF_GROUPS = SSM_GROUPS // SSM_HALVES
HALF_CH = HALF_GROUPS * SSM_GROUP
HALF_ST = HALF_GROUPS * SSM_STATE
STATE_W = SSM_HALVES * 2 * HALF_ST

FF_CHUNK = 256
FF_NCHUNK = D_FF // FF_CHUNK
FFN_PROMPT_STEPS = 64
FFN_SUB_ROWS = 256
MIX_SUB_T = 32

SUBLANES = 8
SCAN_LANES = 512
VMEM_LIMIT_BYTES = 60 * 1024 * 1024

BF16 = jnp.bfloat16
F32 = jnp.float32


def _rms(x, g):
    ms = jnp.mean(x * x, axis=-1, keepdims=True)
    return x * lax.rsqrt(ms + EPS) * g


def _gelu(x):
    c = math.sqrt(2.0 / math.pi)
    return x * (0.5 * (1.0 + jnp.tanh(c * (x + 0.044715 * (x * x * x)))))


def _dot(a, b):
    return jnp.dot(a, b, preferred_element_type=F32)


def _to_time_major(x, n_batch, n_t):
    w = x.shape[-1]
    return jnp.swapaxes(x.reshape(n_batch, n_t, w), 0, 1).reshape(n_t * n_batch, w)


def _to_batch_major(x, n_batch, n_t):
    w = x.shape[-1]
    return jnp.swapaxes(x.reshape(n_t, n_batch, w), 0, 1)


def _scan_inplace(u_scr, hst_ref, ar_ref, ai_ref, k, n_batch, n_t):
    base = k * 2 * HALF_ST

    def row_chunk(rc, _):
        r0 = pl.multiple_of(rc * SUBLANES, SUBLANES)
        for c in range(HALF_ST // SCAN_LANES):
            lo = c * SCAN_LANES
            re_l = slice(lo, lo + SCAN_LANES)
            im_l = slice(HALF_ST + lo, HALF_ST + lo + SCAN_LANES)
            ar = jnp.broadcast_to(ar_ref[k:k + 1, re_l], (SUBLANES, SCAN_LANES))
            ai = jnp.broadcast_to(ai_ref[k:k + 1, re_l], (SUBLANES, SCAN_LANES))
            hr0 = hst_ref[pl.ds(r0, SUBLANES), base + lo:base + lo + SCAN_LANES]
            hi0 = hst_ref[pl.ds(r0, SUBLANES), base + HALF_ST + lo:base + HALF_ST + lo + SCAN_LANES]

            def step(t, carry):
                hr, hi = carry
                row = pl.multiple_of(t * n_batch + r0, SUBLANES)
                ur = u_scr[pl.ds(row, SUBLANES), re_l]
                ui = u_scr[pl.ds(row, SUBLANES), im_l]
                nhr = ar * hr - ai * hi + ur
                nhi = ar * hi + ai * hr + ui
                u_scr[pl.ds(row, SUBLANES), re_l] = nhr
                u_scr[pl.ds(row, SUBLANES), im_l] = nhi
                return nhr, nhi

            hr, hi = lax.fori_loop(0, n_t, step, (hr0, hi0), unroll=min(n_t, 4))
            hst_ref[pl.ds(r0, SUBLANES), base + lo:base + lo + SCAN_LANES] = hr
            hst_ref[pl.ds(r0, SUBLANES), base + HALF_ST + lo:base + HALF_ST + lo + SCAN_LANES] = hi
        return 0

    lax.fori_loop(0, n_batch // SUBLANES, row_chunk, 0)


def _ssm_glu(xs, n_batch, n_t, u_scr, hst_ref, ar_ref, ai_ref, bsub_ref, csub_ref, dskip_ref,
             wglu_ref, bglu_ref, gob_ref):
    xs_b = xs.astype(BF16)
    y_parts = []
    for k in range(SSM_HALVES):
        u_scr[...] = _dot(xs_b[:, k * HALF_CH:(k + 1) * HALF_CH], bsub_ref[k])
        _scan_inplace(u_scr, hst_ref, ar_ref, ai_ref, k, n_batch, n_t)
        y_parts.append(_dot(u_scr[...].astype(BF16), csub_ref[k]))
    ys = jnp.concatenate(y_parts, axis=-1) + dskip_ref[...] * xs
    g = _gelu(ys)
    z = _dot(g.astype(BF16), wglu_ref[...]) + bglu_ref[...]
    return _rms(g * jax.nn.sigmoid(z), gob_ref[...])


def _mixer_prompt_kernel(n_batch, n_steps,
                         x_ref, h0_ref, gmix_ref, win_ref, gv_ref, ws_ref, bs_ref,
                         ar_ref, ai_ref, bsub_ref, csub_ref, dskip_ref,
                         wglu_ref, bglu_ref, goa_ref, gob_ref, wout_ref,
                         hout_ref, st_ref, hst_ref):
    sub_t = MIX_SUB_T
    n_sub = CHUNK // sub_t
    sub_rows = n_batch * sub_t
    i = pl.program_id(0)

    @pl.when(i == 0)
    def _():
        hst_ref[...] = h0_ref[...]

    state, trans = [], []
    for k in range(SSM_HALVES):
        base = k * 2 * HALF_ST
        state.append((hst_ref[:, base:base + HALF_ST], hst_ref[:, base + HALF_ST:base + 2 * HALF_ST]))
        trans.append((jnp.broadcast_to(ar_ref[k:k + 1, :], (n_batch, HALF_ST)),
                      jnp.broadcast_to(ai_ref[k:k + 1, :], (n_batch, HALF_ST))))

    xs_sub, proj_sub, bn_sub = [], [], []
    for s in range(n_sub):
        x = x_ref[:, s * sub_t:(s + 1) * sub_t, :].reshape(sub_rows, D_MODEL)
        n1 = _rms(x, gmix_ref[...]).astype(BF16)
        proj = _dot(n1, win_ref[...])
        xs_sub.append(x)
        proj_sub.append(proj)

        xs = _to_time_major(proj[:, 2 * A_WIDTH:], n_batch, sub_t)
        xs_b = xs.astype(BF16)
        y_parts = []
        for k in range(SSM_HALVES):
            u = _dot(xs_b[:, k * HALF_CH:(k + 1) * HALF_CH], bsub_ref[k])
            ar, ai = trans[k]
            hr, hi = state[k]
            slabs = []
            for t in range(sub_t):
                ur = u[t * n_batch:(t + 1) * n_batch, :HALF_ST]
                ui = u[t * n_batch:(t + 1) * n_batch, HALF_ST:]
                hr, hi = ar * hr - ai * hi + ur, ar * hi + ai * hr + ui
                slabs.append(jnp.concatenate([hr, hi], axis=-1))
            state[k] = (hr, hi)
            y_parts.append(_dot(jnp.concatenate(slabs, axis=0).astype(BF16), csub_ref[k]))
        ys = jnp.concatenate(y_parts, axis=-1) + dskip_ref[...] * xs
        g = _gelu(ys)
        z = _dot(g.astype(BF16), wglu_ref[...]) + bglu_ref[...]
        bn_tm = _rms(g * jax.nn.sigmoid(z), gob_ref[...])
        bn_sub.append(_to_batch_major(bn_tm, n_batch, sub_t).reshape(sub_rows, SSM_WIDTH))

    for k in range(SSM_HALVES):
        base = k * 2 * HALF_ST
        hst_ref[:, base:base + HALF_ST] = state[k][0]
        hst_ref[:, base + HALF_ST:base + 2 * HALF_ST] = state[k][1]

    a_heads = [[] for _ in range(n_sub)]
    for h in range(A_HEADS):
        hs = slice(h * A_HEAD_DIM, (h + 1) * A_HEAD_DIM)
        vn = [_rms(proj_sub[s][:, A_WIDTH + h * A_HEAD_DIM:A_WIDTH + (h + 1) * A_HEAD_DIM],
                   gv_ref[h:h + 1, :]).astype(BF16) for s in range(n_sub)]
        v_wide = jnp.concatenate(
            [jnp.concatenate([vn[s][b * sub_t:(b + 1) * sub_t, :] for s in range(n_sub)], axis=0)
             for b in range(n_batch)], axis=-1)
        s_wide = _dot(ws_ref[h], v_wide)
        for s in range(n_sub):
            ts = slice(s * sub_t, (s + 1) * sub_t)
            bias = bs_ref[ts, hs]
            gate = jnp.concatenate([s_wide[ts, b * A_HEAD_DIM:(b + 1) * A_HEAD_DIM] + bias
                                    for b in range(n_batch)], axis=0)
            a_heads[s].append(proj_sub[s][:, hs] * gate)

    for s in range(n_sub):
        an = _rms(jnp.concatenate(a_heads[s], axis=-1), goa_ref[...])
        mix = jnp.concatenate([an, bn_sub[s]], axis=-1).astype(BF16)
        hout_ref[:, s * sub_t:(s + 1) * sub_t, :] = (
            xs_sub[s] + _dot(mix, wout_ref[...])).reshape(n_batch, sub_t, D_MODEL)

    @pl.when(i == n_steps - 1)
    def _():
        st_ref[...] = hst_ref[...]


def _mixer_sample_kernel(n_batch, n_t,
                         x_ref, h0_ref, gmix_ref, win_ref, gv_ref, ws_ref, bs_ref,
                         ar_ref, ai_ref, bsub_ref, csub_ref, dskip_ref,
                         wglu_ref, bglu_ref, goa_ref, gob_ref, wout_ref,
                         hout_ref, st_ref, v_ref, u_scr, hst_ref):
    hst_ref[...] = h0_ref[...]
    x = x_ref[...]
    n1 = _rms(x, gmix_ref[...]).astype(BF16)
    proj = _dot(n1, win_ref[...])

    a_heads = []
    for h in range(A_HEADS):
        hs = slice(h * A_HEAD_DIM, (h + 1) * A_HEAD_DIM)
        vn = _rms(proj[:, A_WIDTH + h * A_HEAD_DIM:A_WIDTH + (h + 1) * A_HEAD_DIM], gv_ref[h:h + 1, :])
        v_ref[:, hs] = vn
        slabs = []
        for t in range(n_t):
            s = jnp.broadcast_to(bs_ref[h * n_t + t:h * n_t + t + 1, :], (n_batch, A_HEAD_DIM))
            for src in range(t + 1):
                r = (h * n_t + t) * n_t + src
                s = s + ws_ref[r:r + 1, :] * vn[src * n_batch:(src + 1) * n_batch, :]
            slabs.append(proj[t * n_batch:(t + 1) * n_batch, hs] * s)
        a_heads.append(jnp.concatenate(slabs, axis=0))
    an = _rms(jnp.concatenate(a_heads, axis=-1), goa_ref[...])

    bn = _ssm_glu(proj[:, 2 * A_WIDTH:], n_batch, n_t, u_scr, hst_ref, ar_ref, ai_ref, bsub_ref,
                  csub_ref, dskip_ref, wglu_ref, bglu_ref, gob_ref)
    mix = jnp.concatenate([an, bn], axis=-1).astype(BF16)
    hout_ref[...] = x + _dot(mix, wout_ref[...])
    st_ref[...] = hst_ref[...]


def _ffn_kernel(n_batch, n_t, n_steps, batch_major,
                h_ref, conv0_ref, gffn_ref, wfi_ref, cw_ref, cb_ref, wfo_ref, gfin_ref,
                y_ref, convout_ref, carry_scr):
    rows = n_t * n_batch
    halo = (CONV_W - 1) * n_batch
    i = pl.program_id(0)

    @pl.when(i == 0)
    def _():
        carry_scr[...] = conv0_ref[...]

    n_sub = rows // FFN_SUB_ROWS
    sub_t = FFN_SUB_ROWS // n_batch
    hs, n2s = [], []
    for s in range(n_sub):
        if batch_major:
            h_s = _to_time_major(h_ref[:, s * sub_t:(s + 1) * sub_t, :].reshape(FFN_SUB_ROWS, D_MODEL),
                                 n_batch, sub_t)
        else:
            h_s = h_ref[s * FFN_SUB_ROWS:(s + 1) * FFN_SUB_ROWS, :]
        hs.append(h_s)
        n2s.append(_rms(h_s, gffn_ref[...]).astype(BF16))

    def col_slices(j):
        return [slice(c0, c0 + FF_CHUNK) for c0 in (j * FF_CHUNK, D_FF + j * FF_CHUNK)]

    def up_proj(j, s):
        return [jnp.dot(n2s[s], wfi_ref[:, cs], preferred_element_type=F32) for cs in col_slices(j)]

    order = [(j, s) for j in range(FF_NCHUNK) for s in range(n_sub)]
    accs = [jnp.zeros((FFN_SUB_ROWS, D_MODEL), F32) for _ in range(n_sub)]
    tails = [None, None]
    ups_next = up_proj(*order[0])
    for idx, (j, s) in enumerate(order):
        ups = ups_next
        if idx + 1 < len(order):
            ups_next = up_proj(*order[idx + 1])
        conv = []
        for ci, (cs, up) in enumerate(zip(col_slices(j), ups)):
            tail = carry_scr[:, cs] if s == 0 else tails[ci]
            padded = jnp.concatenate([tail, up], axis=0)
            tails[ci] = padded[FFN_SUB_ROWS:FFN_SUB_ROWS + halo, :]
            if s == n_sub - 1:
                carry_scr[:, cs] = tails[ci]
            out = cb_ref[:, cs]
            for tap in range(CONV_W):
                out = out + cw_ref[tap:tap + 1, cs] * padded[tap * n_batch:tap * n_batch + FFN_SUB_ROWS, :]
            conv.append(out)
        act = _gelu(conv[0]) * conv[1]
        accs[s] = accs[s] + jnp.dot(act.astype(BF16), wfo_ref[j * FF_CHUNK:(j + 1) * FF_CHUNK, :],
                                    preferred_element_type=F32)
    for s in range(n_sub):
        y_s = _rms(hs[s] + accs[s], gfin_ref[...])
        if batch_major:
            y_ref[:, s * sub_t:(s + 1) * sub_t, :] = _to_batch_major(y_s, n_batch, sub_t)
        else:
            y_ref[s * FFN_SUB_ROWS:(s + 1) * FFN_SUB_ROWS, :] = y_s

    @pl.when(i == n_steps - 1)
    def _():
        convout_ref[...] = carry_scr[...]


def _const_spec(shape):
    zeros = (0,) * len(shape)
    return pl.BlockSpec(shape, lambda i: zeros, pipeline_mode=pl.Buffered(1))


def _params():
    return pltpu.CompilerParams(dimension_semantics=("arbitrary",), vmem_limit_bytes=VMEM_LIMIT_BYTES)


def _mixer_consts(h0, p, ws, bs):
    return [h0, p['g_mix'], p['w_in'], p['g_v'], ws, bs, p['ar'], p['ai'],
            p['bsub'], p['csub'], p['d_skip'], p['w_glu'], p['b_glu'],
            p['g_out_a'], p['g_out_b'], p['w_out']]


def _mixer_prompt_call(x, h0, p, ws, bs):
    n_batch, n_l, _ = x.shape
    n_steps = n_l // CHUNK
    blk = pl.BlockSpec((n_batch, CHUNK, D_MODEL), lambda i: (0, i, 0))
    consts = _mixer_consts(h0, p, ws, bs)
    return pl.pallas_call(
        functools.partial(_mixer_prompt_kernel, n_batch, n_steps),
        grid=(n_steps,),
        in_specs=[blk] + [_const_spec(c.shape) for c in consts],
        out_specs=[blk, pl.BlockSpec((n_batch, STATE_W), lambda i: (0, 0))],
        out_shape=[jax.ShapeDtypeStruct(x.shape, F32),
                   jax.ShapeDtypeStruct((n_batch, STATE_W), F32)],
        scratch_shapes=[pltpu.VMEM((n_batch, STATE_W), F32)],
        compiler_params=_params(),
        name="mixer_prompt",
    )(x, *consts)


def _mixer_sample_call(x_tm, h0, p, ws, bs, n_batch, n_t):
    rows = n_batch * n_t
    consts = _mixer_consts(h0, p, ws, bs)
    return pl.pallas_call(
        functools.partial(_mixer_sample_kernel, n_batch, n_t),
        grid=(1,),
        in_specs=[_const_spec(x_tm.shape)] + [_const_spec(c.shape) for c in consts],
        out_specs=[pl.BlockSpec((rows, D_MODEL), lambda i: (0, 0)),
                   pl.BlockSpec((n_batch, STATE_W), lambda i: (0, 0)),
                   pl.BlockSpec((rows, A_WIDTH), lambda i: (0, 0))],
        out_shape=[jax.ShapeDtypeStruct((rows, D_MODEL), F32),
                   jax.ShapeDtypeStruct((n_batch, STATE_W), F32),
                   jax.ShapeDtypeStruct((rows, A_WIDTH), F32)],
        scratch_shapes=[pltpu.VMEM((rows, 2 * HALF_ST), F32),
                        pltpu.VMEM((n_batch, STATE_W), F32)],
        compiler_params=_params(),
        name="mixer_sample",
    )(x_tm, *consts)


def _ffn_call(h, conv0, p, n_batch, n_t, batch_major):
    if batch_major:
        n_steps = h.shape[1] // n_t
        blk = pl.BlockSpec((n_batch, n_t, D_MODEL), lambda i: (0, i, 0))
    else:
        n_steps = h.shape[0] // (n_t * n_batch)
        blk = pl.BlockSpec((n_t * n_batch, D_MODEL), lambda i: (i, 0))
    halo = (CONV_W - 1) * n_batch
    consts = [conv0, p['g_ffn'], p['w_ffn_in'], p['conv_w'], p['conv_b'], p['w_ffn_out'], p['g_final']]
    return pl.pallas_call(
        functools.partial(_ffn_kernel, n_batch, n_t, n_steps, batch_major),
        grid=(n_steps,),
        in_specs=[blk] + [_const_spec(c.shape) for c in consts],
        out_specs=[blk, pl.BlockSpec((halo, 2 * D_FF), lambda i: (0, 0))],
        out_shape=[jax.ShapeDtypeStruct(h.shape, F32),
                   jax.ShapeDtypeStruct((halo, 2 * D_FF), F32)],
        scratch_shapes=[pltpu.VMEM((halo, 2 * D_FF), F32)],
        compiler_params=_params(),
        name="ffn_prompt" if batch_major else "ffn_sample",
    )(h, *consts)


def _prep_params(g_mix, w_in, g_v, lam_re, lam_im, log_dt, b_re, b_im, c_re, c_im,
                 d_skip, w_glu, b_glu, g_out_a, g_out_b, w_out, g_ffn, w_ffn_in, conv_w, conv_b,
                 w_ffn_out, g_final):
    lr = lam_re.astype(F32)
    li = lam_im.astype(F32)
    dt = jnp.exp(log_dt.astype(F32))[:, None]
    mag = jnp.exp(lr * dt)
    ar = mag * jnp.cos(li * dt)
    ai = mag * jnp.sin(li * dt)
    den = lr * lr + li * li
    fr = ((ar - 1.0) * lr + ai * li) / den
    fi = (ai * lr - (ar - 1.0) * li) / den
    bre = b_re.astype(F32)
    bim = b_im.astype(F32)
    bbr = fr[..., None] * bre - fi[..., None] * bim
    bbi = fr[..., None] * bim + fi[..., None] * bre
    eye = jnp.eye(HALF_GROUPS, dtype=F32)

    def blockdiag_in(m):
        return jnp.einsum('gpc,gh->gchp', m, eye).reshape(HALF_CH, HALF_ST)

    def blockdiag_out(m):
        return jnp.einsum('gcp,gh->gphc', m, eye).reshape(HALF_ST, HALF_CH)

    bsub, csub, ars, ais = [], [], [], []
    for k in range(SSM_HALVES):
        gs = slice(k * HALF_GROUPS, (k + 1) * HALF_GROUPS)
        bsub.append(jnp.concatenate([blockdiag_in(bbr[gs]), blockdiag_in(bbi[gs])], axis=1))
        csub.append(jnp.concatenate([blockdiag_out(c_re[gs].astype(F32)),
                                     -blockdiag_out(c_im[gs].astype(F32))], axis=0))
        ars.append(ar[gs].reshape(1, HALF_ST))
        ais.append(ai[gs].reshape(1, HALF_ST))

    return dict(
        g_mix=g_mix.reshape(1, D_MODEL), w_in=w_in.astype(BF16), g_v=g_v,
        ar=jnp.concatenate(ars, axis=0), ai=jnp.concatenate(ais, axis=0),
        bsub=jnp.stack(bsub).astype(BF16), csub=jnp.stack(csub).astype(BF16),
        d_skip=d_skip.reshape(1, SSM_WIDTH), w_glu=w_glu.astype(BF16), b_glu=b_glu.reshape(1, SSM_WIDTH),
        g_out_a=g_out_a.reshape(1, A_WIDTH), g_out_b=g_out_b.reshape(1, SSM_WIDTH),
        w_out=w_out.astype(BF16), g_ffn=g_ffn.reshape(1, D_MODEL),
        w_ffn_in=w_ffn_in.astype(BF16), conv_w=conv_w, conv_b=conv_b.reshape(1, 2 * D_FF),
        w_ffn_out=w_ffn_out.astype(BF16), g_final=g_final.reshape(1, D_MODEL))


def _state_in(s_re, s_im):
    nb = s_re.shape[0]
    parts = []
    for k in range(SSM_HALVES):
        gs = slice(k * HALF_GROUPS, (k + 1) * HALF_GROUPS)
        parts += [s_re[:, gs].reshape(nb, HALF_ST), s_im[:, gs].reshape(nb, HALF_ST)]
    return jnp.concatenate(parts, axis=1)


def _state_out(st):
    nb = st.shape[0]
    st = st.reshape(nb, SSM_HALVES, 2, HALF_GROUPS, SSM_STATE)
    return (st[:, :, 0].reshape(nb, SSM_GROUPS, SSM_STATE),
            st[:, :, 1].reshape(nb, SSM_GROUPS, SSM_STATE))


def _conv_state_in(conv_buf):
    nb = conv_buf.shape[0]
    return jnp.transpose(conv_buf, (1, 0, 2)).reshape((CONV_W - 1) * nb, 2 * D_FF)


def _conv_state_out(rows_tm, nb):
    return jnp.transpose(rows_tm.reshape(CONV_W - 1, nb, 2 * D_FF), (1, 0, 2))


def kernel(x_prompt, x_sample, state_ssm_re, state_ssm_im, state_conv, g_mix, w_in, g_v, w_s, b_s,
           lam_re, lam_im, log_dt, b_re, b_im, c_re, c_im, d_skip, w_glu, b_glu, g_out_a, g_out_b,
           w_out, g_ffn, w_ffn_in, conv_w, conv_b, w_ffn_out, g_final):
    p = _prep_params(g_mix, w_in, g_v, lam_re, lam_im, log_dt, b_re, b_im, c_re, c_im, d_skip,
                     w_glu, b_glu, g_out_a, g_out_b, w_out, g_ffn, w_ffn_in, conv_w, conv_b,
                     w_ffn_out, g_final)
    tril = jnp.tril(w_s.astype(F32))

    nbp = x_prompt.shape[0]
    zero_state = jnp.zeros((nbp, SSM_GROUPS, SSM_STATE), state_ssm_re.dtype)
    zero_conv = jnp.zeros((nbp, CONV_W - 1, 2 * D_FF), state_conv.dtype)
    bs_p = jnp.broadcast_to(jnp.transpose(b_s)[:, :, None], (CHUNK, A_HEADS, A_HEAD_DIM)).reshape(CHUNK, A_WIDTH)
    h_p, st_p = _mixer_prompt_call(x_prompt, _state_in(zero_state, zero_state), p, tril.astype(BF16), bs_p)
    y_p, conv_p = _ffn_call(h_p, _conv_state_in(zero_conv), p, nbp, FFN_PROMPT_STEPS, True)
    re_p, im_p = _state_out(st_p)
    conv_p = _conv_state_out(conv_p, nbp)

    nbs, nls, _ = x_sample.shape
    x_tm = jnp.transpose(x_sample, (1, 0, 2)).reshape(nls * nbs, D_MODEL)
    ws_s = jnp.broadcast_to(tril[:, :nls, :nls].reshape(A_HEADS * nls * nls, 1), (A_HEADS * nls * nls, A_HEAD_DIM))
    bs_s = jnp.broadcast_to(b_s[:, :nls].reshape(A_HEADS * nls, 1), (A_HEADS * nls, A_HEAD_DIM))
    h_s, st_s, v_s = _mixer_sample_call(x_tm, _state_in(state_ssm_re, state_ssm_im), p, ws_s, bs_s, nbs, nls)
    y_s, conv_s = _ffn_call(h_s, _conv_state_in(state_conv), p, nbs, nls, False)
    y_s = jnp.transpose(y_s.reshape(nls, nbs, D_MODEL), (1, 0, 2))
    v_s = jnp.transpose(v_s.reshape(nls, nbs, A_HEADS, A_HEAD_DIM), (1, 0, 2, 3))
    re_s, im_s = _state_out(st_s)
    conv_s = _conv_state_out(conv_s, nbs)
    return (y_p, y_s, v_s, re_p, im_p, conv_p, re_s, im_s, conv_s)
```

```python
import functools
import math

import jax
import jax.numpy as jnp
from jax import lax
from jax.experimental import pallas as pl
from jax.experimental.pallas import tpu as pltpu

D_MODEL = 1024
A_HEADS = 4
A_HEAD_DIM = 128
A_WIDTH = 512
CHUNK = 128
SSM_WIDTH = 512
SSM_GROUP = 16
SSM_GROUPS = 32
SSM_STATE = 64
D_FF = 2816
CONV_W = 3
EPS = 1e-6

SSM_HALVES = 2
HALF_GROUPS = SSM_GROUPS // SSM_HALVES
HALF_CH = HALF_GROUPS * SSM_GROUP
HALF_ST = HALF_GROUPS * SSM_STATE
STATE_W = SSM_HALVES * 2 * HALF_ST

FF_CHUNK = 256
FF_NCHUNK = D_FF // FF_CHUNK
SUB_ROWS = 256
FFN_PROMPT_STEPS = 64

SUBLANES = 8
SCAN_LANES = 512
VMEM_LIMIT_BYTES = 60 * 1024 * 1024

BF16 = jnp.bfloat16
F32 = jnp.float32


def _rms(x, g):
    ms = jnp.mean(x * x, axis=-1, keepdims=True)
    return x * lax.rsqrt(ms + EPS) * g


def _gelu(x):
    c = math.sqrt(2.0 / math.pi)
    return x * (0.5 * (1.0 + jnp.tanh(c * (x + 0.044715 * (x * x * x)))))


def _dot(a, b):
    return jnp.dot(a, b, preferred_element_type=F32)


def _to_time_major(x, n_batch, n_t):
    w = x.shape[-1]
    return jnp.swapaxes(x.reshape(n_batch, n_t, w), 0, 1).reshape(n_t * n_batch, w)


def _to_batch_major(x, n_batch, n_t):
    w = x.shape[-1]
    return jnp.swapaxes(x.reshape(n_t, n_batch, w), 0, 1)


def _glu_norm(ys, xs, dskip_ref, wglu_ref, bglu_ref, gob_ref):
    g = _gelu(ys + dskip_ref[...] * xs)
    z = _dot(g.astype(BF16), wglu_ref[...]) + bglu_ref[...]
    return _rms(g * jax.nn.sigmoid(z), gob_ref[...])


def _ffn_subblocks(get_h, put_y, n_batch, n_sub, carry_scr,
                   gffn_ref, wfi_ref, cw_ref, cb_ref, wfo_ref, gfin_ref):
    halo = (CONV_W - 1) * n_batch
    hs, n2s = {}, {}

    def norm_in(s):
        hs[s] = get_h(s)
        n2s[s] = _rms(hs[s], gffn_ref[...]).astype(BF16)

    def col_slices(j):
        return [slice(c0, c0 + FF_CHUNK) for c0 in (j * FF_CHUNK, D_FF + j * FF_CHUNK)]

    def up_proj(j, s):
        return [jnp.dot(n2s[s], wfi_ref[:, cs], preferred_element_type=F32) for cs in col_slices(j)]

    order = [(j, s) for j in range(FF_NCHUNK) for s in range(n_sub)]
    accs = [jnp.zeros((SUB_ROWS, D_MODEL), F32) for _ in range(n_sub)]
    tails = [None, None]
    norm_in(0)
    ups_next = up_proj(*order[0])
    for s in range(1, n_sub):
        norm_in(s)
    for idx, (j, s) in enumerate(order):
        ups = ups_next
        if idx + 1 < len(order):
            ups_next = up_proj(*order[idx + 1])
        conv = []
        for ci, (cs, up) in enumerate(zip(col_slices(j), ups)):
            tail = carry_scr[:, cs] if s == 0 else tails[ci]
            padded = jnp.concatenate([tail, up], axis=0)
            tails[ci] = padded[SUB_ROWS:SUB_ROWS + halo, :]
            if s == n_sub - 1:
                carry_scr[:, cs] = tails[ci]
            out = cb_ref[:, cs]
            for tap in range(CONV_W):
                out = out + cw_ref[tap:tap + 1, cs] * padded[tap * n_batch:tap * n_batch + SUB_ROWS, :]
            conv.append(out)
        act = _gelu(conv[0]) * conv[1]
        accs[s] = accs[s] + jnp.dot(act.astype(BF16), wfo_ref[j * FF_CHUNK:(j + 1) * FF_CHUNK, :],
                                    preferred_element_type=F32)
        if j == FF_NCHUNK - 1:
            put_y(s, _rms(hs[s] + accs[s], gfin_ref[...]))


def _mixer_prompt_kernel(n_batch, n_steps,
                         x_ref, h0_ref, gmix_ref, win_ref, gv_ref, ws_ref, bs_ref,
                         ar_ref, ai_ref, bsub_ref, csub_ref, dskip_ref,
                         wglu_ref, bglu_ref, goa_ref, gob_ref, wout_ref,
                         hout_ref, st_ref, hst_ref):
    sub_t = SUB_ROWS // n_batch
    n_sub = CHUNK // sub_t
    i = pl.program_id(0)

    @pl.when(i == 0)
    def _():
        hst_ref[...] = h0_ref[...]

    state, trans = [], []
    for k in range(SSM_HALVES):
        base = k * 2 * HALF_ST
        state.append((hst_ref[:, base:base + HALF_ST], hst_ref[:, base + HALF_ST:base + 2 * HALF_ST]))
        trans.append((jnp.broadcast_to(ar_ref[k:k + 1, :], (n_batch, HALF_ST)),
                      jnp.broadcast_to(ai_ref[k:k + 1, :], (n_batch, HALF_ST))))

    x_sub, proj_sub, xs_sub, y_sub, bn_sub = {}, {}, {}, {}, {}
    a_heads = [[] for _ in range(n_sub)]

    def project(s):
        x = x_ref[:, s * sub_t:(s + 1) * sub_t, :].reshape(SUB_ROWS, D_MODEL)
        n1 = _rms(x, gmix_ref[...]).astype(BF16)
        x_sub[s] = x
        proj_sub[s] = _dot(n1, win_ref[...])
        xs_sub[s] = _to_time_major(proj_sub[s][:, 2 * A_WIDTH:], n_batch, sub_t)
        y_sub[s] = []

    def ssm_half(s, k):
        u = _dot(xs_sub[s].astype(BF16)[:, k * HALF_CH:(k + 1) * HALF_CH], bsub_ref[k])
        ar, ai = trans[k]
        hr, hi = state[k]
        slabs = []
        for t in range(sub_t):
            ur = u[t * n_batch:(t + 1) * n_batch, :HALF_ST]
            ui = u[t * n_batch:(t + 1) * n_batch, HALF_ST:]
            hr, hi = ar * hr - ai * hi + ur, ar * hi + ai * hr + ui
            slabs.append(jnp.concatenate([hr, hi], axis=-1))
        state[k] = (hr, hi)
        y_sub[s].append(_dot(jnp.concatenate(slabs, axis=0).astype(BF16), csub_ref[k]))

    def glu(s):
        bn_tm = _glu_norm(jnp.concatenate(y_sub[s], axis=-1), xs_sub[s], dskip_ref, wglu_ref, bglu_ref,
                          gob_ref)
        bn_sub[s] = _to_batch_major(bn_tm, n_batch, sub_t).reshape(SUB_ROWS, SSM_WIDTH)

    def gate_head(h):
        hs = slice(h * A_HEAD_DIM, (h + 1) * A_HEAD_DIM)
        vn = [_rms(proj_sub[s][:, A_WIDTH + h * A_HEAD_DIM:A_WIDTH + (h + 1) * A_HEAD_DIM],
                   gv_ref[h:h + 1, :]).astype(BF16) for s in range(n_sub)]
        v_wide = jnp.concatenate(
            [jnp.concatenate([vn[s][b * sub_t:(b + 1) * sub_t, :] for s in range(n_sub)], axis=0)
             for b in range(n_batch)], axis=-1)
        s_wide = _dot(ws_ref[h], v_wide)
        for s in range(n_sub):
            ts = slice(s * sub_t, (s + 1) * sub_t)
            bias = bs_ref[ts, hs]
            gate = jnp.concatenate([s_wide[ts, b * A_HEAD_DIM:(b + 1) * A_HEAD_DIM] + bias
                                    for b in range(n_batch)], axis=0)
            a_heads[s].append(proj_sub[s][:, hs] * gate)

    def out_proj(s):
        an = _rms(jnp.concatenate(a_heads[s], axis=-1), goa_ref[...])
        mix = jnp.concatenate([an, bn_sub[s]], axis=-1).astype(BF16)
        hout_ref[:, s * sub_t:(s + 1) * sub_t, :] = (
            x_sub[s] + _dot(mix, wout_ref[...])).reshape(n_batch, sub_t, D_MODEL)

    chain = [project] + [functools.partial(ssm_half, k=k) for k in range(SSM_HALVES)] + [glu]
    heads = list(range(A_HEADS))
    outs = list(range(n_sub))
    for wave in range(n_sub + len(chain) - 1):
        if wave >= n_sub:
            for h in heads[:2]:
                gate_head(h)
            heads = heads[2:]
        for s in range(n_sub):
            if 0 <= wave - s < len(chain):
                chain[wave - s](s)
        if not heads and outs and wave - outs[0] >= len(chain) - 1:
            out_proj(outs.pop(0))
    for h in heads:
        gate_head(h)
    for s in outs:
        out_proj(s)

    for k in range(SSM_HALVES):
        base = k * 2 * HALF_ST
        hst_ref[:, base:base + HALF_ST] = state[k][0]
        hst_ref[:, base + HALF_ST:base + 2 * HALF_ST] = state[k][1]

    @pl.when(i == n_steps - 1)
    def _():
        st_ref[...] = hst_ref[...]


def _ffn_prompt_kernel(n_batch, n_t, n_steps,
                       h_ref, conv0_ref, gffn_ref, wfi_ref, cw_ref, cb_ref, wfo_ref, gfin_ref,
                       y_ref, convout_ref, carry_scr):
    sub_t = SUB_ROWS // n_batch
    i = pl.program_id(0)

    @pl.when(i == 0)
    def _():
        carry_scr[...] = conv0_ref[...]

    def get_h(s):
        return _to_time_major(h_ref[:, s * sub_t:(s + 1) * sub_t, :].reshape(SUB_ROWS, D_MODEL),
                              n_batch, sub_t)

    def put_y(s, y):
        y_ref[:, s * sub_t:(s + 1) * sub_t, :] = _to_batch_major(y, n_batch, sub_t)

    _ffn_subblocks(get_h, put_y, n_batch, n_t // sub_t, carry_scr,
                   gffn_ref, wfi_ref, cw_ref, cb_ref, wfo_ref, gfin_ref)

    @pl.when(i == n_steps - 1)
    def _():
        convout_ref[...] = carry_scr[...]


def _scan_inplace(u_scr, hst_ref, ar_ref, ai_ref, k, n_batch, n_t):
    base = k * 2 * HALF_ST

    def row_chunk(rc, _):
        r0 = pl.multiple_of(rc * SUBLANES, SUBLANES)
        for c in range(HALF_ST // SCAN_LANES):
            lo = c * SCAN_LANES
            re_l = slice(lo, lo + SCAN_LANES)
            im_l = slice(HALF_ST + lo, HALF_ST + lo + SCAN_LANES)
            ar = jnp.broadcast_to(ar_ref[k:k + 1, re_l], (SUBLANES, SCAN_LANES))
            ai = jnp.broadcast_to(ai_ref[k:k + 1, re_l], (SUBLANES, SCAN_LANES))
            hr0 = hst_ref[pl.ds(r0, SUBLANES), base + lo:base + lo + SCAN_LANES]
            hi0 = hst_ref[pl.ds(r0, SUBLANES), base + HALF_ST + lo:base + HALF_ST + lo + SCAN_LANES]

            def step(t, carry):
                hr, hi = carry
                row = pl.multiple_of(t * n_batch + r0, SUBLANES)
                ur = u_scr[pl.ds(row, SUBLANES), re_l]
                ui = u_scr[pl.ds(row, SUBLANES), im_l]
                nhr = ar * hr - ai * hi + ur
                nhi = ar * hi + ai * hr + ui
                u_scr[pl.ds(row, SUBLANES), re_l] = nhr
                u_scr[pl.ds(row, SUBLANES), im_l] = nhi
                return nhr, nhi

            hr, hi = lax.fori_loop(0, n_t, step, (hr0, hi0), unroll=min(n_t, 4))
            hst_ref[pl.ds(r0, SUBLANES), base + lo:base + lo + SCAN_LANES] = hr
            hst_ref[pl.ds(r0, SUBLANES), base + HALF_ST + lo:base + HALF_ST + lo + SCAN_LANES] = hi
        return 0

    lax.fori_loop(0, n_batch // SUBLANES, row_chunk, 0)


def _mixer_sample_kernel(n_batch, n_t,
                         x_ref, h0_ref, gmix_ref, win_ref, gv_ref, ws_ref, bs_ref,
                         ar_ref, ai_ref, bsub_ref, csub_ref, dskip_ref,
                         wglu_ref, bglu_ref, goa_ref, gob_ref, wout_ref,
                         hout_ref, st_ref, v_ref, u_scr, hst_ref):
    hst_ref[...] = h0_ref[...]
    x = x_ref[...]
    n1 = _rms(x, gmix_ref[...]).astype(BF16)
    proj = _dot(n1, win_ref[...])

    a_heads = []
    for h in range(A_HEADS):
        hs = slice(h * A_HEAD_DIM, (h + 1) * A_HEAD_DIM)
        vn = _rms(proj[:, A_WIDTH + h * A_HEAD_DIM:A_WIDTH + (h + 1) * A_HEAD_DIM], gv_ref[h:h + 1, :])
        v_ref[:, hs] = vn
        slabs = []
        for t in range(n_t):
            s = jnp.broadcast_to(bs_ref[h * n_t + t:h * n_t + t + 1, :], (n_batch, A_HEAD_DIM))
            for src in range(t + 1):
                r = (h * n_t + t) * n_t + src
                s = s + ws_ref[r:r + 1, :] * vn[src * n_batch:(src + 1) * n_batch, :]
            slabs.append(proj[t * n_batch:(t + 1) * n_batch, hs] * s)
        a_heads.append(jnp.concatenate(slabs, axis=0))
    an = _rms(jnp.concatenate(a_heads, axis=-1), goa_ref[...])

    xs = proj[:, 2 * A_WIDTH:]
    xs_b = xs.astype(BF16)
    y_parts = []
    for k in range(SSM_HALVES):
        u_scr[...] = _dot(xs_b[:, k * HALF_CH:(k + 1) * HALF_CH], bsub_ref[k])
        _scan_inplace(u_scr, hst_ref, ar_ref, ai_ref, k, n_batch, n_t)
        y_parts.append(_dot(u_scr[...].astype(BF16), csub_ref[k]))
    bn = _glu_norm(jnp.concatenate(y_parts, axis=-1), xs, dskip_ref, wglu_ref, bglu_ref, gob_ref)

    mix = jnp.concatenate([an, bn], axis=-1).astype(BF16)
    hout_ref[...] = x + _dot(mix, wout_ref[...])
    st_ref[...] = hst_ref[...]


def _ffn_sample_kernel(n_batch, n_t,
                       h_ref, conv0_ref, gffn_ref, wfi_ref, cw_ref, cb_ref, wfo_ref, gfin_ref,
                       y_ref, convout_ref, carry_scr):
    carry_scr[...] = conv0_ref[...]

    def get_h(s):
        return h_ref[s * SUB_ROWS:(s + 1) * SUB_ROWS, :]

    def put_y(s, y):
        y_ref[s * SUB_ROWS:(s + 1) * SUB_ROWS, :] = y

    _ffn_subblocks(get_h, put_y, n_batch, n_t * n_batch // SUB_ROWS, carry_scr,
                   gffn_ref, wfi_ref, cw_ref, cb_ref, wfo_ref, gfin_ref)
    convout_ref[...] = carry_scr[...]


def _const_spec(shape):
    zeros = (0,) * len(shape)
    return pl.BlockSpec(shape, lambda i: zeros, pipeline_mode=pl.Buffered(1))


def _params():
    return pltpu.CompilerParams(dimension_semantics=("arbitrary",), vmem_limit_bytes=VMEM_LIMIT_BYTES)


def _mixer_consts(h0, p, ws, bs):
    return [h0, p['g_mix'], p['w_in'], p['g_v'], ws, bs, p['ar'], p['ai'],
            p['bsub'], p['csub'], p['d_skip'], p['w_glu'], p['b_glu'],
            p['g_out_a'], p['g_out_b'], p['w_out']]


def _ffn_consts(conv0, p):
    return [conv0, p['g_ffn'], p['w_ffn_in'], p['conv_w'], p['conv_b'], p['w_ffn_out'], p['g_final']]


def _mixer_prompt_call(x, h0, p, ws, bs):
    n_batch, n_l, _ = x.shape
    n_steps = n_l // CHUNK
    blk = pl.BlockSpec((n_batch, CHUNK, D_MODEL), lambda i: (0, i, 0))
    consts = _mixer_consts(h0, p, ws, bs)
    return pl.pallas_call(
        functools.partial(_mixer_prompt_kernel, n_batch, n_steps),
        grid=(n_steps,),
        in_specs=[blk] + [_const_spec(c.shape) for c in consts],
        out_specs=[blk, pl.BlockSpec((n_batch, STATE_W), lambda i: (0, 0))],
        out_shape=[jax.ShapeDtypeStruct(x.shape, F32),
                   jax.ShapeDtypeStruct((n_batch, STATE_W), F32)],
        scratch_shapes=[pltpu.VMEM((n_batch, STATE_W), F32)],
        compiler_params=_params(),
        name="mixer_prompt",
    )(x, *consts)


def _ffn_prompt_call(h, conv0, p):
    n_batch, n_l, _ = h.shape
    n_t = FFN_PROMPT_STEPS
    n_steps = n_l // n_t
    halo = (CONV_W - 1) * n_batch
    blk = pl.BlockSpec((n_batch, n_t, D_MODEL), lambda i: (0, i, 0))
    consts = _ffn_consts(conv0, p)
    return pl.pallas_call(
        functools.partial(_ffn_prompt_kernel, n_batch, n_t, n_steps),
        grid=(n_steps,),
        in_specs=[blk] + [_const_spec(c.shape) for c in consts],
        out_specs=[blk, pl.BlockSpec((halo, 2 * D_FF), lambda i: (0, 0))],
        out_shape=[jax.ShapeDtypeStruct(h.shape, F32),
                   jax.ShapeDtypeStruct((halo, 2 * D_FF), F32)],
        scratch_shapes=[pltpu.VMEM((halo, 2 * D_FF), F32)],
        compiler_params=_params(),
        name="ffn_prompt",
    )(h, *consts)


def _mixer_sample_call(x_tm, h0, p, ws, bs, n_batch, n_t):
    rows = n_batch * n_t
    consts = _mixer_consts(h0, p, ws, bs)
    return pl.pallas_call(
        functools.partial(_mixer_sample_kernel, n_batch, n_t),
        grid=(1,),
        in_specs=[_const_spec(x_tm.shape)] + [_const_spec(c.shape) for c in consts],
        out_specs=[pl.BlockSpec((rows, D_MODEL), lambda i: (0, 0)),
                   pl.BlockSpec((n_batch, STATE_W), lambda i: (0, 0)),
                   pl.BlockSpec((rows, A_WIDTH), lambda i: (0, 0))],
        out_shape=[jax.ShapeDtypeStruct((rows, D_MODEL), F32),
                   jax.ShapeDtypeStruct((n_batch, STATE_W), F32),
                   jax.ShapeDtypeStruct((rows, A_WIDTH), F32)],
        scratch_shapes=[pltpu.VMEM((rows, 2 * HALF_ST), F32),
                        pltpu.VMEM((n_batch, STATE_W), F32)],
        compiler_params=_params(),
        name="mixer_sample",
    )(x_tm, *consts)


def _ffn_sample_call(h_tm, conv0, p, n_batch, n_t):
    halo = (CONV_W - 1) * n_batch
    consts = _ffn_consts(conv0, p)
    return pl.pallas_call(
        functools.partial(_ffn_sample_kernel, n_batch, n_t),
        grid=(1,),
        in_specs=[_const_spec(h_tm.shape)] + [_const_spec(c.shape) for c in consts],
        out_specs=[pl.BlockSpec(h_tm.shape, lambda i: (0, 0)),
                   pl.BlockSpec((halo, 2 * D_FF), lambda i: (0, 0))],
        out_shape=[jax.ShapeDtypeStruct(h_tm.shape, F32),
                   jax.ShapeDtypeStruct((halo, 2 * D_FF), F32)],
        scratch_shapes=[pltpu.VMEM((halo, 2 * D_FF), F32)],
        compiler_params=_params(),
        name="ffn_sample",
    )(h_tm, *consts)


def _prep_params(g_mix, w_in, g_v, lam_re, lam_im, log_dt, b_re, b_im, c_re, c_im,
                 d_skip, w_glu, b_glu, g_out_a, g_out_b, w_out, g_ffn, w_ffn_in, conv_w, conv_b,
                 w_ffn_out, g_final):
    lr = lam_re.astype(F32)
    li = lam_im.astype(F32)
    dt = jnp.exp(log_dt.astype(F32))[:, None]
    mag = jnp.exp(lr * dt)
    ar = mag * jnp.cos(li * dt)
    ai = mag * jnp.sin(li * dt)
    den = lr * lr + li * li
    fr = ((ar - 1.0) * lr + ai * li) / den
    fi = (ai * lr - (ar - 1.0) * li) / den
    bre = b_re.astype(F32)
    bim = b_im.astype(F32)
    bbr = fr[..., None] * bre - fi[..., None] * bim
    bbi = fr[..., None] * bim + fi[..., None] * bre
    eye = jnp.eye(HALF_GROUPS, dtype=F32)

    def blockdiag_in(m):
        return jnp.einsum('gpc,gh->gchp', m, eye).reshape(HALF_CH, HALF_ST)

    def blockdiag_out(m):
        return jnp.einsum('gcp,gh->gphc', m, eye).reshape(HALF_ST, HALF_CH)

    bsub, csub, ars, ais = [], [], [], []
    for k in range(SSM_HALVES):
        gs = slice(k * HALF_GROUPS, (k + 1) * HALF_GROUPS)
        bsub.append(jnp.concatenate([blockdiag_in(bbr[gs]), blockdiag_in(bbi[gs])], axis=1))
        csub.append(jnp.concatenate([blockdiag_out(c_re[gs].astype(F32)),
                                     -blockdiag_out(c_im[gs].astype(F32))], axis=0))
        ars.append(ar[gs].reshape(1, HALF_ST))
        ais.append(ai[gs].reshape(1, HALF_ST))

    return dict(
        g_mix=g_mix.reshape(1, D_MODEL), w_in=w_in.astype(BF16), g_v=g_v,
        ar=jnp.concatenate(ars, axis=0), ai=jnp.concatenate(ais, axis=0),
        bsub=jnp.stack(bsub).astype(BF16), csub=jnp.stack(csub).astype(BF16),
        d_skip=d_skip.reshape(1, SSM_WIDTH), w_glu=w_glu.astype(BF16), b_glu=b_glu.reshape(1, SSM_WIDTH),
        g_out_a=g_out_a.reshape(1, A_WIDTH), g_out_b=g_out_b.reshape(1, SSM_WIDTH),
        w_out=w_out.astype(BF16), g_ffn=g_ffn.reshape(1, D_MODEL),
        w_ffn_in=w_ffn_in.astype(BF16), conv_w=conv_w, conv_b=conv_b.reshape(1, 2 * D_FF),
        w_ffn_out=w_ffn_out.astype(BF16), g_final=g_final.reshape(1, D_MODEL))


def _state_in(s_re, s_im):
    nb = s_re.shape[0]
    parts = []
    for k in range(SSM_HALVES):
        gs = slice(k * HALF_GROUPS, (k + 1) * HALF_GROUPS)
        parts += [s_re[:, gs].reshape(nb, HALF_ST), s_im[:, gs].reshape(nb, HALF_ST)]
    return jnp.concatenate(parts, axis=1)


def _state_out(st):
    nb = st.shape[0]
    st = st.reshape(nb, SSM_HALVES, 2, HALF_GROUPS, SSM_STATE)
    return (st[:, :, 0].reshape(nb, SSM_GROUPS, SSM_STATE),
            st[:, :, 1].reshape(nb, SSM_GROUPS, SSM_STATE))


def _conv_state_in(conv_buf):
    nb = conv_buf.shape[0]
    return jnp.transpose(conv_buf, (1, 0, 2)).reshape((CONV_W - 1) * nb, 2 * D_FF)


def _conv_state_out(rows_tm, nb):
    return jnp.transpose(rows_tm.reshape(CONV_W - 1, nb, 2 * D_FF), (1, 0, 2))


def kernel(x_prompt, x_sample, state_ssm_re, state_ssm_im, state_conv, g_mix, w_in, g_v, w_s, b_s,
           lam_re, lam_im, log_dt, b_re, b_im, c_re, c_im, d_skip, w_glu, b_glu, g_out_a, g_out_b,
           w_out, g_ffn, w_ffn_in, conv_w, conv_b, w_ffn_out, g_final):
    p = _prep_params(g_mix, w_in, g_v, lam_re, lam_im, log_dt, b_re, b_im, c_re, c_im, d_skip,
                     w_glu, b_glu, g_out_a, g_out_b, w_out, g_ffn, w_ffn_in, conv_w, conv_b,
                     w_ffn_out, g_final)
    tril = jnp.tril(w_s.astype(F32))

    nbp = x_prompt.shape[0]
    zero_state = jnp.zeros((nbp, SSM_GROUPS, SSM_STATE), state_ssm_re.dtype)
    zero_conv = jnp.zeros((nbp, CONV_W - 1, 2 * D_FF), state_conv.dtype)
    bs_p = jnp.broadcast_to(jnp.transpose(b_s)[:, :, None], (CHUNK, A_HEADS, A_HEAD_DIM)).reshape(CHUNK, A_WIDTH)
    h_p, st_p = _mixer_prompt_call(x_prompt, _state_in(zero_state, zero_state), p, tril.astype(BF16), bs_p)
    y_p, conv_p = _ffn_prompt_call(h_p, _conv_state_in(zero_conv), p)
    re_p, im_p = _state_out(st_p)
    conv_p = _conv_state_out(conv_p, nbp)

    nbs, nls, _ = x_sample.shape
    x_tm = jnp.transpose(x_sample, (1, 0, 2)).reshape(nls * nbs, D_MODEL)
    ws_s = jnp.broadcast_to(tril[:, :nls, :nls].reshape(A_HEADS * nls * nls, 1), (A_HEADS * nls * nls, A_HEAD_DIM))
    bs_s = jnp.broadcast_to(b_s[:, :nls].reshape(A_HEADS * nls, 1), (A_HEADS * nls, A_HEAD_DIM))
    h_s, st_s, v_s = _mixer_sample_call(x_tm, _state_in(state_ssm_re, state_ssm_im), p, ws_s, bs_s, nbs, nls)
    y_s, conv_s = _ffn_sample_call(h_s, _conv_state_in(state_conv), p, nbs, nls)
    y_s = jnp.transpose(y_s.reshape(nls, nbs, D_MODEL), (1, 0, 2))
    v_s = jnp.transpose(v_s.reshape(nls, nbs, A_HEADS, A_HEAD_DIM), (1, 0, 2, 3))
    re_s, im_s = _state_out(st_s)
    conv_s = _conv_state_out(conv_s, nbs)
    return (y_p, y_s, v_s, re_p, im_p, conv_p, re_s, im_s, conv_s)
```

```python
import functools
import math

import jax
import jax.numpy as jnp
from jax import lax
from jax.experimental import pallas as pl
from jax.experimental.pallas import tpu as pltpu

D_MODEL = 1024
A_HEADS = 4
A_HEAD_DIM = 128
A_WIDTH = 512
CHUNK = 128
SSM_WIDTH = 512
SSM_GROUP = 16
SSM_GROUPS = 32
SSM_STATE = 64
D_FF = 2816
CONV_W = 3
EPS = 1e-6

SSM_HALVES = 2
HALF_GROUPS = SSM_GROUPS // SSM_HALVES
HALF_CH = HALF_GROUPS * SSM_GROUP
HALF_ST = HALF_GROUPS * SSM_STATE
STATE_W = SSM_HALVES * 2 * HALF_ST

FF_CHUNK = 256
FF_NCHUNK = D_FF // FF_CHUNK
SUB_ROWS = 256
FFN_PROMPT_STEPS = 64

SUBLANES = 8
SCAN_LANES = 512
VMEM_LIMIT_BYTES = 60 * 1024 * 1024

BF16 = jnp.bfloat16
F32 = jnp.float32


def _rms(x, g):
    ms = jnp.mean(x * x, axis=-1, keepdims=True)
    return x * lax.rsqrt(ms + EPS) * g


def _gelu(x):
    c = math.sqrt(2.0 / math.pi)
    return x * (0.5 * (1.0 + jnp.tanh(c * (x + 0.044715 * (x * x * x)))))


def _dot(a, b):
    return jnp.dot(a, b, preferred_element_type=F32)


def _to_time_major(x, n_batch, n_t):
    w = x.shape[-1]
    return jnp.swapaxes(x.reshape(n_batch, n_t, w), 0, 1).reshape(n_t * n_batch, w)


def _to_batch_major(x, n_batch, n_t):
    w = x.shape[-1]
    return jnp.swapaxes(x.reshape(n_t, n_batch, w), 0, 1)


def _glu_norm(ys, xs, dskip_ref, wglu_ref, bglu_ref, gob_ref):
    g = _gelu(ys + dskip_ref[...] * xs)
    z = _dot(g.astype(BF16), wglu_ref[...]) + bglu_ref[...]
    return _rms(g * jax.nn.sigmoid(z), gob_ref[...])


def _ffn_subblocks(get_h, put_y, n_batch, n_sub, carry_scr,
                   gffn_ref, wfi_ref, cw_ref, cb_ref, wfo_ref, gfin_ref):
    halo = (CONV_W - 1) * n_batch
    hs, n2s = {}, {}

    def norm_in(s):
        hs[s] = get_h(s)
        n2s[s] = _rms(hs[s], gffn_ref[...]).astype(BF16)

    def col_slices(j):
        return [slice(c0, c0 + FF_CHUNK) for c0 in (j * FF_CHUNK, D_FF + j * FF_CHUNK)]

    def up_proj(j, s):
        return [jnp.dot(n2s[s], wfi_ref[:, cs], preferred_element_type=F32) for cs in col_slices(j)]

    order = [(j, s) for j in range(FF_NCHUNK) for s in range(n_sub)]
    accs = [jnp.zeros((SUB_ROWS, D_MODEL), F32) for _ in range(n_sub)]
    tails = [None, None]
    norm_in(0)
    ups_next = up_proj(*order[0])
    for s in range(1, n_sub):
        norm_in(s)
    for idx, (j, s) in enumerate(order):
        ups = ups_next
        if idx + 1 < len(order):
            ups_next = up_proj(*order[idx + 1])
        conv = []
        for ci, (cs, up) in enumerate(zip(col_slices(j), ups)):
            tail = carry_scr[:, cs] if s == 0 else tails[ci]
            padded = jnp.concatenate([tail, up], axis=0)
            tails[ci] = padded[SUB_ROWS:SUB_ROWS + halo, :]
            if s == n_sub - 1:
                carry_scr[:, cs] = tails[ci]
            out = cb_ref[:, cs]
            for tap in range(CONV_W):
                out = out + cw_ref[tap:tap + 1, cs] * padded[tap * n_batch:tap * n_batch + SUB_ROWS, :]
            conv.append(out)
        act = _gelu(conv[0]) * conv[1]
        accs[s] = accs[s] + jnp.dot(act.astype(BF16), wfo_ref[j * FF_CHUNK:(j + 1) * FF_CHUNK, :],
                                    preferred_element_type=F32)
        if j == FF_NCHUNK - 1:
            put_y(s, _rms(hs[s] + accs[s], gfin_ref[...]))


def _mixer_prompt_kernel(n_batch, n_steps,
                         x_ref, h0_ref, gmix_ref, win_ref, gv_ref, ws_ref, bs_ref,
                         ar_ref, ai_ref, bsub_ref, csub_ref, dskip_ref,
                         wglu_ref, bglu_ref, goa_ref, gob_ref, wout_ref,
                         hout_ref, st_ref, hst_ref):
    sub_t = SUB_ROWS // n_batch
    n_sub = CHUNK // sub_t
    i = pl.program_id(0)

    @pl.when(i == 0)
    def _():
        hst_ref[...] = h0_ref[...]

    state, trans = [], []
    for k in range(SSM_HALVES):
        base = k * 2 * HALF_ST
        state.append((hst_ref[:, base:base + HALF_ST], hst_ref[:, base + HALF_ST:base + 2 * HALF_ST]))
        trans.append((jnp.broadcast_to(ar_ref[k:k + 1, :], (n_batch, HALF_ST)),
                      jnp.broadcast_to(ai_ref[k:k + 1, :], (n_batch, HALF_ST))))

    x_sub, proj_sub, xs_sub, y_sub, bn_sub = {}, {}, {}, {}, {}
    a_heads = [[] for _ in range(n_sub)]

    def project(s):
        x = x_ref[:, s * sub_t:(s + 1) * sub_t, :].reshape(SUB_ROWS, D_MODEL)
        n1 = _rms(x, gmix_ref[...]).astype(BF16)
        x_sub[s] = x
        proj_sub[s] = _dot(n1, win_ref[...])
        xs_sub[s] = _to_time_major(proj_sub[s][:, 2 * A_WIDTH:], n_batch, sub_t)
        y_sub[s] = []

    def ssm_half(s, k):
        u = _dot(xs_sub[s].astype(BF16)[:, k * HALF_CH:(k + 1) * HALF_CH], bsub_ref[k])
        ar, ai = trans[k]
        hr, hi = state[k]
        slabs = []
        for t in range(sub_t):
            ur = u[t * n_batch:(t + 1) * n_batch, :HALF_ST]
            ui = u[t * n_batch:(t + 1) * n_batch, HALF_ST:]
            hr, hi = ar * hr - ai * hi + ur, ar * hi + ai * hr + ui
            slabs.append(jnp.concatenate([hr, hi], axis=-1))
        state[k] = (hr, hi)
        y_sub[s].append(_dot(jnp.concatenate(slabs, axis=0).astype(BF16), csub_ref[k]))

    def glu(s):
        bn_tm = _glu_norm(jnp.concatenate(y_sub[s], axis=-1), xs_sub[s], dskip_ref, wglu_ref, bglu_ref,
                          gob_ref)
        bn_sub[s] = _to_batch_major(bn_tm, n_batch, sub_t).reshape(SUB_ROWS, SSM_WIDTH)

    def gate_head(h):
        hs = slice(h * A_HEAD_DIM, (h + 1) * A_HEAD_DIM)
        vn = [_rms(proj_sub[s][:, A_WIDTH + h * A_HEAD_DIM:A_WIDTH + (h + 1) * A_HEAD_DIM],
                   gv_ref[h:h + 1, :]).astype(BF16) for s in range(n_sub)]
        v_wide = jnp.concatenate(
            [jnp.concatenate([vn[s][b * sub_t:(b + 1) * sub_t, :] for s in range(n_sub)], axis=0)
             for b in range(n_batch)], axis=-1)
        s_wide = _dot(ws_ref[h], v_wide)
        for s in range(n_sub):
            ts = slice(s * sub_t, (s + 1) * sub_t)
            bias = bs_ref[ts, hs]
            gate = jnp.concatenate([s_wide[ts, b * A_HEAD_DIM:(b + 1) * A_HEAD_DIM] + bias
                                    for b in range(n_batch)], axis=0)
            a_heads[s].append(proj_sub[s][:, hs] * gate)

    def out_proj(s):
        an = _rms(jnp.concatenate(a_heads[s], axis=-1), goa_ref[...])
        mix = jnp.concatenate([an, bn_sub[s]], axis=-1).astype(BF16)
        hout_ref[:, s * sub_t:(s + 1) * sub_t, :] = (
            x_sub[s] + _dot(mix, wout_ref[...])).reshape(n_batch, sub_t, D_MODEL)

    chain = [project] + [functools.partial(ssm_half, k=k) for k in range(SSM_HALVES)] + [glu]
    heads = list(range(A_HEADS))
    outs = list(range(n_sub))
    for wave in range(n_sub + len(chain) - 1):
        if wave >= n_sub:
            for h in heads[:2]:
                gate_head(h)
            heads = heads[2:]
        for s in range(n_sub):
            if 0 <= wave - s < len(chain):
                chain[wave - s](s)
        if not heads and outs and wave - outs[0] >= len(chain) - 1:
            out_proj(outs.pop(0))
    for h in heads:
        gate_head(h)
    for s in outs:
        out_proj(s)

    for k in range(SSM_HALVES):
        base = k * 2 * HALF_ST
        hst_ref[:, base:base + HALF_ST] = state[k][0]
        hst_ref[:, base + HALF_ST:base + 2 * HALF_ST] = state[k][1]

    @pl.when(i == n_steps - 1)
    def _():
        st_ref[...] = hst_ref[...]


def _ffn_prompt_kernel(n_batch, n_t, n_steps,
                       h_ref, conv0_ref, gffn_ref, wfi_ref, cw_ref, cb_ref, wfo_ref, gfin_ref,
                       y_ref, convout_ref, carry_scr):
    sub_t = SUB_ROWS // n_batch
    i = pl.program_id(0)

    @pl.when(i == 0)
    def _():
        carry_scr[...] = conv0_ref[...]

    def get_h(s):
        return _to_time_major(h_ref[:, s * sub_t:(s + 1) * sub_t, :].reshape(SUB_ROWS, D_MODEL),
                              n_batch, sub_t)

    def put_y(s, y):
        y_ref[:, s * sub_t:(s + 1) * sub_t, :] = _to_batch_major(y, n_batch, sub_t)

    _ffn_subblocks(get_h, put_y, n_batch, n_t // sub_t, carry_scr,
                   gffn_ref, wfi_ref, cw_ref, cb_ref, wfo_ref, gfin_ref)

    @pl.when(i == n_steps - 1)
    def _():
        convout_ref[...] = carry_scr[...]


def _scan_inplace(u_scr, hst_ref, ar_ref, ai_ref, k, n_batch, n_t):
    base = k * 2 * HALF_ST

    def row_chunk(rc, _):
        r0 = pl.multiple_of(rc * SUBLANES, SUBLANES)
        for c in range(HALF_ST // SCAN_LANES):
            lo = c * SCAN_LANES
            re_l = slice(lo, lo + SCAN_LANES)
            im_l = slice(HALF_ST + lo, HALF_ST + lo + SCAN_LANES)
            ar = jnp.broadcast_to(ar_ref[k:k + 1, re_l], (SUBLANES, SCAN_LANES))
            ai = jnp.broadcast_to(ai_ref[k:k + 1, re_l], (SUBLANES, SCAN_LANES))
            hr0 = hst_ref[pl.ds(r0, SUBLANES), base + lo:base + lo + SCAN_LANES]
            hi0 = hst_ref[pl.ds(r0, SUBLANES), base + HALF_ST + lo:base + HALF_ST + lo + SCAN_LANES]

            def step(t, carry):
                hr, hi = carry
                row = pl.multiple_of(t * n_batch + r0, SUBLANES)
                ur = u_scr[pl.ds(row, SUBLANES), re_l]
                ui = u_scr[pl.ds(row, SUBLANES), im_l]
                nhr = ar * hr - ai * hi + ur
                nhi = ar * hi + ai * hr + ui
                u_scr[pl.ds(row, SUBLANES), re_l] = nhr
                u_scr[pl.ds(row, SUBLANES), im_l] = nhi
                return nhr, nhi

            hr, hi = lax.fori_loop(0, n_t, step, (hr0, hi0), unroll=min(n_t, 4))
            hst_ref[pl.ds(r0, SUBLANES), base + lo:base + lo + SCAN_LANES] = hr
            hst_ref[pl.ds(r0, SUBLANES), base + HALF_ST + lo:base + HALF_ST + lo + SCAN_LANES] = hi
        return 0

    lax.fori_loop(0, n_batch // SUBLANES, row_chunk, 0)


def _mixer_sample_kernel(n_batch, n_t,
                         x_ref, h0_ref, gmix_ref, win_ref, gv_ref, ws_ref, bs_ref,
                         ar_ref, ai_ref, bsub_ref, csub_ref, dskip_ref,
                         wglu_ref, bglu_ref, goa_ref, gob_ref, wout_ref,
                         hout_ref, st_ref, v_hbm, x_buf, v_buf, u_scr, hst_ref, sem):
    x_in = [pltpu.make_async_copy(x_ref.at[:, t, :], x_buf.at[pl.ds(t * n_batch, n_batch), :], sem.at[t])
            for t in range(n_t)]
    v_out = [pltpu.make_async_copy(
        v_buf.at[pl.ds(t * n_batch, n_batch), pl.ds(h * A_HEAD_DIM, A_HEAD_DIM)],
        v_hbm.at[:, t, h, :], sem.at[n_t + t * A_HEADS + h]) for t in range(n_t) for h in range(A_HEADS)]
    for cp in x_in:
        cp.start()
    hst_ref[...] = h0_ref[...]
    for cp in x_in:
        cp.wait()
    x = x_buf[...]
    n1 = _rms(x, gmix_ref[...]).astype(BF16)
    proj = _dot(n1, win_ref[...])

    a_heads = []
    for h in range(A_HEADS):
        hs = slice(h * A_HEAD_DIM, (h + 1) * A_HEAD_DIM)
        vn = _rms(proj[:, A_WIDTH + h * A_HEAD_DIM:A_WIDTH + (h + 1) * A_HEAD_DIM], gv_ref[h:h + 1, :])
        v_buf[:, hs] = vn
        slabs = []
        for t in range(n_t):
            s = jnp.broadcast_to(bs_ref[h * n_t + t:h * n_t + t + 1, :], (n_batch, A_HEAD_DIM))
            for src in range(t + 1):
                r = (h * n_t + t) * n_t + src
                s = s + ws_ref[r:r + 1, :] * vn[src * n_batch:(src + 1) * n_batch, :]
            slabs.append(proj[t * n_batch:(t + 1) * n_batch, hs] * s)
        a_heads.append(jnp.concatenate(slabs, axis=0))
    an = _rms(jnp.concatenate(a_heads, axis=-1), goa_ref[...])
    for cp in v_out:
        cp.start()

    xs = proj[:, 2 * A_WIDTH:]
    xs_b = xs.astype(BF16)
    y_parts = []
    for k in range(SSM_HALVES):
        u_scr[...] = _dot(xs_b[:, k * HALF_CH:(k + 1) * HALF_CH], bsub_ref[k])
        _scan_inplace(u_scr, hst_ref, ar_ref, ai_ref, k, n_batch, n_t)
        y_parts.append(_dot(u_scr[...].astype(BF16), csub_ref[k]))
    bn = _glu_norm(jnp.concatenate(y_parts, axis=-1), xs, dskip_ref, wglu_ref, bglu_ref, gob_ref)

    mix = jnp.concatenate([an, bn], axis=-1).astype(BF16)
    hout_ref[...] = x + _dot(mix, wout_ref[...])
    st_ref[...] = hst_ref[...]
    for cp in v_out:
        cp.wait()


def _ffn_sample_kernel(n_batch, n_t,
                       h_ref, conv_hbm, gffn_ref, wfi_ref, cw_ref, cb_ref, wfo_ref, gfin_ref,
                       y_hbm, convout_hbm, carry_scr, y_buf, sem):
    n_hist = CONV_W - 1
    conv_in = [pltpu.make_async_copy(conv_hbm.at[:, k, :], carry_scr.at[pl.ds(k * n_batch, n_batch), :],
                                     sem.at[k]) for k in range(n_hist)]
    conv_out = [pltpu.make_async_copy(carry_scr.at[pl.ds(k * n_batch, n_batch), :], convout_hbm.at[:, k, :],
                                      sem.at[k]) for k in range(n_hist)]
    y_out = [pltpu.make_async_copy(y_buf.at[pl.ds(t * n_batch, n_batch), :], y_hbm.at[:, t, :],
                                   sem.at[n_hist + t]) for t in range(n_t)]
    for cp in conv_in:
        cp.start()
    for cp in conv_in:
        cp.wait()

    def get_h(s):
        return h_ref[s * SUB_ROWS:(s + 1) * SUB_ROWS, :]

    def put_y(s, y):
        y_buf[s * SUB_ROWS:(s + 1) * SUB_ROWS, :] = y

    _ffn_subblocks(get_h, put_y, n_batch, n_t * n_batch // SUB_ROWS, carry_scr,
                   gffn_ref, wfi_ref, cw_ref, cb_ref, wfo_ref, gfin_ref)
    for cp in conv_out + y_out:
        cp.start()
    for cp in conv_out + y_out:
        cp.wait()


def _const_spec(shape):
    zeros = (0,) * len(shape)
    return pl.BlockSpec(shape, lambda i: zeros, pipeline_mode=pl.Buffered(1))


def _params():
    return pltpu.CompilerParams(dimension_semantics=("arbitrary",), vmem_limit_bytes=VMEM_LIMIT_BYTES)


def _mixer_consts(h0, p, ws, bs):
    return [h0, p['g_mix'], p['w_in'], p['g_v'], ws, bs, p['ar'], p['ai'],
            p['bsub'], p['csub'], p['d_skip'], p['w_glu'], p['b_glu'],
            p['g_out_a'], p['g_out_b'], p['w_out']]


def _ffn_consts(conv0, p):
    return [conv0, p['g_ffn'], p['w_ffn_in'], p['conv_w'], p['conv_b'], p['w_ffn_out'], p['g_final']]


def _mixer_prompt_call(x, h0, p, ws, bs):
    n_batch, n_l, _ = x.shape
    n_steps = n_l // CHUNK
    blk = pl.BlockSpec((n_batch, CHUNK, D_MODEL), lambda i: (0, i, 0))
    consts = _mixer_consts(h0, p, ws, bs)
    return pl.pallas_call(
        functools.partial(_mixer_prompt_kernel, n_batch, n_steps),
        grid=(n_steps,),
        in_specs=[blk] + [_const_spec(c.shape) for c in consts],
        out_specs=[blk, pl.BlockSpec((n_batch, STATE_W), lambda i: (0, 0))],
        out_shape=[jax.ShapeDtypeStruct(x.shape, F32),
                   jax.ShapeDtypeStruct((n_batch, STATE_W), F32)],
        scratch_shapes=[pltpu.VMEM((n_batch, STATE_W), F32)],
        compiler_params=_params(),
        name="mixer_prompt",
    )(x, *consts)


def _ffn_prompt_call(h, conv0, p):
    n_batch, n_l, _ = h.shape
    n_t = FFN_PROMPT_STEPS
    n_steps = n_l // n_t
    halo = (CONV_W - 1) * n_batch
    blk = pl.BlockSpec((n_batch, n_t, D_MODEL), lambda i: (0, i, 0))
    consts = _ffn_consts(conv0, p)
    return pl.pallas_call(
        functools.partial(_ffn_prompt_kernel, n_batch, n_t, n_steps),
        grid=(n_steps,),
        in_specs=[blk] + [_const_spec(c.shape) for c in consts],
        out_specs=[blk, pl.BlockSpec((halo, 2 * D_FF), lambda i: (0, 0))],
        out_shape=[jax.ShapeDtypeStruct(h.shape, F32),
                   jax.ShapeDtypeStruct((halo, 2 * D_FF), F32)],
        scratch_shapes=[pltpu.VMEM((halo, 2 * D_FF), F32)],
        compiler_params=_params(),
        name="ffn_prompt",
    )(h, *consts)


def _mixer_sample_call(x, h0, p, ws, bs):
    n_batch, n_t, _ = x.shape
    rows = n_batch * n_t
    consts = _mixer_consts(h0, p, ws, bs)
    return pl.pallas_call(
        functools.partial(_mixer_sample_kernel, n_batch, n_t),
        grid=(1,),
        in_specs=[pl.BlockSpec(memory_space=pl.ANY)] + [_const_spec(c.shape) for c in consts],
        out_specs=[pl.BlockSpec((rows, D_MODEL), lambda i: (0, 0)),
                   pl.BlockSpec((n_batch, STATE_W), lambda i: (0, 0)),
                   pl.BlockSpec(memory_space=pl.ANY)],
        out_shape=[jax.ShapeDtypeStruct((rows, D_MODEL), F32),
                   jax.ShapeDtypeStruct((n_batch, STATE_W), F32),
                   jax.ShapeDtypeStruct((n_batch, n_t, A_HEADS, A_HEAD_DIM), F32)],
        scratch_shapes=[pltpu.VMEM((rows, D_MODEL), F32),
                        pltpu.VMEM((rows, A_WIDTH), F32),
                        pltpu.VMEM((rows, 2 * HALF_ST), F32),
                        pltpu.VMEM((n_batch, STATE_W), F32),
                        pltpu.SemaphoreType.DMA((n_t + n_t * A_HEADS,))],
        compiler_params=_params(),
        name="mixer_sample",
    )(x, *consts)


def _ffn_sample_call(h_tm, conv_buf, p):
    n_batch, n_hist, _ = conv_buf.shape
    rows = h_tm.shape[0]
    n_t = rows // n_batch
    consts = _ffn_consts(conv_buf, p)[1:]
    return pl.pallas_call(
        functools.partial(_ffn_sample_kernel, n_batch, n_t),
        grid=(1,),
        in_specs=[_const_spec(h_tm.shape), pl.BlockSpec(memory_space=pl.ANY)]
        + [_const_spec(c.shape) for c in consts],
        out_specs=[pl.BlockSpec(memory_space=pl.ANY), pl.BlockSpec(memory_space=pl.ANY)],
        out_shape=[jax.ShapeDtypeStruct((n_batch, n_t, D_MODEL), F32),
                   jax.ShapeDtypeStruct(conv_buf.shape, F32)],
        scratch_shapes=[pltpu.VMEM((n_hist * n_batch, 2 * D_FF), F32),
                        pltpu.VMEM((rows, D_MODEL), F32),
                        pltpu.SemaphoreType.DMA((n_hist + n_t,))],
        compiler_params=_params(),
        name="ffn_sample",
    )(h_tm, conv_buf, *consts)


def _prep_params(g_mix, w_in, g_v, lam_re, lam_im, log_dt, b_re, b_im, c_re, c_im,
                 d_skip, w_glu, b_glu, g_out_a, g_out_b, w_out, g_ffn, w_ffn_in, conv_w, conv_b,
                 w_ffn_out, g_final):
    lr = lam_re.astype(F32)
    li = lam_im.astype(F32)
    dt = jnp.exp(log_dt.astype(F32))[:, None]
    mag = jnp.exp(lr * dt)
    ar = mag * jnp.cos(li * dt)
    ai = mag * jnp.sin(li * dt)
    den = lr * lr + li * li
    fr = ((ar - 1.0) * lr + ai * li) / den
    fi = (ai * lr - (ar - 1.0) * li) / den
    bre = b_re.astype(F32)
    bim = b_im.astype(F32)
    bbr = fr[..., None] * bre - fi[..., None] * bim
    bbi = fr[..., None] * bim + fi[..., None] * bre
    eye = jnp.eye(HALF_GROUPS, dtype=F32)

    def blockdiag_in(m):
        return jnp.einsum('gpc,gh->gchp', m, eye).reshape(HALF_CH, HALF_ST)

    def blockdiag_out(m):
        return jnp.einsum('gcp,gh->gphc', m, eye).reshape(HALF_ST, HALF_CH)

    bsub, csub, ars, ais = [], [], [], []
    for k in range(SSM_HALVES):
        gs = slice(k * HALF_GROUPS, (k + 1) * HALF_GROUPS)
        bsub.append(jnp.concatenate([blockdiag_in(bbr[gs]), blockdiag_in(bbi[gs])], axis=1))
        csub.append(jnp.concatenate([blockdiag_out(c_re[gs].astype(F32)),
                                     -blockdiag_out(c_im[gs].astype(F32))], axis=0))
        ars.append(ar[gs].reshape(1, HALF_ST))
        ais.append(ai[gs].reshape(1, HALF_ST))

    return dict(
        g_mix=g_mix.reshape(1, D_MODEL), w_in=w_in.astype(BF16), g_v=g_v,
        ar=jnp.concatenate(ars, axis=0), ai=jnp.concatenate(ais, axis=0),
        bsub=jnp.stack(bsub).astype(BF16), csub=jnp.stack(csub).astype(BF16),
        d_skip=d_skip.reshape(1, SSM_WIDTH), w_glu=w_glu.astype(BF16), b_glu=b_glu.reshape(1, SSM_WIDTH),
        g_out_a=g_out_a.reshape(1, A_WIDTH), g_out_b=g_out_b.reshape(1, SSM_WIDTH),
        w_out=w_out.astype(BF16), g_ffn=g_ffn.reshape(1, D_MODEL),
        w_ffn_in=w_ffn_in.astype(BF16), conv_w=conv_w, conv_b=conv_b.reshape(1, 2 * D_FF),
        w_ffn_out=w_ffn_out.astype(BF16), g_final=g_final.reshape(1, D_MODEL))


def _state_in(s_re, s_im):
    nb = s_re.shape[0]
    parts = []
    for k in range(SSM_HALVES):
        gs = slice(k * HALF_GROUPS, (k + 1) * HALF_GROUPS)
        parts += [s_re[:, gs].reshape(nb, HALF_ST), s_im[:, gs].reshape(nb, HALF_ST)]
    return jnp.concatenate(parts, axis=1)


def _state_out(st):
    nb = st.shape[0]
    st = st.reshape(nb, SSM_HALVES, 2, HALF_GROUPS, SSM_STATE)
    return (st[:, :, 0].reshape(nb, SSM_GROUPS, SSM_STATE),
            st[:, :, 1].reshape(nb, SSM_GROUPS, SSM_STATE))


def _conv_state_in(conv_buf):
    nb = conv_buf.shape[0]
    return jnp.transpose(conv_buf, (1, 0, 2)).reshape((CONV_W - 1) * nb, 2 * D_FF)


def _conv_state_out(rows_tm, nb):
    return jnp.transpose(rows_tm.reshape(CONV_W - 1, nb, 2 * D_FF), (1, 0, 2))


def kernel(x_prompt, x_sample, state_ssm_re, state_ssm_im, state_conv, g_mix, w_in, g_v, w_s, b_s,
           lam_re, lam_im, log_dt, b_re, b_im, c_re, c_im, d_skip, w_glu, b_glu, g_out_a, g_out_b,
           w_out, g_ffn, w_ffn_in, conv_w, conv_b, w_ffn_out, g_final):
    p = _prep_params(g_mix, w_in, g_v, lam_re, lam_im, log_dt, b_re, b_im, c_re, c_im, d_skip,
                     w_glu, b_glu, g_out_a, g_out_b, w_out, g_ffn, w_ffn_in, conv_w, conv_b,
                     w_ffn_out, g_final)
    tril = jnp.tril(w_s.astype(F32))

    nbp = x_prompt.shape[0]
    zero_state = jnp.zeros((nbp, SSM_GROUPS, SSM_STATE), state_ssm_re.dtype)
    zero_conv = jnp.zeros((nbp, CONV_W - 1, 2 * D_FF), state_conv.dtype)
    bs_p = jnp.broadcast_to(jnp.transpose(b_s)[:, :, None], (CHUNK, A_HEADS, A_HEAD_DIM)).reshape(CHUNK, A_WIDTH)
    h_p, st_p = _mixer_prompt_call(x_prompt, _state_in(zero_state, zero_state), p, tril.astype(BF16), bs_p)
    y_p, conv_p = _ffn_prompt_call(h_p, _conv_state_in(zero_conv), p)
    re_p, im_p = _state_out(st_p)
    conv_p = _conv_state_out(conv_p, nbp)

    nls = x_sample.shape[1]
    ws_s = jnp.broadcast_to(tril[:, :nls, :nls].reshape(A_HEADS * nls * nls, 1), (A_HEADS * nls * nls, A_HEAD_DIM))
    bs_s = jnp.broadcast_to(b_s[:, :nls].reshape(A_HEADS * nls, 1), (A_HEADS * nls, A_HEAD_DIM))
    h_s, st_s, v_s = _mixer_sample_call(x_sample, _state_in(state_ssm_re, state_ssm_im), p, ws_s, bs_s)
    y_s, conv_s = _ffn_sample_call(h_s, state_conv, p)
    re_s, im_s = _state_out(st_s)
    return (y_p, y_s, v_s, re_p, im_p, conv_p, re_s, im_s, conv_s)
```

```python
import functools
import math

import jax
import jax.numpy as jnp
from jax import lax
from jax.experimental import pallas as pl
from jax.experimental.pallas import tpu as pltpu

D_MODEL = 1024
A_HEADS = 4
A_HEAD_DIM = 128
A_WIDTH = 512
CHUNK = 128
SSM_WIDTH = 512
SSM_GROUP = 16
SSM_GROUPS = 32
SSM_STATE = 64
D_FF = 2816
CONV_W = 3
EPS = 1e-6

SSM_HALVES = 2
HALF_GROUPS = SSM_GROUPS // SSM_HALVES
HALF_CH = HALF_GROUPS * SSM_GROUP
HALF_ST = HALF_GROUPS * SSM_STATE
ALL_ST = SSM_HALVES * HALF_ST
STATE_W = 2 * ALL_ST

FF_CHUNK = 256
FF_NCHUNK = D_FF // FF_CHUNK
SUB_ROWS = 256
FFN_PROMPT_STEPS = 64

SUBLANES = 8
SCAN_LANES = 512
VMEM_LIMIT_BYTES = 60 * 1024 * 1024

BF16 = jnp.bfloat16
F32 = jnp.float32


def _state_lanes(k):
    return k * HALF_ST, ALL_ST + k * HALF_ST


def _rms(x, g):
    ms = jnp.mean(x * x, axis=-1, keepdims=True)
    return x * lax.rsqrt(ms + EPS) * g


def _gelu(x):
    c = math.sqrt(2.0 / math.pi)
    return x * (0.5 * (1.0 + jnp.tanh(c * (x + 0.044715 * (x * x * x)))))


def _dot(a, b):
    return jnp.dot(a, b, preferred_element_type=F32)


def _to_time_major(x, n_batch, n_t):
    w = x.shape[-1]
    return jnp.swapaxes(x.reshape(n_batch, n_t, w), 0, 1).reshape(n_t * n_batch, w)


def _to_batch_major(x, n_batch, n_t):
    w = x.shape[-1]
    return jnp.swapaxes(x.reshape(n_t, n_batch, w), 0, 1)


def _glu_norm(ys, xs, dskip_ref, wglu_ref, bglu_ref, gob_ref):
    g = _gelu(ys + dskip_ref[...] * xs)
    z = _dot(g.astype(BF16), wglu_ref[...]) + bglu_ref[...]
    return _rms(g * jax.nn.sigmoid(z), gob_ref[...])


def _ffn_subblocks(get_h, put_y, n_batch, n_sub, carry_scr,
                   gffn_ref, wfi_ref, cw_ref, cb_ref, wfo_ref, gfin_ref):
    halo = (CONV_W - 1) * n_batch
    hs, n2s = {}, {}

    def norm_in(s):
        hs[s] = get_h(s)
        n2s[s] = _rms(hs[s], gffn_ref[...]).astype(BF16)

    def col_slices(j):
        return [slice(c0, c0 + FF_CHUNK) for c0 in (j * FF_CHUNK, D_FF + j * FF_CHUNK)]

    def up_proj(j, s):
        return [jnp.dot(n2s[s], wfi_ref[:, cs], preferred_element_type=F32) for cs in col_slices(j)]

    order = [(j, s) for j in range(FF_NCHUNK) for s in range(n_sub)]
    accs = [jnp.zeros((SUB_ROWS, D_MODEL), F32) for _ in range(n_sub)]
    tails = [None, None]
    norm_in(0)
    ups_next = up_proj(*order[0])
    for s in range(1, n_sub):
        norm_in(s)
    for idx, (j, s) in enumerate(order):
        ups = ups_next
        if idx + 1 < len(order):
            ups_next = up_proj(*order[idx + 1])
        conv = []
        for ci, (cs, up) in enumerate(zip(col_slices(j), ups)):
            tail = carry_scr[:, cs] if s == 0 else tails[ci]
            padded = jnp.concatenate([tail, up], axis=0)
            tails[ci] = padded[SUB_ROWS:SUB_ROWS + halo, :]
            if s == n_sub - 1:
                carry_scr[:, cs] = tails[ci]
            out = cb_ref[:, cs]
            for tap in range(CONV_W):
                out = out + cw_ref[tap:tap + 1, cs] * padded[tap * n_batch:tap * n_batch + SUB_ROWS, :]
            conv.append(out)
        act = _gelu(conv[0]) * conv[1]
        accs[s] = accs[s] + jnp.dot(act.astype(BF16), wfo_ref[j * FF_CHUNK:(j + 1) * FF_CHUNK, :],
                                    preferred_element_type=F32)
        if j == FF_NCHUNK - 1:
            put_y(s, _rms(hs[s] + accs[s], gfin_ref[...]))


def _mixer_prompt_kernel(n_batch, n_steps,
                         x_ref, h0_ref, gmix_ref, win_ref, gv_ref, ws_ref, bs_ref,
                         ar_ref, ai_ref, bsub_ref, csub_ref, dskip_ref,
                         wglu_ref, bglu_ref, goa_ref, gob_ref, wout_ref, wfi32_ref, wfo32_ref,
                         hout_ref, st_ref, wfi16_ref, wfo16_ref, hst_ref):
    sub_t = SUB_ROWS // n_batch
    n_sub = CHUNK // sub_t
    i = pl.program_id(0)

    @pl.when(i == 0)
    def _():
        hst_ref[...] = h0_ref[...]

    state, trans = [], []
    for k in range(SSM_HALVES):
        re0, im0 = _state_lanes(k)
        state.append((hst_ref[:, re0:re0 + HALF_ST], hst_ref[:, im0:im0 + HALF_ST]))
        trans.append((jnp.broadcast_to(ar_ref[k:k + 1, :], (n_batch, HALF_ST)),
                      jnp.broadcast_to(ai_ref[k:k + 1, :], (n_batch, HALF_ST))))

    x_sub, proj_sub, xs_sub, y_sub, bn_sub = {}, {}, {}, {}, {}
    a_heads = [[] for _ in range(n_sub)]

    def project(s):
        x = x_ref[:, s * sub_t:(s + 1) * sub_t, :].reshape(SUB_ROWS, D_MODEL)
        n1 = _rms(x, gmix_ref[...]).astype(BF16)
        x_sub[s] = x
        proj_sub[s] = _dot(n1, win_ref[...])
        xs_sub[s] = _to_time_major(proj_sub[s][:, 2 * A_WIDTH:], n_batch, sub_t)
        y_sub[s] = []

    def ssm_half(s, k):
        u = _dot(xs_sub[s].astype(BF16)[:, k * HALF_CH:(k + 1) * HALF_CH], bsub_ref[k])
        ar, ai = trans[k]
        hr, hi = state[k]
        slabs = []
        for t in range(sub_t):
            ur = u[t * n_batch:(t + 1) * n_batch, :HALF_ST]
            ui = u[t * n_batch:(t + 1) * n_batch, HALF_ST:]
            hr, hi = ar * hr - ai * hi + ur, ar * hi + ai * hr + ui
            slabs.append(jnp.concatenate([hr, hi], axis=-1))
        state[k] = (hr, hi)
        y_sub[s].append(_dot(jnp.concatenate(slabs, axis=0).astype(BF16), csub_ref[k]))

    def glu(s):
        bn_tm = _glu_norm(jnp.concatenate(y_sub[s], axis=-1), xs_sub[s], dskip_ref, wglu_ref, bglu_ref,
                          gob_ref)
        bn_sub[s] = _to_batch_major(bn_tm, n_batch, sub_t).reshape(SUB_ROWS, SSM_WIDTH)

    def gate_head(h):
        hs = slice(h * A_HEAD_DIM, (h + 1) * A_HEAD_DIM)
        vn = [_rms(proj_sub[s][:, A_WIDTH + h * A_HEAD_DIM:A_WIDTH + (h + 1) * A_HEAD_DIM],
                   gv_ref[h:h + 1, :]).astype(BF16) for s in range(n_sub)]
        v_wide = jnp.concatenate(
            [jnp.concatenate([vn[s][b * sub_t:(b + 1) * sub_t, :] for s in range(n_sub)], axis=0)
             for b in range(n_batch)], axis=-1)
        s_wide = _dot(ws_ref[h], v_wide)
        for s in range(n_sub):
            ts = slice(s * sub_t, (s + 1) * sub_t)
            bias = bs_ref[ts, hs]
            gate = jnp.concatenate([s_wide[ts, b * A_HEAD_DIM:(b + 1) * A_HEAD_DIM] + bias
                                    for b in range(n_batch)], axis=0)
            a_heads[s].append(proj_sub[s][:, hs] * gate)

    def out_proj(s):
        an = _rms(jnp.concatenate(a_heads[s], axis=-1), goa_ref[...])
        mix = jnp.concatenate([an, bn_sub[s]], axis=-1).astype(BF16)
        hout_ref[:, s * sub_t:(s + 1) * sub_t, :] = (
            x_sub[s] + _dot(mix, wout_ref[...])).reshape(n_batch, sub_t, D_MODEL)

    chain = [project] + [functools.partial(ssm_half, k=k) for k in range(SSM_HALVES)] + [glu]
    heads = list(range(A_HEADS))
    outs = list(range(n_sub))
    for wave in range(n_sub + len(chain) - 1):
        if wave >= n_sub:
            for h in heads[:2]:
                gate_head(h)
            heads = heads[2:]
        for s in range(n_sub):
            if 0 <= wave - s < len(chain):
                chain[wave - s](s)
        if not heads and outs and wave - outs[0] >= len(chain) - 1:
            out_proj(outs.pop(0))
    for h in heads:
        gate_head(h)
    for s in outs:
        out_proj(s)

    for k in range(SSM_HALVES):
        re0, im0 = _state_lanes(k)
        hst_ref[:, re0:re0 + HALF_ST] = state[k][0]
        hst_ref[:, im0:im0 + HALF_ST] = state[k][1]

    wfi16_ref[...] = wfi32_ref[...].astype(BF16)
    wfo16_ref[...] = wfo32_ref[...].astype(BF16)

    @pl.when(i == n_steps - 1)
    def _():
        st_ref[...] = hst_ref[...]


def _ffn_prompt_kernel(n_batch, n_t, n_steps,
                       h_ref, conv0_ref, gffn_ref, wfi_ref, cw_ref, cb_ref, wfo_ref, gfin_ref,
                       y_ref, convout_ref, carry_scr):
    sub_t = SUB_ROWS // n_batch
    i = pl.program_id(0)

    @pl.when(i == 0)
    def _():
        carry_scr[...] = conv0_ref[...]

    def get_h(s):
        return _to_time_major(h_ref[:, s * sub_t:(s + 1) * sub_t, :].reshape(SUB_ROWS, D_MODEL),
                              n_batch, sub_t)

    def put_y(s, y):
        y_ref[:, s * sub_t:(s + 1) * sub_t, :] = _to_batch_major(y, n_batch, sub_t)

    _ffn_subblocks(get_h, put_y, n_batch, n_t // sub_t, carry_scr,
                   gffn_ref, wfi_ref, cw_ref, cb_ref, wfo_ref, gfin_ref)

    @pl.when(i == n_steps - 1)
    def _():
        convout_ref[...] = carry_scr[...]


def _scan_inplace(u_scr, hst_ref, ar_ref, ai_ref, k, n_batch, n_t):
    re0, im0 = _state_lanes(k)

    def row_chunk(rc, _):
        r0 = pl.multiple_of(rc * SUBLANES, SUBLANES)
        for c in range(HALF_ST // SCAN_LANES):
            lo = c * SCAN_LANES
            re_l = slice(lo, lo + SCAN_LANES)
            im_l = slice(HALF_ST + lo, HALF_ST + lo + SCAN_LANES)
            ar = jnp.broadcast_to(ar_ref[k:k + 1, re_l], (SUBLANES, SCAN_LANES))
            ai = jnp.broadcast_to(ai_ref[k:k + 1, re_l], (SUBLANES, SCAN_LANES))
            hr0 = hst_ref[pl.ds(r0, SUBLANES), re0 + lo:re0 + lo + SCAN_LANES]
            hi0 = hst_ref[pl.ds(r0, SUBLANES), im0 + lo:im0 + lo + SCAN_LANES]

            def step(t, carry):
                hr, hi = carry
                row = pl.multiple_of(t * n_batch + r0, SUBLANES)
                ur = u_scr[pl.ds(row, SUBLANES), re_l]
                ui = u_scr[pl.ds(row, SUBLANES), im_l]
                nhr = ar * hr - ai * hi + ur
                nhi = ar * hi + ai * hr + ui
                u_scr[pl.ds(row, SUBLANES), re_l] = nhr
                u_scr[pl.ds(row, SUBLANES), im_l] = nhi
                return nhr, nhi

            hr, hi = lax.fori_loop(0, n_t, step, (hr0, hi0), unroll=min(n_t, 4))
            hst_ref[pl.ds(r0, SUBLANES), re0 + lo:re0 + lo + SCAN_LANES] = hr
            hst_ref[pl.ds(r0, SUBLANES), im0 + lo:im0 + lo + SCAN_LANES] = hi
        return 0

    lax.fori_loop(0, n_batch // SUBLANES, row_chunk, 0)


def _mixer_sample_kernel(n_batch, n_t,
                         x_ref, h0_ref, gmix_ref, win_ref, gv_ref, ws_ref, bs_ref,
                         ar_ref, ai_ref, bsub_ref, csub_ref, dskip_ref,
                         wglu_ref, bglu_ref, goa_ref, gob_ref, wout_ref,
                         hout_ref, st_ref, v_hbm, x_buf, v_buf, u_scr, hst_ref, sem):
    x_in = [pltpu.make_async_copy(x_ref.at[:, t, :], x_buf.at[pl.ds(t * n_batch, n_batch), :], sem.at[t])
            for t in range(n_t)]
    v_out = [pltpu.make_async_copy(
        v_buf.at[pl.ds(t * n_batch, n_batch), pl.ds(h * A_HEAD_DIM, A_HEAD_DIM)],
        v_hbm.at[:, t, h, :], sem.at[n_t + t * A_HEADS + h]) for t in range(n_t) for h in range(A_HEADS)]
    for cp in x_in:
        cp.start()
    hst_ref[...] = h0_ref[...]
    for cp in x_in:
        cp.wait()
    x = x_buf[...]
    n1 = _rms(x, gmix_ref[...]).astype(BF16)
    proj = _dot(n1, win_ref[...])

    a_heads = []
    for h in range(A_HEADS):
        hs = slice(h * A_HEAD_DIM, (h + 1) * A_HEAD_DIM)
        vn = _rms(proj[:, A_WIDTH + h * A_HEAD_DIM:A_WIDTH + (h + 1) * A_HEAD_DIM], gv_ref[h:h + 1, :])
        v_buf[:, hs] = vn
        slabs = []
        for t in range(n_t):
            s = jnp.broadcast_to(bs_ref[h * n_t + t:h * n_t + t + 1, :], (n_batch, A_HEAD_DIM))
            for src in range(t + 1):
                r = (h * n_t + t) * n_t + src
                s = s + ws_ref[r:r + 1, :] * vn[src * n_batch:(src + 1) * n_batch, :]
            slabs.append(proj[t * n_batch:(t + 1) * n_batch, hs] * s)
        a_heads.append(jnp.concatenate(slabs, axis=0))
    an = _rms(jnp.concatenate(a_heads, axis=-1), goa_ref[...])
    for cp in v_out:
        cp.start()

    xs = proj[:, 2 * A_WIDTH:]
    xs_b = xs.astype(BF16)
    y_parts = []
    for k in range(SSM_HALVES):
        u_scr[...] = _dot(xs_b[:, k * HALF_CH:(k + 1) * HALF_CH], bsub_ref[k])
        _scan_inplace(u_scr, hst_ref, ar_ref, ai_ref, k, n_batch, n_t)
        y_parts.append(_dot(u_scr[...].astype(BF16), csub_ref[k]))
    bn = _glu_norm(jnp.concatenate(y_parts, axis=-1), xs, dskip_ref, wglu_ref, bglu_ref, gob_ref)

    mix = jnp.concatenate([an, bn], axis=-1).astype(BF16)
    hout_ref[...] = x + _dot(mix, wout_ref[...])
    st_ref[...] = hst_ref[...]
    for cp in v_out:
        cp.wait()


def _ffn_sample_kernel(n_batch, n_t,
                       h_ref, conv_hbm, gffn_ref, wfi_ref, cw_ref, cb_ref, wfo_ref, gfin_ref,
                       y_hbm, convout_hbm, carry_scr, y_buf, sem):
    n_hist = CONV_W - 1
    conv_in = [pltpu.make_async_copy(conv_hbm.at[:, k, :], carry_scr.at[pl.ds(k * n_batch, n_batch), :],
                                     sem.at[k]) for k in range(n_hist)]
    conv_out = [pltpu.make_async_copy(carry_scr.at[pl.ds(k * n_batch, n_batch), :], convout_hbm.at[:, k, :],
                                      sem.at[k]) for k in range(n_hist)]
    y_out = [pltpu.make_async_copy(y_buf.at[pl.ds(t * n_batch, n_batch), :], y_hbm.at[:, t, :],
                                   sem.at[n_hist + t]) for t in range(n_t)]
    for cp in conv_in:
        cp.start()
    for cp in conv_in:
        cp.wait()

    def get_h(s):
        return h_ref[s * SUB_ROWS:(s + 1) * SUB_ROWS, :]

    def put_y(s, y):
        y_buf[s * SUB_ROWS:(s + 1) * SUB_ROWS, :] = y

    _ffn_subblocks(get_h, put_y, n_batch, n_t * n_batch // SUB_ROWS, carry_scr,
                   gffn_ref, wfi_ref, cw_ref, cb_ref, wfo_ref, gfin_ref)
    for cp in conv_out + y_out:
        cp.start()
    for cp in conv_out + y_out:
        cp.wait()


def _const_spec(shape):
    zeros = (0,) * len(shape)
    return pl.BlockSpec(shape, lambda i: zeros, pipeline_mode=pl.Buffered(1))


def _params():
    return pltpu.CompilerParams(dimension_semantics=("arbitrary",), vmem_limit_bytes=VMEM_LIMIT_BYTES)


def _mixer_consts(h0, p, ws, bs):
    return [h0, p['g_mix'], p['w_in'], p['g_v'], ws, bs, p['ar'], p['ai'],
            p['bsub'], p['csub'], p['d_skip'], p['w_glu'], p['b_glu'],
            p['g_out_a'], p['g_out_b'], p['w_out']]


def _ffn_consts(conv0, p, wfi, wfo):
    return [conv0, p['g_ffn'], wfi, p['conv_w'], p['conv_b'], wfo, p['g_final']]


def _mixer_prompt_call(x, h0, p, ws, bs, w_ffn_in, w_ffn_out):
    n_batch, n_l, _ = x.shape
    n_steps = n_l // CHUNK
    blk = pl.BlockSpec((n_batch, CHUNK, D_MODEL), lambda i: (0, i, 0))
    wfi_blk = pl.BlockSpec((D_MODEL // n_steps, 2 * D_FF), lambda i: (i, 0))
    wfo_blk = pl.BlockSpec((D_FF // n_steps, D_MODEL), lambda i: (i, 0))
    consts = _mixer_consts(h0, p, ws, bs)
    return pl.pallas_call(
        functools.partial(_mixer_prompt_kernel, n_batch, n_steps),
        grid=(n_steps,),
        in_specs=[blk] + [_const_spec(c.shape) for c in consts] + [wfi_blk, wfo_blk],
        out_specs=[blk, pl.BlockSpec((n_batch, STATE_W), lambda i: (0, 0)), wfi_blk, wfo_blk],
        out_shape=[jax.ShapeDtypeStruct(x.shape, F32),
                   jax.ShapeDtypeStruct((n_batch, STATE_W), F32),
                   jax.ShapeDtypeStruct(w_ffn_in.shape, BF16),
                   jax.ShapeDtypeStruct(w_ffn_out.shape, BF16)],
        scratch_shapes=[pltpu.VMEM((n_batch, STATE_W), F32)],
        compiler_params=_params(),
        name="mixer_prompt",
    )(x, *consts, w_ffn_in, w_ffn_out)


def _ffn_prompt_call(h, conv0, p, wfi, wfo):
    n_batch, n_l, _ = h.shape
    n_t = FFN_PROMPT_STEPS
    n_steps = n_l // n_t
    halo = (CONV_W - 1) * n_batch
    blk = pl.BlockSpec((n_batch, n_t, D_MODEL), lambda i: (0, i, 0))
    consts = _ffn_consts(conv0, p, wfi, wfo)
    return pl.pallas_call(
        functools.partial(_ffn_prompt_kernel, n_batch, n_t, n_steps),
        grid=(n_steps,),
        in_specs=[blk] + [_const_spec(c.shape) for c in consts],
        out_specs=[blk, pl.BlockSpec((halo, 2 * D_FF), lambda i: (0, 0))],
        out_shape=[jax.ShapeDtypeStruct(h.shape, F32),
                   jax.ShapeDtypeStruct((halo, 2 * D_FF), F32)],
        scratch_shapes=[pltpu.VMEM((halo, 2 * D_FF), F32)],
        compiler_params=_params(),
        name="ffn_prompt",
    )(h, *consts)


def _mixer_sample_call(x, h0, p, ws, bs):
    n_batch, n_t, _ = x.shape
    rows = n_batch * n_t
    consts = _mixer_consts(h0, p, ws, bs)
    return pl.pallas_call(
        functools.partial(_mixer_sample_kernel, n_batch, n_t),
        grid=(1,),
        in_specs=[pl.BlockSpec(memory_space=pl.ANY)] + [_const_spec(c.shape) for c in consts],
        out_specs=[pl.BlockSpec((rows, D_MODEL), lambda i: (0, 0)),
                   pl.BlockSpec((n_batch, STATE_W), lambda i: (0, 0)),
                   pl.BlockSpec(memory_space=pl.ANY)],
        out_shape=[jax.ShapeDtypeStruct((rows, D_MODEL), F32),
                   jax.ShapeDtypeStruct((n_batch, STATE_W), F32),
                   jax.ShapeDtypeStruct((n_batch, n_t, A_HEADS, A_HEAD_DIM), F32)],
        scratch_shapes=[pltpu.VMEM((rows, D_MODEL), F32),
                        pltpu.VMEM((rows, A_WIDTH), F32),
                        pltpu.VMEM((rows, 2 * HALF_ST), F32),
                        pltpu.VMEM((n_batch, STATE_W), F32),
                        pltpu.SemaphoreType.DMA((n_t + n_t * A_HEADS,))],
        compiler_params=_params(),
        name="mixer_sample",
    )(x, *consts)


def _ffn_sample_call(h_tm, conv_buf, p, wfi, wfo):
    n_batch, n_hist, _ = conv_buf.shape
    rows = h_tm.shape[0]
    n_t = rows // n_batch
    consts = _ffn_consts(conv_buf, p, wfi, wfo)[1:]
    return pl.pallas_call(
        functools.partial(_ffn_sample_kernel, n_batch, n_t),
        grid=(1,),
        in_specs=[_const_spec(h_tm.shape), pl.BlockSpec(memory_space=pl.ANY)]
        + [_const_spec(c.shape) for c in consts],
        out_specs=[pl.BlockSpec(memory_space=pl.ANY), pl.BlockSpec(memory_space=pl.ANY)],
        out_shape=[jax.ShapeDtypeStruct((n_batch, n_t, D_MODEL), F32),
                   jax.ShapeDtypeStruct(conv_buf.shape, F32)],
        scratch_shapes=[pltpu.VMEM((n_hist * n_batch, 2 * D_FF), F32),
                        pltpu.VMEM((rows, D_MODEL), F32),
                        pltpu.SemaphoreType.DMA((n_hist + n_t,))],
        compiler_params=_params(),
        name="ffn_sample",
    )(h_tm, conv_buf, *consts)


def _prep_params(g_mix, w_in, g_v, lam_re, lam_im, log_dt, b_re, b_im, c_re, c_im,
                 d_skip, w_glu, b_glu, g_out_a, g_out_b, w_out, g_ffn, conv_w, conv_b, g_final):
    lr = lam_re.astype(F32)
    li = lam_im.astype(F32)
    dt = jnp.exp(log_dt.astype(F32))[:, None]
    mag = jnp.exp(lr * dt)
    ar = mag * jnp.cos(li * dt)
    ai = mag * jnp.sin(li * dt)
    den = lr * lr + li * li
    fr = ((ar - 1.0) * lr + ai * li) / den
    fi = (ai * lr - (ar - 1.0) * li) / den
    bre = b_re.astype(F32)
    bim = b_im.astype(F32)
    bbr = fr[..., None] * bre - fi[..., None] * bim
    bbi = fr[..., None] * bim + fi[..., None] * bre

    def blockdiag(m, rows_per_group, cols_per_group):
        tiled = jnp.tile(m.reshape(HALF_GROUPS * rows_per_group, cols_per_group), (1, HALF_GROUPS))
        row_g = lax.broadcasted_iota(jnp.int32, tiled.shape, 0) // rows_per_group
        col_g = lax.broadcasted_iota(jnp.int32, tiled.shape, 1) // cols_per_group
        return jnp.where(row_g == col_g, tiled, 0.0)

    def blockdiag_in(m):
        return blockdiag(jnp.transpose(m, (0, 2, 1)), SSM_GROUP, SSM_STATE)

    def blockdiag_out(m):
        return blockdiag(jnp.transpose(m, (0, 2, 1)), SSM_STATE, SSM_GROUP)

    bsub, csub, ars, ais = [], [], [], []
    for k in range(SSM_HALVES):
        gs = slice(k * HALF_GROUPS, (k + 1) * HALF_GROUPS)
        bsub.append(jnp.concatenate([blockdiag_in(bbr[gs]), blockdiag_in(bbi[gs])], axis=1))
        csub.append(jnp.concatenate([blockdiag_out(c_re[gs].astype(F32)),
                                     -blockdiag_out(c_im[gs].astype(F32))], axis=0))
        ars.append(ar[gs].reshape(1, HALF_ST))
        ais.append(ai[gs].reshape(1, HALF_ST))

    return dict(
        g_mix=g_mix.reshape(1, D_MODEL), w_in=w_in.astype(BF16), g_v=g_v,
        ar=jnp.concatenate(ars, axis=0), ai=jnp.concatenate(ais, axis=0),
        bsub=jnp.stack(bsub).astype(BF16), csub=jnp.stack(csub).astype(BF16),
        d_skip=d_skip.reshape(1, SSM_WIDTH), w_glu=w_glu.astype(BF16), b_glu=b_glu.reshape(1, SSM_WIDTH),
        g_out_a=g_out_a.reshape(1, A_WIDTH), g_out_b=g_out_b.reshape(1, SSM_WIDTH),
        w_out=w_out.astype(BF16), g_ffn=g_ffn.reshape(1, D_MODEL),
        conv_w=conv_w, conv_b=conv_b.reshape(1, 2 * D_FF), g_final=g_final.reshape(1, D_MODEL))


def _state_in(s_re, s_im):
    nb = s_re.shape[0]
    return jnp.concatenate([s_re.reshape(nb, ALL_ST), s_im.reshape(nb, ALL_ST)], axis=1)


def _state_out(st):
    nb = st.shape[0]
    return (st[:, :ALL_ST].reshape(nb, SSM_GROUPS, SSM_STATE),
            st[:, ALL_ST:].reshape(nb, SSM_GROUPS, SSM_STATE))


def _conv_state_in(conv_buf):
    nb = conv_buf.shape[0]
    return jnp.transpose(conv_buf, (1, 0, 2)).reshape((CONV_W - 1) * nb, 2 * D_FF)


def _conv_state_out(rows_tm, nb):
    return jnp.transpose(rows_tm.reshape(CONV_W - 1, nb, 2 * D_FF), (1, 0, 2))


def kernel(x_prompt, x_sample, state_ssm_re, state_ssm_im, state_conv, g_mix, w_in, g_v, w_s, b_s,
           lam_re, lam_im, log_dt, b_re, b_im, c_re, c_im, d_skip, w_glu, b_glu, g_out_a, g_out_b,
           w_out, g_ffn, w_ffn_in, conv_w, conv_b, w_ffn_out, g_final):
    p = _prep_params(g_mix, w_in, g_v, lam_re, lam_im, log_dt, b_re, b_im, c_re, c_im, d_skip,
                     w_glu, b_glu, g_out_a, g_out_b, w_out, g_ffn, conv_w, conv_b, g_final)
    tril = jnp.tril(w_s.astype(F32))

    nbp = x_prompt.shape[0]
    zero_state = jnp.zeros((nbp, SSM_GROUPS, SSM_STATE), state_ssm_re.dtype)
    zero_conv = jnp.zeros((nbp, CONV_W - 1, 2 * D_FF), state_conv.dtype)
    bs_p = jnp.broadcast_to(jnp.transpose(b_s)[:, :, None], (CHUNK, A_HEADS, A_HEAD_DIM)).reshape(CHUNK, A_WIDTH)
    h_p, st_p, wfi, wfo = _mixer_prompt_call(x_prompt, _state_in(zero_state, zero_state), p,
                                             tril.astype(BF16), bs_p, w_ffn_in, w_ffn_out)
    y_p, conv_p = _ffn_prompt_call(h_p, _conv_state_in(zero_conv), p, wfi, wfo)
    re_p, im_p = _state_out(st_p)
    conv_p = _conv_state_out(conv_p, nbp)

    nls = x_sample.shape[1]
    ws_s = jnp.broadcast_to(tril[:, :nls, :nls].reshape(A_HEADS * nls * nls, 1), (A_HEADS * nls * nls, A_HEAD_DIM))
    bs_s = jnp.broadcast_to(b_s[:, :nls].reshape(A_HEADS * nls, 1), (A_HEADS * nls, A_HEAD_DIM))
    h_s, st_s, v_s = _mixer_sample_call(x_sample, _state_in(state_ssm_re, state_ssm_im), p, ws_s, bs_s)
    y_s, conv_s = _ffn_sample_call(h_s, state_conv, p, wfi, wfo)
    re_s, im_s = _state_out(st_s)
    return (y_p, y_s, v_s, re_p, im_p, conv_p, re_s, im_s, conv_s)
```

```python
import functools
import math

import jax
import jax.numpy as jnp
from jax import lax
from jax.experimental import pallas as pl
from jax.experimental.pallas import tpu as pltpu

D_MODEL = 1024
A_HEADS = 4
A_HEAD_DIM = 128
A_WIDTH = 512
CHUNK = 128
SSM_WIDTH = 512
SSM_GROUP = 16
SSM_GROUPS = 32
SSM_STATE = 64
D_FF = 2816
CONV_W = 3
EPS = 1e-6

SSM_HALVES = 2
HALF_GROUPS = SSM_GROUPS // SSM_HALVES
HALF_CH = HALF_GROUPS * SSM_GROUP
HALF_ST = HALF_GROUPS * SSM_STATE
ALL_ST = SSM_HALVES * HALF_ST
STATE_W = 2 * ALL_ST

FF_CHUNK = 256
FF_NCHUNK = D_FF // FF_CHUNK
SUB_ROWS = 256
FFN_PROMPT_STEPS = 64

SUBLANES = 8
SCAN_LANES = 512
VMEM_LIMIT_BYTES = 60 * 1024 * 1024

BF16 = jnp.bfloat16
F32 = jnp.float32


def _state_lanes(k):
    return k * HALF_ST, ALL_ST + k * HALF_ST


def _rms(x, g):
    ms = jnp.mean(x * x, axis=-1, keepdims=True)
    return x * lax.rsqrt(ms + EPS) * g


def _gelu(x):
    c = math.sqrt(2.0 / math.pi)
    return x * (0.5 * (1.0 + jnp.tanh(c * (x + 0.044715 * (x * x * x)))))


def _dot(a, b):
    return jnp.dot(a, b, preferred_element_type=F32)


def _to_time_major(x, n_batch, n_t):
    w = x.shape[-1]
    return jnp.swapaxes(x.reshape(n_batch, n_t, w), 0, 1).reshape(n_t * n_batch, w)


def _to_batch_major(x, n_batch, n_t):
    w = x.shape[-1]
    return jnp.swapaxes(x.reshape(n_t, n_batch, w), 0, 1)


def _glu_norm(ys, xs, dskip_ref, wglu_ref, bglu_ref, gob_ref):
    g = _gelu(ys + dskip_ref[...] * xs)
    z = _dot(g.astype(BF16), wglu_ref[...]) + bglu_ref[...]
    return _rms(g * jax.nn.sigmoid(z), gob_ref[...])


def _ffn_subblocks(get_h, put_y, n_batch, n_sub, carry_scr,
                   gffn_ref, wfi_ref, cw_ref, cb_ref, wfo_ref, gfin_ref):
    halo = (CONV_W - 1) * n_batch
    hs, n2s = {}, {}

    def norm_in(s):
        hs[s] = get_h(s)
        n2s[s] = _rms(hs[s], gffn_ref[...]).astype(BF16)

    def col_slices(j):
        return [slice(c0, c0 + FF_CHUNK) for c0 in (j * FF_CHUNK, D_FF + j * FF_CHUNK)]

    def up_proj(j, s):
        return [jnp.dot(n2s[s], wfi_ref[:, cs], preferred_element_type=F32) for cs in col_slices(j)]

    order = [(j, s) for j in range(FF_NCHUNK) for s in range(n_sub)]
    accs = [jnp.zeros((SUB_ROWS, D_MODEL), F32) for _ in range(n_sub)]
    tails = [None, None]
    norm_in(0)
    ups_next = up_proj(*order[0])
    for s in range(1, n_sub):
        norm_in(s)
    for idx, (j, s) in enumerate(order):
        ups = ups_next
        if idx + 1 < len(order):
            ups_next = up_proj(*order[idx + 1])
        conv = []
        for ci, (cs, up) in enumerate(zip(col_slices(j), ups)):
            tail = carry_scr[:, cs] if s == 0 else tails[ci]
            padded = jnp.concatenate([tail, up], axis=0)
            tails[ci] = padded[SUB_ROWS:SUB_ROWS + halo, :]
            if s == n_sub - 1:
                carry_scr[:, cs] = tails[ci]
            out = cb_ref[:, cs]
            for tap in range(CONV_W):
                out = out + cw_ref[tap:tap + 1, cs] * padded[tap * n_batch:tap * n_batch + SUB_ROWS, :]
            conv.append(out)
        act = _gelu(conv[0]) * conv[1]
        accs[s] = accs[s] + jnp.dot(act.astype(BF16), wfo_ref[j * FF_CHUNK:(j + 1) * FF_CHUNK, :],
                                    preferred_element_type=F32)
        if j == FF_NCHUNK - 1:
            put_y(s, _rms(hs[s] + accs[s], gfin_ref[...]))


def _mixer_prompt_kernel(n_batch, n_steps,
                         x_ref, h0_ref, gmix_ref, win_ref, gv_ref, ws_ref, bs_ref,
                         ar_ref, ai_ref, bsub_ref, csub_ref, dskip_ref,
                         wglu_ref, bglu_ref, goa_ref, gob_ref, wout_ref, wfi32_ref, wfo32_ref,
                         hout_hbm, st_ref, wfi16_ref, wfo16_ref, hst_ref, h_buf, sem):
    sub_t = SUB_ROWS // n_batch
    n_sub = CHUNK // sub_t
    i = pl.program_id(0)
    slot = lax.rem(i, 2)

    def h_copies(step, buf_slot):
        return [pltpu.make_async_copy(h_buf.at[buf_slot, b], hout_hbm.at[pl.ds(step * CHUNK, CHUNK), b, :],
                                      sem.at[buf_slot]) for b in range(n_batch)]

    @pl.when(i == 0)
    def _():
        hst_ref[...] = h0_ref[...]

    @pl.when(i >= 2)
    def _():
        for cp in h_copies(i - 2, slot):
            cp.wait()

    state, trans = [], []
    for k in range(SSM_HALVES):
        re0, im0 = _state_lanes(k)
        state.append((hst_ref[:, re0:re0 + HALF_ST], hst_ref[:, im0:im0 + HALF_ST]))
        trans.append((jnp.broadcast_to(ar_ref[k:k + 1, :], (n_batch, HALF_ST)),
                      jnp.broadcast_to(ai_ref[k:k + 1, :], (n_batch, HALF_ST))))

    x_sub, proj_sub, xs_sub, y_sub, bn_sub = {}, {}, {}, {}, {}
    a_heads = [[] for _ in range(n_sub)]

    def project(s):
        x = x_ref[:, s * sub_t:(s + 1) * sub_t, :].reshape(SUB_ROWS, D_MODEL)
        n1 = _rms(x, gmix_ref[...]).astype(BF16)
        x_sub[s] = x
        proj_sub[s] = _dot(n1, win_ref[...])
        xs_sub[s] = _to_time_major(proj_sub[s][:, 2 * A_WIDTH:], n_batch, sub_t)
        y_sub[s] = []

    def ssm_half(s, k):
        u = _dot(xs_sub[s].astype(BF16)[:, k * HALF_CH:(k + 1) * HALF_CH], bsub_ref[k])
        ar, ai = trans[k]
        hr, hi = state[k]
        slabs = []
        for t in range(sub_t):
            ur = u[t * n_batch:(t + 1) * n_batch, :HALF_ST]
            ui = u[t * n_batch:(t + 1) * n_batch, HALF_ST:]
            hr, hi = ar * hr - ai * hi + ur, ar * hi + ai * hr + ui
            slabs.append(jnp.concatenate([hr, hi], axis=-1))
        state[k] = (hr, hi)
        y_sub[s].append(_dot(jnp.concatenate(slabs, axis=0).astype(BF16), csub_ref[k]))

    def glu(s):
        bn_tm = _glu_norm(jnp.concatenate(y_sub[s], axis=-1), xs_sub[s], dskip_ref, wglu_ref, bglu_ref,
                          gob_ref)
        bn_sub[s] = _to_batch_major(bn_tm, n_batch, sub_t).reshape(SUB_ROWS, SSM_WIDTH)

    def gate_head(h):
        hs = slice(h * A_HEAD_DIM, (h + 1) * A_HEAD_DIM)
        vn = [_rms(proj_sub[s][:, A_WIDTH + h * A_HEAD_DIM:A_WIDTH + (h + 1) * A_HEAD_DIM],
                   gv_ref[h:h + 1, :]).astype(BF16) for s in range(n_sub)]
        v_wide = jnp.concatenate(
            [jnp.concatenate([vn[s][b * sub_t:(b + 1) * sub_t, :] for s in range(n_sub)], axis=0)
             for b in range(n_batch)], axis=-1)
        s_wide = _dot(ws_ref[h], v_wide)
        for s in range(n_sub):
            ts = slice(s * sub_t, (s + 1) * sub_t)
            bias = bs_ref[ts, hs]
            gate = jnp.concatenate([s_wide[ts, b * A_HEAD_DIM:(b + 1) * A_HEAD_DIM] + bias
                                    for b in range(n_batch)], axis=0)
            a_heads[s].append(proj_sub[s][:, hs] * gate)

    def out_proj(s):
        an = _rms(jnp.concatenate(a_heads[s], axis=-1), goa_ref[...])
        mix = jnp.concatenate([an, bn_sub[s]], axis=-1).astype(BF16)
        h_buf[slot, :, s * sub_t:(s + 1) * sub_t, :] = (
            x_sub[s] + _dot(mix, wout_ref[...])).reshape(n_batch, sub_t, D_MODEL)

    chain = [project] + [functools.partial(ssm_half, k=k) for k in range(SSM_HALVES)] + [glu]
    heads = list(range(A_HEADS))
    outs = list(range(n_sub))
    for wave in range(n_sub + len(chain) - 1):
        if wave >= n_sub:
            for h in heads[:2]:
                gate_head(h)
            heads = heads[2:]
        for s in range(n_sub):
            if 0 <= wave - s < len(chain):
                chain[wave - s](s)
        if not heads and outs and wave - outs[0] >= len(chain) - 1:
            out_proj(outs.pop(0))
    for h in heads:
        gate_head(h)
    for s in outs:
        out_proj(s)

    for k in range(SSM_HALVES):
        re0, im0 = _state_lanes(k)
        hst_ref[:, re0:re0 + HALF_ST] = state[k][0]
        hst_ref[:, im0:im0 + HALF_ST] = state[k][1]

    wfi16_ref[...] = wfi32_ref[...].astype(BF16)
    wfo16_ref[...] = wfo32_ref[...].astype(BF16)
    for cp in h_copies(i, slot):
        cp.start()

    @pl.when(i == n_steps - 1)
    def _():
        st_ref[...] = hst_ref[...]
        for cp in h_copies(i - 1, 1 - slot) + h_copies(i, slot):
            cp.wait()


def _ffn_prompt_kernel(n_batch, n_t, n_steps,
                       h_ref, conv0_ref, gffn_ref, wfi_ref, cw_ref, cb_ref, wfo_ref, gfin_ref,
                       y_hbm, convout_ref, carry_scr, y_buf, sem):
    sub_t = SUB_ROWS // n_batch
    i = pl.program_id(0)
    slot = lax.rem(i, 2)

    def y_copies(step, buf_slot):
        return [pltpu.make_async_copy(y_buf.at[buf_slot, :, b, :], y_hbm.at[b, pl.ds(step * n_t, n_t), :],
                                      sem.at[buf_slot]) for b in range(n_batch)]

    @pl.when(i == 0)
    def _():
        carry_scr[...] = conv0_ref[...]

    @pl.when(i >= 2)
    def _():
        for cp in y_copies(i - 2, slot):
            cp.wait()

    def get_h(s):
        return h_ref[s * sub_t:(s + 1) * sub_t].reshape(SUB_ROWS, D_MODEL)

    def put_y(s, y):
        y_buf[slot, s * sub_t:(s + 1) * sub_t] = y.reshape(sub_t, n_batch, D_MODEL)

    _ffn_subblocks(get_h, put_y, n_batch, n_t // sub_t, carry_scr,
                   gffn_ref, wfi_ref, cw_ref, cb_ref, wfo_ref, gfin_ref)
    for cp in y_copies(i, slot):
        cp.start()

    @pl.when(i == n_steps - 1)
    def _():
        convout_ref[...] = carry_scr[...]
        for cp in y_copies(i - 1, 1 - slot) + y_copies(i, slot):
            cp.wait()


def _scan_inplace(u_scr, hst_ref, ar_ref, ai_ref, k, n_batch, n_t):
    re0, im0 = _state_lanes(k)

    def row_chunk(rc, _):
        r0 = pl.multiple_of(rc * SUBLANES, SUBLANES)
        for c in range(HALF_ST // SCAN_LANES):
            lo = c * SCAN_LANES
            re_l = slice(lo, lo + SCAN_LANES)
            im_l = slice(HALF_ST + lo, HALF_ST + lo + SCAN_LANES)
            ar = jnp.broadcast_to(ar_ref[k:k + 1, re_l], (SUBLANES, SCAN_LANES))
            ai = jnp.broadcast_to(ai_ref[k:k + 1, re_l], (SUBLANES, SCAN_LANES))
            hr0 = hst_ref[pl.ds(r0, SUBLANES), re0 + lo:re0 + lo + SCAN_LANES]
            hi0 = hst_ref[pl.ds(r0, SUBLANES), im0 + lo:im0 + lo + SCAN_LANES]

            def step(t, carry):
                hr, hi = carry
                row = pl.multiple_of(t * n_batch + r0, SUBLANES)
                ur = u_scr[pl.ds(row, SUBLANES), re_l]
                ui = u_scr[pl.ds(row, SUBLANES), im_l]
                nhr = ar * hr - ai * hi + ur
                nhi = ar * hi + ai * hr + ui
                u_scr[pl.ds(row, SUBLANES), re_l] = nhr
                u_scr[pl.ds(row, SUBLANES), im_l] = nhi
                return nhr, nhi

            hr, hi = lax.fori_loop(0, n_t, step, (hr0, hi0), unroll=min(n_t, 4))
            hst_ref[pl.ds(r0, SUBLANES), re0 + lo:re0 + lo + SCAN_LANES] = hr
            hst_ref[pl.ds(r0, SUBLANES), im0 + lo:im0 + lo + SCAN_LANES] = hi
        return 0

    lax.fori_loop(0, n_batch // SUBLANES, row_chunk, 0)


def _mixer_sample_kernel(n_batch, n_t,
                         x_ref, h0_ref, gmix_ref, win_ref, gv_ref, ws_ref, bs_ref,
                         ar_ref, ai_ref, bsub_ref, csub_ref, dskip_ref,
                         wglu_ref, bglu_ref, goa_ref, gob_ref, wout_ref,
                         hout_ref, st_ref, v_hbm, x_buf, v_buf, u_scr, hst_ref, sem):
    x_in = [pltpu.make_async_copy(x_ref.at[:, t, :], x_buf.at[pl.ds(t * n_batch, n_batch), :], sem.at[t])
            for t in range(n_t)]
    v_out = [pltpu.make_async_copy(
        v_buf.at[pl.ds(t * n_batch, n_batch), pl.ds(h * A_HEAD_DIM, A_HEAD_DIM)],
        v_hbm.at[:, t, h, :], sem.at[n_t + t * A_HEADS + h]) for t in range(n_t) for h in range(A_HEADS)]
    for cp in x_in:
        cp.start()
    hst_ref[...] = h0_ref[...]
    for cp in x_in:
        cp.wait()
    x = x_buf[...]
    n1 = _rms(x, gmix_ref[...]).astype(BF16)
    proj = _dot(n1, win_ref[...])

    a_heads = []
    for h in range(A_HEADS):
        hs = slice(h * A_HEAD_DIM, (h + 1) * A_HEAD_DIM)
        vn = _rms(proj[:, A_WIDTH + h * A_HEAD_DIM:A_WIDTH + (h + 1) * A_HEAD_DIM], gv_ref[h:h + 1, :])
        v_buf[:, hs] = vn
        slabs = []
        for t in range(n_t):
            s = jnp.broadcast_to(bs_ref[h * n_t + t:h * n_t + t + 1, :], (n_batch, A_HEAD_DIM))
            for src in range(t + 1):
                r = (h * n_t + t) * n_t + src
                s = s + ws_ref[r:r + 1, :] * vn[src * n_batch:(src + 1) * n_batch, :]
            slabs.append(proj[t * n_batch:(t + 1) * n_batch, hs] * s)
        a_heads.append(jnp.concatenate(slabs, axis=0))
    an = _rms(jnp.concatenate(a_heads, axis=-1), goa_ref[...])
    for cp in v_out:
        cp.start()

    xs = proj[:, 2 * A_WIDTH:]
    xs_b = xs.astype(BF16)
    y_parts = []
    for k in range(SSM_HALVES):
        u_scr[...] = _dot(xs_b[:, k * HALF_CH:(k + 1) * HALF_CH], bsub_ref[k])
        _scan_inplace(u_scr, hst_ref, ar_ref, ai_ref, k, n_batch, n_t)
        y_parts.append(_dot(u_scr[...].astype(BF16), csub_ref[k]))
    bn = _glu_norm(jnp.concatenate(y_parts, axis=-1), xs, dskip_ref, wglu_ref, bglu_ref, gob_ref)

    mix = jnp.concatenate([an, bn], axis=-1).astype(BF16)
    hout_ref[...] = x + _dot(mix, wout_ref[...])
    st_ref[...] = hst_ref[...]
    for cp in v_out:
        cp.wait()


def _ffn_sample_kernel(n_batch, n_t,
                       h_ref, conv_hbm, gffn_ref, wfi_ref, cw_ref, cb_ref, wfo_ref, gfin_ref,
                       y_hbm, convout_hbm, carry_scr, y_buf, sem):
    n_hist = CONV_W - 1
    conv_in = [pltpu.make_async_copy(conv_hbm.at[:, k, :], carry_scr.at[pl.ds(k * n_batch, n_batch), :],
                                     sem.at[k]) for k in range(n_hist)]
    conv_out = [pltpu.make_async_copy(carry_scr.at[pl.ds(k * n_batch, n_batch), :], convout_hbm.at[:, k, :],
                                      sem.at[k]) for k in range(n_hist)]
    y_out = [pltpu.make_async_copy(y_buf.at[pl.ds(t * n_batch, n_batch), :], y_hbm.at[:, t, :],
                                   sem.at[n_hist + t]) for t in range(n_t)]
    for cp in conv_in:
        cp.start()
    for cp in conv_in:
        cp.wait()

    def get_h(s):
        return h_ref[s * SUB_ROWS:(s + 1) * SUB_ROWS, :]

    def put_y(s, y):
        y_buf[s * SUB_ROWS:(s + 1) * SUB_ROWS, :] = y

    _ffn_subblocks(get_h, put_y, n_batch, n_t * n_batch // SUB_ROWS, carry_scr,
                   gffn_ref, wfi_ref, cw_ref, cb_ref, wfo_ref, gfin_ref)
    for cp in conv_out + y_out:
        cp.start()
    for cp in conv_out + y_out:
        cp.wait()


def _const_spec(shape):
    zeros = (0,) * len(shape)
    return pl.BlockSpec(shape, lambda i: zeros, pipeline_mode=pl.Buffered(1))


def _params():
    return pltpu.CompilerParams(dimension_semantics=("arbitrary",), vmem_limit_bytes=VMEM_LIMIT_BYTES)


def _mixer_consts(h0, p, ws, bs):
    return [h0, p['g_mix'], p['w_in'], p['g_v'], ws, bs, p['ar'], p['ai'],
            p['bsub'], p['csub'], p['d_skip'], p['w_glu'], p['b_glu'],
            p['g_out_a'], p['g_out_b'], p['w_out']]


def _ffn_consts(conv0, p, wfi, wfo):
    return [conv0, p['g_ffn'], wfi, p['conv_w'], p['conv_b'], wfo, p['g_final']]


def _mixer_prompt_call(x, h0, p, ws, bs, w_ffn_in, w_ffn_out):
    n_batch, n_l, _ = x.shape
    n_steps = n_l // CHUNK
    assert n_steps >= 2
    blk = pl.BlockSpec((n_batch, CHUNK, D_MODEL), lambda i: (0, i, 0))
    wfi_blk = pl.BlockSpec((D_MODEL // n_steps, 2 * D_FF), lambda i: (i, 0))
    wfo_blk = pl.BlockSpec((D_FF // n_steps, D_MODEL), lambda i: (i, 0))
    consts = _mixer_consts(h0, p, ws, bs)
    return pl.pallas_call(
        functools.partial(_mixer_prompt_kernel, n_batch, n_steps),
        grid=(n_steps,),
        in_specs=[blk] + [_const_spec(c.shape) for c in consts] + [wfi_blk, wfo_blk],
        out_specs=[pl.BlockSpec(memory_space=pl.ANY),
                   pl.BlockSpec((n_batch, STATE_W), lambda i: (0, 0)), wfi_blk, wfo_blk],
        out_shape=[jax.ShapeDtypeStruct((n_l, n_batch, D_MODEL), F32),
                   jax.ShapeDtypeStruct((n_batch, STATE_W), F32),
                   jax.ShapeDtypeStruct(w_ffn_in.shape, BF16),
                   jax.ShapeDtypeStruct(w_ffn_out.shape, BF16)],
        scratch_shapes=[pltpu.VMEM((n_batch, STATE_W), F32),
                        pltpu.VMEM((2, n_batch, CHUNK, D_MODEL), F32),
                        pltpu.SemaphoreType.DMA((2,))],
        compiler_params=_params(),
        name="mixer_prompt",
    )(x, *consts, w_ffn_in, w_ffn_out)


def _ffn_prompt_call(h_tm, conv0, p, wfi, wfo):
    n_l, n_batch, _ = h_tm.shape
    n_t = FFN_PROMPT_STEPS
    n_steps = n_l // n_t
    assert n_steps >= 2
    halo = (CONV_W - 1) * n_batch
    blk = (n_t, n_batch, D_MODEL)
    consts = _ffn_consts(conv0, p, wfi, wfo)
    return pl.pallas_call(
        functools.partial(_ffn_prompt_kernel, n_batch, n_t, n_steps),
        grid=(n_steps,),
        in_specs=[pl.BlockSpec(blk, lambda i: (i, 0, 0))] + [_const_spec(c.shape) for c in consts],
        out_specs=[pl.BlockSpec(memory_space=pl.ANY), pl.BlockSpec((halo, 2 * D_FF), lambda i: (0, 0))],
        out_shape=[jax.ShapeDtypeStruct((n_batch, n_l, D_MODEL), F32),
                   jax.ShapeDtypeStruct((halo, 2 * D_FF), F32)],
        scratch_shapes=[pltpu.VMEM((halo, 2 * D_FF), F32),
                        pltpu.VMEM((2,) + blk, F32),
                        pltpu.SemaphoreType.DMA((2,))],
        compiler_params=_params(),
        name="ffn_prompt",
    )(h_tm, *consts)


def _mixer_sample_call(x, h0, p, ws, bs):
    n_batch, n_t, _ = x.shape
    rows = n_batch * n_t
    consts = _mixer_consts(h0, p, ws, bs)
    return pl.pallas_call(
        functools.partial(_mixer_sample_kernel, n_batch, n_t),
        grid=(1,),
        in_specs=[pl.BlockSpec(memory_space=pl.ANY)] + [_const_spec(c.shape) for c in consts],
        out_specs=[pl.BlockSpec((rows, D_MODEL), lambda i: (0, 0)),
                   pl.BlockSpec((n_batch, STATE_W), lambda i: (0, 0)),
                   pl.BlockSpec(memory_space=pl.ANY)],
        out_shape=[jax.ShapeDtypeStruct((rows, D_MODEL), F32),
                   jax.ShapeDtypeStruct((n_batch, STATE_W), F32),
                   jax.ShapeDtypeStruct((n_batch, n_t, A_HEADS, A_HEAD_DIM), F32)],
        scratch_shapes=[pltpu.VMEM((rows, D_MODEL), F32),
                        pltpu.VMEM((rows, A_WIDTH), F32),
                        pltpu.VMEM((rows, 2 * HALF_ST), F32),
                        pltpu.VMEM((n_batch, STATE_W), F32),
                        pltpu.SemaphoreType.DMA((n_t + n_t * A_HEADS,))],
        compiler_params=_params(),
        name="mixer_sample",
    )(x, *consts)


def _ffn_sample_call(h_tm, conv_buf, p, wfi, wfo):
    n_batch, n_hist, _ = conv_buf.shape
    rows = h_tm.shape[0]
    n_t = rows // n_batch
    consts = _ffn_consts(conv_buf, p, wfi, wfo)[1:]
    return pl.pallas_call(
        functools.partial(_ffn_sample_kernel, n_batch, n_t),
        grid=(1,),
        in_specs=[_const_spec(h_tm.shape), pl.BlockSpec(memory_space=pl.ANY)]
        + [_const_spec(c.shape) for c in consts],
        out_specs=[pl.BlockSpec(memory_space=pl.ANY), pl.BlockSpec(memory_space=pl.ANY)],
        out_shape=[jax.ShapeDtypeStruct((n_batch, n_t, D_MODEL), F32),
                   jax.ShapeDtypeStruct(conv_buf.shape, F32)],
        scratch_shapes=[pltpu.VMEM((n_hist * n_batch, 2 * D_FF), F32),
                        pltpu.VMEM((rows, D_MODEL), F32),
                        pltpu.SemaphoreType.DMA((n_hist + n_t,))],
        compiler_params=_params(),
        name="ffn_sample",
    )(h_tm, conv_buf, *consts)


def _prep_params(g_mix, w_in, g_v, lam_re, lam_im, log_dt, b_re, b_im, c_re, c_im,
                 d_skip, w_glu, b_glu, g_out_a, g_out_b, w_out, g_ffn, conv_w, conv_b, g_final):
    lr = lam_re.astype(F32)
    li = lam_im.astype(F32)
    dt = jnp.exp(log_dt.astype(F32))[:, None]
    mag = jnp.exp(lr * dt)
    ar = mag * jnp.cos(li * dt)
    ai = mag * jnp.sin(li * dt)
    den = lr * lr + li * li
    fr = ((ar - 1.0) * lr + ai * li) / den
    fi = (ai * lr - (ar - 1.0) * li) / den
    bre = b_re.astype(F32)
    bim = b_im.astype(F32)
    bbr = fr[..., None] * bre - fi[..., None] * bim
    bbi = fr[..., None] * bim + fi[..., None] * bre

    def blockdiag(m, rows_per_group, cols_per_group):
        tiled = jnp.tile(m.reshape(HALF_GROUPS * rows_per_group, cols_per_group), (1, HALF_GROUPS))
        row_g = lax.broadcasted_iota(jnp.int32, tiled.shape, 0) // rows_per_group
        col_g = lax.broadcasted_iota(jnp.int32, tiled.shape, 1) // cols_per_group
        return jnp.where(row_g == col_g, tiled, 0.0)

    def blockdiag_in(m):
        return blockdiag(jnp.transpose(m, (0, 2, 1)), SSM_GROUP, SSM_STATE)

    def blockdiag_out(m):
        return blockdiag(jnp.transpose(m, (0, 2, 1)), SSM_STATE, SSM_GROUP)

    bsub, csub, ars, ais = [], [], [], []
    for k in range(SSM_HALVES):
        gs = slice(k * HALF_GROUPS, (k + 1) * HALF_GROUPS)
        bsub.append(jnp.concatenate([blockdiag_in(bbr[gs]), blockdiag_in(bbi[gs])], axis=1))
        csub.append(jnp.concatenate([blockdiag_out(c_re[gs].astype(F32)),
                                     -blockdiag_out(c_im[gs].astype(F32))], axis=0))
        ars.append(ar[gs].reshape(1, HALF_ST))
        ais.append(ai[gs].reshape(1, HALF_ST))

    return dict(
        g_mix=g_mix.reshape(1, D_MODEL), w_in=w_in.astype(BF16), g_v=g_v,
        ar=jnp.concatenate(ars, axis=0), ai=jnp.concatenate(ais, axis=0),
        bsub=jnp.stack(bsub).astype(BF16), csub=jnp.stack(csub).astype(BF16),
        d_skip=d_skip.reshape(1, SSM_WIDTH), w_glu=w_glu.astype(BF16), b_glu=b_glu.reshape(1, SSM_WIDTH),
        g_out_a=g_out_a.reshape(1, A_WIDTH), g_out_b=g_out_b.reshape(1, SSM_WIDTH),
        w_out=w_out.astype(BF16), g_ffn=g_ffn.reshape(1, D_MODEL),
        conv_w=conv_w, conv_b=conv_b.reshape(1, 2 * D_FF), g_final=g_final.reshape(1, D_MODEL))


def _state_in(s_re, s_im):
    nb = s_re.shape[0]
    return jnp.concatenate([s_re.reshape(nb, ALL_ST), s_im.reshape(nb, ALL_ST)], axis=1)


def _state_out(st):
    nb = st.shape[0]
    return (st[:, :ALL_ST].reshape(nb, SSM_GROUPS, SSM_STATE),
            st[:, ALL_ST:].reshape(nb, SSM_GROUPS, SSM_STATE))


def _conv_state_in(conv_buf):
    nb = conv_buf.shape[0]
    return jnp.transpose(conv_buf, (1, 0, 2)).reshape((CONV_W - 1) * nb, 2 * D_FF)


def _conv_state_out(rows_tm, nb):
    return jnp.transpose(rows_tm.reshape(CONV_W - 1, nb, 2 * D_FF), (1, 0, 2))


def kernel(x_prompt, x_sample, state_ssm_re, state_ssm_im, state_conv, g_mix, w_in, g_v, w_s, b_s,
           lam_re, lam_im, log_dt, b_re, b_im, c_re, c_im, d_skip, w_glu, b_glu, g_out_a, g_out_b,
           w_out, g_ffn, w_ffn_in, conv_w, conv_b, w_ffn_out, g_final):
    p = _prep_params(g_mix, w_in, g_v, lam_re, lam_im, log_dt, b_re, b_im, c_re, c_im, d_skip,
                     w_glu, b_glu, g_out_a, g_out_b, w_out, g_ffn, conv_w, conv_b, g_final)
    tril = jnp.tril(w_s.astype(F32))

    nbp = x_prompt.shape[0]
    zero_state = jnp.zeros((nbp, SSM_GROUPS, SSM_STATE), state_ssm_re.dtype)
    zero_conv = jnp.zeros((nbp, CONV_W - 1, 2 * D_FF), state_conv.dtype)
    bs_p = jnp.broadcast_to(jnp.transpose(b_s)[:, :, None], (CHUNK, A_HEADS, A_HEAD_DIM)).reshape(CHUNK, A_WIDTH)
    h_p, st_p, wfi, wfo = _mixer_prompt_call(x_prompt, _state_in(zero_state, zero_state), p,
                                             tril.astype(BF16), bs_p, w_ffn_in, w_ffn_out)
    y_p, conv_p = _ffn_prompt_call(h_p, _conv_state_in(zero_conv), p, wfi, wfo)
    re_p, im_p = _state_out(st_p)
    conv_p = _conv_state_out(conv_p, nbp)

    nls = x_sample.shape[1]
    ws_s = jnp.broadcast_to(tril[:, :nls, :nls].reshape(A_HEADS * nls * nls, 1), (A_HEADS * nls * nls, A_HEAD_DIM))
    bs_s = jnp.broadcast_to(b_s[:, :nls].reshape(A_HEADS * nls, 1), (A_HEADS * nls, A_HEAD_DIM))
    h_s, st_s, v_s = _mixer_sample_call(x_sample, _state_in(state_ssm_re, state_ssm_im), p, ws_s, bs_s)
    y_s, conv_s = _ffn_sample_call(h_s, state_conv, p, wfi, wfo)
    re_s, im_s = _state_out(st_s)
    return (y_p, y_s, v_s, re_p, im_p, conv_p, re_s, im_s, conv_s)
```

```python
import functools
import math

import jax
import jax.numpy as jnp
from jax import lax
from jax.experimental import pallas as pl
from jax.experimental.pallas import tpu as pltpu

D_MODEL = 1024
A_HEADS = 4
A_HEAD_DIM = 128
A_WIDTH = 512
CHUNK = 128
SSM_WIDTH = 512
SSM_GROUP = 16
SSM_GROUPS = 32
SSM_STATE = 64
D_FF = 2816
CONV_W = 3
EPS = 1e-6

SSM_HALVES = 2
HALF_GROUPS = SSM_GROUPS // SSM_HALVES
HALF_CH = HALF_GROUPS * SSM_GROUP
HALF_ST = HALF_GROUPS * SSM_STATE
ALL_ST = SSM_HALVES * HALF_ST
STATE_W = 2 * ALL_ST

FF_CHUNK = 256
FF_NCHUNK = D_FF // FF_CHUNK
SUB_ROWS = 256
FFN_PROMPT_STEPS = 64

SUBLANES = 8
SCAN_LANES = 512
VMEM_LIMIT_BYTES = 60 * 1024 * 1024

BF16 = jnp.bfloat16
F32 = jnp.float32


def _state_lanes(k):
    return k * HALF_ST, ALL_ST + k * HALF_ST


def _rms(x, g):
    ms = jnp.mean(x * x, axis=-1, keepdims=True)
    return x * lax.rsqrt(ms + EPS) * g


def _gelu(x):
    c = math.sqrt(2.0 / math.pi)
    return x * (0.5 * (1.0 + jnp.tanh(c * (x + 0.044715 * (x * x * x)))))


def _dot(a, b):
    return jnp.dot(a, b, preferred_element_type=F32)


def _to_time_major(x, n_batch, n_t):
    w = x.shape[-1]
    return jnp.swapaxes(x.reshape(n_batch, n_t, w), 0, 1).reshape(n_t * n_batch, w)


def _to_batch_major(x, n_batch, n_t):
    w = x.shape[-1]
    return jnp.swapaxes(x.reshape(n_t, n_batch, w), 0, 1)


def _glu_norm(ys, xs, dskip_ref, wglu_ref, bglu_ref, gob_ref):
    g = _gelu(ys + dskip_ref[...] * xs)
    z = _dot(g.astype(BF16), wglu_ref[...]) + bglu_ref[...]
    return _rms(g * jax.nn.sigmoid(z), gob_ref[...])


def _ffn_subblocks(get_h, put_y, n_batch, n_sub, carry_scr,
                   gffn_ref, wfi_ref, cw_ref, cb_ref, wfo_ref, gfin_ref):
    halo = (CONV_W - 1) * n_batch
    hs, n2s = {}, {}

    def norm_in(s):
        hs[s] = get_h(s)
        n2s[s] = _rms(hs[s], gffn_ref[...]).astype(BF16)

    def col_slices(j):
        return [slice(c0, c0 + FF_CHUNK) for c0 in (j * FF_CHUNK, D_FF + j * FF_CHUNK)]

    def up_proj(j, s):
        return [jnp.dot(n2s[s], wfi_ref[:, cs], preferred_element_type=F32) for cs in col_slices(j)]

    order = [(j, s) for j in range(FF_NCHUNK) for s in range(n_sub)]
    accs = [jnp.zeros((SUB_ROWS, D_MODEL), F32) for _ in range(n_sub)]
    tails = [None, None]
    norm_in(0)
    ups_next = up_proj(*order[0])
    for s in range(1, n_sub):
        norm_in(s)
    for idx, (j, s) in enumerate(order):
        ups = ups_next
        if idx + 1 < len(order):
            ups_next = up_proj(*order[idx + 1])
        conv = []
        for ci, (cs, up) in enumerate(zip(col_slices(j), ups)):
            tail = carry_scr[:, cs] if s == 0 else tails[ci]
            padded = jnp.concatenate([tail, up], axis=0)
            tails[ci] = padded[SUB_ROWS:SUB_ROWS + halo, :]
            if s == n_sub - 1:
                carry_scr[:, cs] = tails[ci]
            out = cb_ref[:, cs]
            for tap in range(CONV_W):
                out = out + cw_ref[tap:tap + 1, cs] * padded[tap * n_batch:tap * n_batch + SUB_ROWS, :]
            conv.append(out)
        act = _gelu(conv[0]) * conv[1]
        accs[s] = accs[s] + jnp.dot(act.astype(BF16), wfo_ref[j * FF_CHUNK:(j + 1) * FF_CHUNK, :],
                                    preferred_element_type=F32)
        if j == FF_NCHUNK - 1:
            put_y(s, _rms(hs[s] + accs[s], gfin_ref[...]))


def _mixer_prompt_kernel(n_batch, n_steps,
                         x_ref, h0_ref, gmix_ref, win_ref, gv_ref, ws_ref, bs_ref,
                         ar_ref, ai_ref, bsub_ref, csub_ref, dskip_ref,
                         wglu_ref, bglu_ref, goa_ref, gob_ref, wout_ref, wfi32_ref, wfo32_ref,
                         hout_hbm, st_ref, wfi16_ref, wfo16_ref, hst_ref, h_buf, sem):
    sub_t = SUB_ROWS // n_batch
    n_sub = CHUNK // sub_t
    i = pl.program_id(0)
    slot = lax.rem(i, 2)

    def h_copies(step, buf_slot):
        return [pltpu.make_async_copy(h_buf.at[buf_slot, b], hout_hbm.at[pl.ds(step * CHUNK, CHUNK), b, :],
                                      sem.at[buf_slot]) for b in range(n_batch)]

    @pl.when(i == 0)
    def _():
        hst_ref[...] = h0_ref[...]

    @pl.when(i >= 2)
    def _():
        for cp in h_copies(i - 2, slot):
            cp.wait()

    state, trans = [], []
    for k in range(SSM_HALVES):
        re0, im0 = _state_lanes(k)
        state.append((hst_ref[:, re0:re0 + HALF_ST], hst_ref[:, im0:im0 + HALF_ST]))
        trans.append((jnp.broadcast_to(ar_ref[k:k + 1, :], (n_batch, HALF_ST)),
                      jnp.broadcast_to(ai_ref[k:k + 1, :], (n_batch, HALF_ST))))

    x_sub, proj_sub, xs_sub, y_sub, bn_sub = {}, {}, {}, {}, {}
    a_heads = [[] for _ in range(n_sub)]

    def project(s):
        x = x_ref[:, s * sub_t:(s + 1) * sub_t, :].reshape(SUB_ROWS, D_MODEL)
        n1 = _rms(x, gmix_ref[...]).astype(BF16)
        x_sub[s] = x
        proj_sub[s] = _dot(n1, win_ref[...])
        xs_sub[s] = _to_time_major(proj_sub[s][:, 2 * A_WIDTH:], n_batch, sub_t)
        y_sub[s] = []

    def ssm_half(s, k):
        u = _dot(xs_sub[s].astype(BF16)[:, k * HALF_CH:(k + 1) * HALF_CH], bsub_ref[k])
        ar, ai = trans[k]
        hr, hi = state[k]
        slabs = []
        for t in range(sub_t):
            ur = u[t * n_batch:(t + 1) * n_batch, :HALF_ST]
            ui = u[t * n_batch:(t + 1) * n_batch, HALF_ST:]
            hr, hi = ar * hr - ai * hi + ur, ar * hi + ai * hr + ui
            slabs.append(jnp.concatenate([hr, hi], axis=-1))
        state[k] = (hr, hi)
        y_sub[s].append(_dot(jnp.concatenate(slabs, axis=0).astype(BF16), csub_ref[k]))

    def glu(s):
        bn_tm = _glu_norm(jnp.concatenate(y_sub[s], axis=-1), xs_sub[s], dskip_ref, wglu_ref, bglu_ref,
                          gob_ref)
        bn_sub[s] = _to_batch_major(bn_tm, n_batch, sub_t).reshape(SUB_ROWS, SSM_WIDTH)

    def gate_head(h):
        hs = slice(h * A_HEAD_DIM, (h + 1) * A_HEAD_DIM)
        vn = [_rms(proj_sub[s][:, A_WIDTH + h * A_HEAD_DIM:A_WIDTH + (h + 1) * A_HEAD_DIM],
                   gv_ref[h:h + 1, :]).astype(BF16) for s in range(n_sub)]
        v_wide = jnp.concatenate(
            [jnp.concatenate([vn[s][b * sub_t:(b + 1) * sub_t, :] for s in range(n_sub)], axis=0)
             for b in range(n_batch)], axis=-1)
        s_wide = _dot(ws_ref[h], v_wide)
        for s in range(n_sub):
            ts = slice(s * sub_t, (s + 1) * sub_t)
            bias = bs_ref[ts, hs]
            gate = jnp.concatenate([s_wide[ts, b * A_HEAD_DIM:(b + 1) * A_HEAD_DIM] + bias
                                    for b in range(n_batch)], axis=0)
            a_heads[s].append(proj_sub[s][:, hs] * gate)

    def out_proj(s):
        an = _rms(jnp.concatenate(a_heads[s], axis=-1), goa_ref[...])
        mix = jnp.concatenate([an, bn_sub[s]], axis=-1).astype(BF16)
        h_buf[slot, :, s * sub_t:(s + 1) * sub_t, :] = (
            x_sub[s] + _dot(mix, wout_ref[...])).reshape(n_batch, sub_t, D_MODEL)

    chain = [project] + [functools.partial(ssm_half, k=k) for k in range(SSM_HALVES)] + [glu]
    heads = list(range(A_HEADS))
    outs = list(range(n_sub))
    for wave in range(n_sub + len(chain) - 1):
        if wave >= n_sub:
            for h in heads[:2]:
                gate_head(h)
            heads = heads[2:]
        for s in range(n_sub):
            if 0 <= wave - s < len(chain):
                chain[wave - s](s)
        if not heads and outs and wave - outs[0] >= len(chain) - 1:
            out_proj(outs.pop(0))
    for h in heads:
        gate_head(h)
    for s in outs:
        out_proj(s)

    for k in range(SSM_HALVES):
        re0, im0 = _state_lanes(k)
        hst_ref[:, re0:re0 + HALF_ST] = state[k][0]
        hst_ref[:, im0:im0 + HALF_ST] = state[k][1]

    wfi16_ref[...] = wfi32_ref[...].astype(BF16)
    wfo16_ref[...] = wfo32_ref[...].astype(BF16)
    for cp in h_copies(i, slot):
        cp.start()

    @pl.when(i == n_steps - 1)
    def _():
        st_ref[...] = hst_ref[...]
        for cp in h_copies(i - 1, 1 - slot) + h_copies(i, slot):
            cp.wait()


def _ffn_prompt_kernel(n_batch, n_t, n_steps,
                       h_ref, conv0_ref, gffn_ref, wfi_ref, cw_ref, cb_ref, wfo_ref, gfin_ref,
                       y_hbm, convout_ref, carry_scr, y_buf, sem):
    sub_t = SUB_ROWS // n_batch
    i = pl.program_id(0)
    slot = lax.rem(i, 2)

    def y_copies(step, buf_slot):
        return [pltpu.make_async_copy(y_buf.at[buf_slot, :, b, :], y_hbm.at[b, pl.ds(step * n_t, n_t), :],
                                      sem.at[buf_slot]) for b in range(n_batch)]

    @pl.when(i == 0)
    def _():
        carry_scr[...] = conv0_ref[...]

    @pl.when(i >= 2)
    def _():
        for cp in y_copies(i - 2, slot):
            cp.wait()

    def get_h(s):
        return h_ref[s * sub_t:(s + 1) * sub_t].reshape(SUB_ROWS, D_MODEL)

    def put_y(s, y):
        y_buf[slot, s * sub_t:(s + 1) * sub_t] = y.reshape(sub_t, n_batch, D_MODEL)

    _ffn_subblocks(get_h, put_y, n_batch, n_t // sub_t, carry_scr,
                   gffn_ref, wfi_ref, cw_ref, cb_ref, wfo_ref, gfin_ref)
    for cp in y_copies(i, slot):
        cp.start()

    @pl.when(i == n_steps - 1)
    def _():
        convout_ref[...] = carry_scr[...]
        for cp in y_copies(i - 1, 1 - slot) + y_copies(i, slot):
            cp.wait()


def _scan_inplace(u_scr, hst_ref, ar_ref, ai_ref, k, n_batch, n_t):
    re0, im0 = _state_lanes(k)

    def row_chunk(rc, _):
        r0 = pl.multiple_of(rc * SUBLANES, SUBLANES)
        for c in range(HALF_ST // SCAN_LANES):
            lo = c * SCAN_LANES
            re_l = slice(lo, lo + SCAN_LANES)
            im_l = slice(HALF_ST + lo, HALF_ST + lo + SCAN_LANES)
            ar = jnp.broadcast_to(ar_ref[k:k + 1, re_l], (SUBLANES, SCAN_LANES))
            ai = jnp.broadcast_to(ai_ref[k:k + 1, re_l], (SUBLANES, SCAN_LANES))
            hr0 = hst_ref[pl.ds(r0, SUBLANES), re0 + lo:re0 + lo + SCAN_LANES]
            hi0 = hst_ref[pl.ds(r0, SUBLANES), im0 + lo:im0 + lo + SCAN_LANES]

            def step(t, carry):
                hr, hi = carry
                row = pl.multiple_of(t * n_batch + r0, SUBLANES)
                ur = u_scr[pl.ds(row, SUBLANES), re_l]
                ui = u_scr[pl.ds(row, SUBLANES), im_l]
                nhr = ar * hr - ai * hi + ur
                nhi = ar * hi + ai * hr + ui
                u_scr[pl.ds(row, SUBLANES), re_l] = nhr
                u_scr[pl.ds(row, SUBLANES), im_l] = nhi
                return nhr, nhi

            hr, hi = lax.fori_loop(0, n_t, step, (hr0, hi0), unroll=min(n_t, 4))
            hst_ref[pl.ds(r0, SUBLANES), re0 + lo:re0 + lo + SCAN_LANES] = hr
            hst_ref[pl.ds(r0, SUBLANES), im0 + lo:im0 + lo + SCAN_LANES] = hi
        return 0

    lax.fori_loop(0, n_batch // SUBLANES, row_chunk, 0)


def _mixer_sample_kernel(n_batch, n_t,
                         x_ref, h0_ref, gmix_ref, win_ref, gv_ref, ws_ref, bs_ref,
                         ar_ref, ai_ref, bsub_ref, csub_ref, dskip_ref,
                         wglu_ref, bglu_ref, goa_ref, gob_ref, wout_ref,
                         hout_ref, st_ref, v_hbm, x_buf, v_buf, u_scr, hst_ref, sem):
    x_in = [pltpu.make_async_copy(x_ref.at[:, t, :], x_buf.at[pl.ds(t * n_batch, n_batch), :], sem.at[t])
            for t in range(n_t)]
    v_out = [pltpu.make_async_copy(
        v_buf.at[pl.ds(t * n_batch, n_batch), pl.ds(h * A_HEAD_DIM, A_HEAD_DIM)],
        v_hbm.at[:, t, h, :], sem.at[n_t + t * A_HEADS + h]) for t in range(n_t) for h in range(A_HEADS)]
    for cp in x_in:
        cp.start()
    hst_ref[...] = h0_ref[...]
    for cp in x_in:
        cp.wait()
    x = x_buf[...]
    n1 = _rms(x, gmix_ref[...]).astype(BF16)
    proj = _dot(n1, win_ref[...])

    a_heads = []
    for h in range(A_HEADS):
        hs = slice(h * A_HEAD_DIM, (h + 1) * A_HEAD_DIM)
        vn = _rms(proj[:, A_WIDTH + h * A_HEAD_DIM:A_WIDTH + (h + 1) * A_HEAD_DIM], gv_ref[h:h + 1, :])
        v_buf[:, hs] = vn
        slabs = []
        for t in range(n_t):
            s = jnp.broadcast_to(bs_ref[h * n_t + t:h * n_t + t + 1, :], (n_batch, A_HEAD_DIM))
            for src in range(t + 1):
                r = (h * n_t + t) * n_t + src
                s = s + ws_ref[r:r + 1, :] * vn[src * n_batch:(src + 1) * n_batch, :]
            slabs.append(proj[t * n_batch:(t + 1) * n_batch, hs] * s)
        a_heads.append(jnp.concatenate(slabs, axis=0))
    an = _rms(jnp.concatenate(a_heads, axis=-1), goa_ref[...])
    for cp in v_out:
        cp.start()

    xs = proj[:, 2 * A_WIDTH:]
    xs_b = xs.astype(BF16)
    y_parts = []
    for k in range(SSM_HALVES):
        u_scr[...] = _dot(xs_b[:, k * HALF_CH:(k + 1) * HALF_CH], bsub_ref[k])
        _scan_inplace(u_scr, hst_ref, ar_ref, ai_ref, k, n_batch, n_t)
        y_parts.append(_dot(u_scr[...].astype(BF16), csub_ref[k]))
    bn = _glu_norm(jnp.concatenate(y_parts, axis=-1), xs, dskip_ref, wglu_ref, bglu_ref, gob_ref)

    mix = jnp.concatenate([an, bn], axis=-1).astype(BF16)
    hout_ref[...] = x + _dot(mix, wout_ref[...])
    st_ref[...] = hst_ref[...]
    for cp in v_out:
        cp.wait()


def _ffn_sample_kernel(n_batch, n_t,
                       h_ref, conv_hbm, gffn_ref, wfi_ref, cw_ref, cb_ref, wfo_ref, gfin_ref,
                       y_hbm, convout_hbm, carry_scr, y_buf, sem):
    n_hist = CONV_W - 1
    conv_in = [pltpu.make_async_copy(conv_hbm.at[:, k, :], carry_scr.at[pl.ds(k * n_batch, n_batch), :],
                                     sem.at[k]) for k in range(n_hist)]
    conv_out = [pltpu.make_async_copy(carry_scr.at[pl.ds(k * n_batch, n_batch), :], convout_hbm.at[:, k, :],
                                      sem.at[k]) for k in range(n_hist)]
    y_out = [pltpu.make_async_copy(y_buf.at[pl.ds(t * n_batch, n_batch), :], y_hbm.at[:, t, :],
                                   sem.at[n_hist + t]) for t in range(n_t)]
    for cp in conv_in:
        cp.start()
    for cp in conv_in:
        cp.wait()

    def get_h(s):
        return h_ref[s * SUB_ROWS:(s + 1) * SUB_ROWS, :]

    def put_y(s, y):
        y_buf[s * SUB_ROWS:(s + 1) * SUB_ROWS, :] = y

    _ffn_subblocks(get_h, put_y, n_batch, n_t * n_batch // SUB_ROWS, carry_scr,
                   gffn_ref, wfi_ref, cw_ref, cb_ref, wfo_ref, gfin_ref)
    for cp in conv_out + y_out:
        cp.start()
    for cp in conv_out + y_out:
        cp.wait()


def _const_spec(shape):
    zeros = (0,) * len(shape)
    return pl.BlockSpec(shape, lambda i: zeros, pipeline_mode=pl.Buffered(1))


def _params():
    return pltpu.CompilerParams(dimension_semantics=("arbitrary",), vmem_limit_bytes=VMEM_LIMIT_BYTES)


def _mixer_consts(h0, p, ws, bs):
    return [h0, p['g_mix'], p['w_in'], p['g_v'], ws, bs, p['ar'], p['ai'],
            p['bsub'], p['csub'], p['d_skip'], p['w_glu'], p['b_glu'],
            p['g_out_a'], p['g_out_b'], p['w_out']]


def _ffn_consts(conv0, p, wfi, wfo):
    return [conv0, p['g_ffn'], wfi, p['conv_w'], p['conv_b'], wfo, p['g_final']]


def _mixer_prompt_call(x, h0, p, ws, bs, w_ffn_in, w_ffn_out):
    n_batch, n_l, _ = x.shape
    n_steps = n_l // CHUNK
    assert n_steps >= 2
    blk = pl.BlockSpec((n_batch, CHUNK, D_MODEL), lambda i: (0, i, 0))
    wfi_blk = pl.BlockSpec((D_MODEL // n_steps, 2 * D_FF), lambda i: (i, 0))
    wfo_blk = pl.BlockSpec((D_FF // n_steps, D_MODEL), lambda i: (i, 0))
    consts = _mixer_consts(h0, p, ws, bs)
    return pl.pallas_call(
        functools.partial(_mixer_prompt_kernel, n_batch, n_steps),
        grid=(n_steps,),
        in_specs=[blk] + [_const_spec(c.shape) for c in consts] + [wfi_blk, wfo_blk],
        out_specs=[pl.BlockSpec(memory_space=pl.ANY),
                   pl.BlockSpec((n_batch, STATE_W), lambda i: (0, 0)), wfi_blk, wfo_blk],
        out_shape=[jax.ShapeDtypeStruct((n_l, n_batch, D_MODEL), F32),
                   jax.ShapeDtypeStruct((n_batch, STATE_W), F32),
                   jax.ShapeDtypeStruct(w_ffn_in.shape, BF16),
                   jax.ShapeDtypeStruct(w_ffn_out.shape, BF16)],
        scratch_shapes=[pltpu.VMEM((n_batch, STATE_W), F32),
                        pltpu.VMEM((2, n_batch, CHUNK, D_MODEL), F32),
                        pltpu.SemaphoreType.DMA((2,))],
        compiler_params=_params(),
        name="mixer_prompt",
    )(x, *consts, w_ffn_in, w_ffn_out)


def _ffn_prompt_call(h_tm, conv0, p, wfi, wfo):
    n_l, n_batch, _ = h_tm.shape
    n_t = FFN_PROMPT_STEPS
    n_steps = n_l // n_t
    assert n_steps >= 2
    halo = (CONV_W - 1) * n_batch
    blk = (n_t, n_batch, D_MODEL)
    consts = _ffn_consts(conv0, p, wfi, wfo)
    return pl.pallas_call(
        functools.partial(_ffn_prompt_kernel, n_batch, n_t, n_steps),
        grid=(n_steps,),
        in_specs=[pl.BlockSpec(blk, lambda i: (i, 0, 0))] + [_const_spec(c.shape) for c in consts],
        out_specs=[pl.BlockSpec(memory_space=pl.ANY), pl.BlockSpec((halo, 2 * D_FF), lambda i: (0, 0))],
        out_shape=[jax.ShapeDtypeStruct((n_batch, n_l, D_MODEL), F32),
                   jax.ShapeDtypeStruct((halo, 2 * D_FF), F32)],
        scratch_shapes=[pltpu.VMEM((halo, 2 * D_FF), F32),
                        pltpu.VMEM((2,) + blk, F32),
                        pltpu.SemaphoreType.DMA((2,))],
        compiler_params=_params(),
        name="ffn_prompt",
    )(h_tm, *consts)


def _mixer_sample_call(x, h0, p, ws, bs):
    n_batch, n_t, _ = x.shape
    rows = n_batch * n_t
    consts = _mixer_consts(h0, p, ws, bs)
    return pl.pallas_call(
        functools.partial(_mixer_sample_kernel, n_batch, n_t),
        grid=(1,),
        in_specs=[pl.BlockSpec(memory_space=pl.ANY)] + [_const_spec(c.shape) for c in consts],
        out_specs=[pl.BlockSpec((rows, D_MODEL), lambda i: (0, 0)),
                   pl.BlockSpec((n_batch, STATE_W), lambda i: (0, 0)),
                   pl.BlockSpec(memory_space=pl.ANY)],
        out_shape=[jax.ShapeDtypeStruct((rows, D_MODEL), F32),
                   jax.ShapeDtypeStruct((n_batch, STATE_W), F32),
                   jax.ShapeDtypeStruct((n_batch, n_t, A_HEADS, A_HEAD_DIM), F32)],
        scratch_shapes=[pltpu.VMEM((rows, D_MODEL), F32),
                        pltpu.VMEM((rows, A_WIDTH), F32),
                        pltpu.VMEM((rows, 2 * HALF_ST), F32),
                        pltpu.VMEM((n_batch, STATE_W), F32),
                        pltpu.SemaphoreType.DMA((n_t + n_t * A_HEADS,))],
        compiler_params=_params(),
        name="mixer_sample",
    )(x, *consts)


def _ffn_sample_call(h_tm, conv_buf, p, wfi, wfo):
    n_batch, n_hist, _ = conv_buf.shape
    rows = h_tm.shape[0]
    n_t = rows // n_batch
    consts = _ffn_consts(conv_buf, p, wfi, wfo)[1:]
    return pl.pallas_call(
        functools.partial(_ffn_sample_kernel, n_batch, n_t),
        grid=(1,),
        in_specs=[_const_spec(h_tm.shape), pl.BlockSpec(memory_space=pl.ANY)]
        + [_const_spec(c.shape) for c in consts],
        out_specs=[pl.BlockSpec(memory_space=pl.ANY), pl.BlockSpec(memory_space=pl.ANY)],
        out_shape=[jax.ShapeDtypeStruct((n_batch, n_t, D_MODEL), F32),
                   jax.ShapeDtypeStruct(conv_buf.shape, F32)],
        scratch_shapes=[pltpu.VMEM((n_hist * n_batch, 2 * D_FF), F32),
                        pltpu.VMEM((rows, D_MODEL), F32),
                        pltpu.SemaphoreType.DMA((n_hist + n_t,))],
        compiler_params=_params(),
        name="ffn_sample",
    )(h_tm, conv_buf, *consts)


def _prep_params(g_mix, w_in, g_v, lam_re, lam_im, log_dt, b_re, b_im, c_re, c_im,
                 d_skip, w_glu, b_glu, g_out_a, g_out_b, w_out, g_ffn, conv_w, conv_b, g_final):
    lr = lam_re.astype(F32)
    li = lam_im.astype(F32)
    dt = jnp.exp(log_dt.astype(F32))[:, None]
    mag = jnp.exp(lr * dt)
    ar = mag * jnp.cos(li * dt)
    ai = mag * jnp.sin(li * dt)
    den = lr * lr + li * li
    fr = ((ar - 1.0) * lr + ai * li) / den
    fi = (ai * lr - (ar - 1.0) * li) / den
    bre = b_re.astype(F32)
    bim = b_im.astype(F32)
    bbr = fr[..., None] * bre - fi[..., None] * bim
    bbi = fr[..., None] * bim + fi[..., None] * bre

    def blockdiag(m, rows_per_group, cols_per_group):
        tiled = jnp.tile(m.reshape(HALF_GROUPS * rows_per_group, cols_per_group), (1, HALF_GROUPS))
        row_g = lax.broadcasted_iota(jnp.int32, tiled.shape, 0) // rows_per_group
        col_g = lax.broadcasted_iota(jnp.int32, tiled.shape, 1) // cols_per_group
        return jnp.where(row_g == col_g, tiled, 0.0)

    def blockdiag_in(m):
        return blockdiag(jnp.transpose(m, (0, 2, 1)), SSM_GROUP, SSM_STATE)

    def blockdiag_out(m):
        return blockdiag(jnp.transpose(m, (0, 2, 1)), SSM_STATE, SSM_GROUP)

    bsub, csub, ars, ais = [], [], [], []
    for k in range(SSM_HALVES):
        gs = slice(k * HALF_GROUPS, (k + 1) * HALF_GROUPS)
        bsub.append(jnp.concatenate([blockdiag_in(bbr[gs]), blockdiag_in(bbi[gs])], axis=1))
        csub.append(jnp.concatenate([blockdiag_out(c_re[gs].astype(F32)),
                                     -blockdiag_out(c_im[gs].astype(F32))], axis=0))
        ars.append(ar[gs].reshape(1, HALF_ST))
        ais.append(ai[gs].reshape(1, HALF_ST))

    return dict(
        g_mix=g_mix.reshape(1, D_MODEL), w_in=w_in.astype(BF16), g_v=g_v,
        ar=jnp.concatenate(ars, axis=0), ai=jnp.concatenate(ais, axis=0),
        bsub=jnp.stack(bsub).astype(BF16), csub=jnp.stack(csub).astype(BF16),
        d_skip=d_skip.reshape(1, SSM_WIDTH), w_glu=w_glu.astype(BF16), b_glu=b_glu.reshape(1, SSM_WIDTH),
        g_out_a=g_out_a.reshape(1, A_WIDTH), g_out_b=g_out_b.reshape(1, SSM_WIDTH),
        w_out=w_out.astype(BF16), g_ffn=g_ffn.reshape(1, D_MODEL),
        conv_w=conv_w, conv_b=conv_b.reshape(1, 2 * D_FF), g_final=g_final.reshape(1, D_MODEL))


def _state_in(s_re, s_im):
    nb = s_re.shape[0]
    return jnp.concatenate([s_re.reshape(nb, ALL_ST), s_im.reshape(nb, ALL_ST)], axis=1)


def _state_out(st):
    nb = st.shape[0]
    return (st[:, :ALL_ST].reshape(nb, SSM_GROUPS, SSM_STATE),
            st[:, ALL_ST:].reshape(nb, SSM_GROUPS, SSM_STATE))


def _conv_state_out(rows_tm, nb):
    return jnp.transpose(rows_tm.reshape(CONV_W - 1, nb, 2 * D_FF), (1, 0, 2))


def kernel(x_prompt, x_sample, state_ssm_re, state_ssm_im, state_conv, g_mix, w_in, g_v, w_s, b_s,
           lam_re, lam_im, log_dt, b_re, b_im, c_re, c_im, d_skip, w_glu, b_glu, g_out_a, g_out_b,
           w_out, g_ffn, w_ffn_in, conv_w, conv_b, w_ffn_out, g_final):
    p = _prep_params(g_mix, w_in, g_v, lam_re, lam_im, log_dt, b_re, b_im, c_re, c_im, d_skip,
                     w_glu, b_glu, g_out_a, g_out_b, w_out, g_ffn, conv_w, conv_b, g_final)
    tril = jnp.tril(w_s.astype(F32))

    nbp = x_prompt.shape[0]
    zero_state = jnp.zeros((nbp, STATE_W), state_ssm_re.dtype)
    zero_conv = jnp.zeros(((CONV_W - 1) * nbp, 2 * D_FF), state_conv.dtype)
    bs_p = jnp.broadcast_to(jnp.transpose(b_s)[:, :, None], (CHUNK, A_HEADS, A_HEAD_DIM)).reshape(CHUNK, A_WIDTH)
    h_p, st_p, wfi, wfo = _mixer_prompt_call(x_prompt, zero_state, p,
                                             tril.astype(BF16), bs_p, w_ffn_in, w_ffn_out)
    y_p, conv_p = _ffn_prompt_call(h_p, zero_conv, p, wfi, wfo)
    re_p, im_p = _state_out(st_p)
    conv_p = _conv_state_out(conv_p, nbp)

    nls = x_sample.shape[1]
    ws_s = jnp.broadcast_to(tril[:, :nls, :nls].reshape(A_HEADS * nls * nls, 1), (A_HEADS * nls * nls, A_HEAD_DIM))
    bs_s = jnp.broadcast_to(b_s[:, :nls].reshape(A_HEADS * nls, 1), (A_HEADS * nls, A_HEAD_DIM))
    h_s, st_s, v_s = _mixer_sample_call(x_sample, _state_in(state_ssm_re, state_ssm_im), p, ws_s, bs_s)
    y_s, conv_s = _ffn_sample_call(h_s, state_conv, p, wfi, wfo)
    re_s, im_s = _state_out(st_s)
    return (y_p, y_s, v_s, re_p, im_p, conv_p, re_s, im_s, conv_s)
```

```python
import functools
import math

import jax
import jax.numpy as jnp
from jax import lax
from jax.experimental import pallas as pl
from jax.experimental.pallas import tpu as pltpu

D_MODEL = 1024
A_HEADS = 4
A_HEAD_DIM = 128
A_WIDTH = 512
CHUNK = 128
SSM_WIDTH = 512
SSM_GROUP = 16
SSM_GROUPS = 32
SSM_STATE = 64
D_FF = 2816
CONV_W = 3
EPS = 1e-6

SSM_HALVES = 2
HALF_GROUPS = SSM_GROUPS // SSM_HALVES
HALF_CH = HALF_GROUPS * SSM_GROUP
HALF_ST = HALF_GROUPS * SSM_STATE
ALL_ST = SSM_HALVES * HALF_ST
STATE_W = 2 * ALL_ST

FF_CHUNK = 256
FF_NCHUNK = D_FF // FF_CHUNK
SUB_ROWS = 256
FFN_PROMPT_STEPS = 64

SUBLANES = 8
SCAN_LANES = 512
VMEM_LIMIT_BYTES = 60 * 1024 * 1024

BF16 = jnp.bfloat16
F32 = jnp.float32


def _state_lanes(k):
    return k * HALF_ST, ALL_ST + k * HALF_ST


def _rms(x, g):
    ms = jnp.mean(x * x, axis=-1, keepdims=True)
    return x * lax.rsqrt(ms + EPS) * g


def _gelu(x):
    c = math.sqrt(2.0 / math.pi)
    return x * (0.5 * (1.0 + jnp.tanh(c * (x + 0.044715 * (x * x * x)))))


def _dot(a, b):
    return jnp.dot(a, b, preferred_element_type=F32)


def _to_time_major(x, n_batch, n_t):
    w = x.shape[-1]
    return jnp.swapaxes(x.reshape(n_batch, n_t, w), 0, 1).reshape(n_t * n_batch, w)


def _to_batch_major(x, n_batch, n_t):
    w = x.shape[-1]
    return jnp.swapaxes(x.reshape(n_t, n_batch, w), 0, 1)


def _glu_norm(ys, xs, dskip_ref, wglu_ref, bglu_ref, gob_ref):
    g = _gelu(ys + dskip_ref[...] * xs)
    z = _dot(g.astype(BF16), wglu_ref[...]) + bglu_ref[...]
    return _rms(g * jax.nn.sigmoid(z), gob_ref[...])


def _ffn_subblocks(get_h, put_y, n_batch, n_sub, carry_scr,
                   gffn_ref, wfi_ref, cw_ref, cb_ref, wfo_ref, gfin_ref):
    halo = (CONV_W - 1) * n_batch
    hs, n2s = {}, {}

    def norm_in(s):
        hs[s] = get_h(s)
        n2s[s] = _rms(hs[s], gffn_ref[...]).astype(BF16)

    def col_slices(j):
        return [slice(c0, c0 + FF_CHUNK) for c0 in (j * FF_CHUNK, D_FF + j * FF_CHUNK)]

    def up_proj(j, s):
        return [jnp.dot(n2s[s], wfi_ref[:, cs], preferred_element_type=F32) for cs in col_slices(j)]

    order = [(j, s) for j in range(FF_NCHUNK) for s in range(n_sub)]
    accs = [jnp.zeros((SUB_ROWS, D_MODEL), F32) for _ in range(n_sub)]
    tails = [None, None]
    norm_in(0)
    ups_next = up_proj(*order[0])
    for s in range(1, n_sub):
        norm_in(s)
    for idx, (j, s) in enumerate(order):
        ups = ups_next
        if idx + 1 < len(order):
            ups_next = up_proj(*order[idx + 1])
        conv = []
        for ci, (cs, up) in enumerate(zip(col_slices(j), ups)):
            tail = carry_scr[:, cs] if s == 0 else tails[ci]
            padded = jnp.concatenate([tail, up], axis=0)
            tails[ci] = padded[SUB_ROWS:SUB_ROWS + halo, :]
            if s == n_sub - 1:
                carry_scr[:, cs] = tails[ci]
            out = cb_ref[:, cs]
            for tap in range(CONV_W):
                out = out + cw_ref[tap:tap + 1, cs] * padded[tap * n_batch:tap * n_batch + SUB_ROWS, :]
            conv.append(out)
        act = _gelu(conv[0]) * conv[1]
        accs[s] = accs[s] + jnp.dot(act.astype(BF16), wfo_ref[j * FF_CHUNK:(j + 1) * FF_CHUNK, :],
                                    preferred_element_type=F32)
        if j == FF_NCHUNK - 1:
            put_y(s, _rms(hs[s] + accs[s], gfin_ref[...]))


def _mixer_prompt_kernel(n_batch, n_steps,
                         x_ref, h0_ref, gmix_ref, win_ref, gv_ref, ws_ref, bs_ref,
                         ar_ref, ai_ref, bsub_ref, csub_ref, dskip_ref,
                         wglu_ref, bglu_ref, goa_ref, gob_ref, wout_ref, wfi32_ref, wfo32_ref,
                         hout_hbm, st_ref, wfi16_ref, wfo16_ref, hst_ref, h_buf, sem):
    sub_t = SUB_ROWS // n_batch
    n_sub = CHUNK // sub_t
    i = pl.program_id(0)
    slot = lax.rem(i, 2)

    def h_copies(step, buf_slot):
        return [pltpu.make_async_copy(h_buf.at[buf_slot, b], hout_hbm.at[pl.ds(step * CHUNK, CHUNK), b, :],
                                      sem.at[buf_slot]) for b in range(n_batch)]

    @pl.when(i == 0)
    def _():
        hst_ref[...] = h0_ref[...]

    @pl.when(i >= 2)
    def _():
        for cp in h_copies(i - 2, slot):
            cp.wait()

    state, trans = [], []
    for k in range(SSM_HALVES):
        re0, im0 = _state_lanes(k)
        state.append((hst_ref[:, re0:re0 + HALF_ST], hst_ref[:, im0:im0 + HALF_ST]))
        trans.append((jnp.broadcast_to(ar_ref[k:k + 1, :], (n_batch, HALF_ST)),
                      jnp.broadcast_to(ai_ref[k:k + 1, :], (n_batch, HALF_ST))))

    x_sub, proj_sub, xs_sub, y_sub, bn_sub = {}, {}, {}, {}, {}
    a_heads = [[] for _ in range(n_sub)]

    def project(s):
        x = x_ref[:, s * sub_t:(s + 1) * sub_t, :].reshape(SUB_ROWS, D_MODEL)
        n1 = _rms(x, gmix_ref[...]).astype(BF16)
        x_sub[s] = x
        proj_sub[s] = _dot(n1, win_ref[...])
        xs_sub[s] = _to_time_major(proj_sub[s][:, 2 * A_WIDTH:], n_batch, sub_t)
        y_sub[s] = []

    def ssm_half(s, k):
        u = _dot(xs_sub[s].astype(BF16)[:, k * HALF_CH:(k + 1) * HALF_CH], bsub_ref[k])
        ar, ai = trans[k]
        hr, hi = state[k]
        slabs = []
        for t in range(sub_t):
            ur = u[t * n_batch:(t + 1) * n_batch, :HALF_ST]
            ui = u[t * n_batch:(t + 1) * n_batch, HALF_ST:]
            hr, hi = ar * hr - ai * hi + ur, ar * hi + ai * hr + ui
            slabs.append(jnp.concatenate([hr, hi], axis=-1))
        state[k] = (hr, hi)
        y_sub[s].append(_dot(jnp.concatenate(slabs, axis=0).astype(BF16), csub_ref[k]))

    def glu(s):
        bn_tm = _glu_norm(jnp.concatenate(y_sub[s], axis=-1), xs_sub[s], dskip_ref, wglu_ref, bglu_ref,
                          gob_ref)
        bn_sub[s] = _to_batch_major(bn_tm, n_batch, sub_t).reshape(SUB_ROWS, SSM_WIDTH)

    def gate_head(h):
        hs = slice(h * A_HEAD_DIM, (h + 1) * A_HEAD_DIM)
        vn = [_rms(proj_sub[s][:, A_WIDTH + h * A_HEAD_DIM:A_WIDTH + (h + 1) * A_HEAD_DIM],
                   gv_ref[h:h + 1, :]).astype(BF16) for s in range(n_sub)]
        v_wide = jnp.concatenate(
            [jnp.concatenate([vn[s][b * sub_t:(b + 1) * sub_t, :] for s in range(n_sub)], axis=0)
             for b in range(n_batch)], axis=-1)
        s_wide = _dot(ws_ref[h], v_wide)
        for s in range(n_sub):
            ts = slice(s * sub_t, (s + 1) * sub_t)
            bias = bs_ref[ts, hs]
            gate = jnp.concatenate([s_wide[ts, b * A_HEAD_DIM:(b + 1) * A_HEAD_DIM] + bias
                                    for b in range(n_batch)], axis=0)
            a_heads[s].append(proj_sub[s][:, hs] * gate)

    def out_proj(s):
        an = _rms(jnp.concatenate(a_heads[s], axis=-1), goa_ref[...])
        mix = jnp.concatenate([an, bn_sub[s]], axis=-1).astype(BF16)
        h_buf[slot, :, s * sub_t:(s + 1) * sub_t, :] = (
            x_sub[s] + _dot(mix, wout_ref[...])).reshape(n_batch, sub_t, D_MODEL)

    chain = [project] + [functools.partial(ssm_half, k=k) for k in range(SSM_HALVES)] + [glu]
    heads = list(range(A_HEADS))
    outs = list(range(n_sub))
    for wave in range(n_sub + len(chain) - 1):
        if wave >= n_sub:
            for h in heads[:2]:
                gate_head(h)
            heads = heads[2:]
        for s in range(n_sub):
            if 0 <= wave - s < len(chain):
                chain[wave - s](s)
        if not heads and outs and wave - outs[0] >= len(chain) - 1:
            out_proj(outs.pop(0))
    for h in heads:
        gate_head(h)
    for s in outs:
        out_proj(s)

    for k in range(SSM_HALVES):
        re0, im0 = _state_lanes(k)
        hst_ref[:, re0:re0 + HALF_ST] = state[k][0]
        hst_ref[:, im0:im0 + HALF_ST] = state[k][1]

    wfi16_ref[...] = wfi32_ref[...].astype(BF16)
    wfo16_ref[...] = wfo32_ref[...].astype(BF16)
    for cp in h_copies(i, slot):
        cp.start()

    @pl.when(i == n_steps - 1)
    def _():
        st_ref[...] = hst_ref[...]
        for cp in h_copies(i - 1, 1 - slot) + h_copies(i, slot):
            cp.wait()


def _ffn_kernel(n_batch, n_t, n_steps, ns_batch, ns_t,
                h_ref, conv0_ref, hs_ref, convs_hbm, gffn_ref, wfi_ref, cw_ref, cb_ref, wfo_ref, gfin_ref,
                y_hbm, convout_ref, ys_hbm, convs_out_hbm,
                carry_scr, y_buf, sem, carry_s_scr, ys_buf, sem_s):
    sub_t = SUB_ROWS // n_batch
    i = pl.program_id(0)
    slot = lax.rem(i, 2)
    weights = (gffn_ref, wfi_ref, cw_ref, cb_ref, wfo_ref, gfin_ref)
    n_hist = CONV_W - 1

    def y_copies(step, buf_slot):
        return [pltpu.make_async_copy(y_buf.at[buf_slot, :, b, :], y_hbm.at[b, pl.ds(step * n_t, n_t), :],
                                      sem.at[buf_slot]) for b in range(n_batch)]

    hist_in = [pltpu.make_async_copy(convs_hbm.at[:, k, :], carry_s_scr.at[pl.ds(k * ns_batch, ns_batch), :],
                                     sem_s.at[k]) for k in range(n_hist)]
    hist_out = [pltpu.make_async_copy(carry_s_scr.at[pl.ds(k * ns_batch, ns_batch), :], convs_out_hbm.at[:, k, :],
                                      sem_s.at[k]) for k in range(n_hist)]
    ys_out = [pltpu.make_async_copy(ys_buf.at[pl.ds(t * ns_batch, ns_batch), :], ys_hbm.at[:, t, :],
                                    sem_s.at[n_hist + t]) for t in range(ns_t)]

    @pl.when(i == 0)
    def _():
        carry_scr[...] = conv0_ref[...]
        for cp in hist_in:
            cp.start()

    @pl.when(jnp.logical_and(i >= 2, i < n_steps))
    def _():
        for cp in y_copies(i - 2, slot):
            cp.wait()

    @pl.when(i < n_steps)
    def _():
        def get_h(s):
            return h_ref[s * sub_t:(s + 1) * sub_t].reshape(SUB_ROWS, D_MODEL)

        def put_y(s, y):
            y_buf[slot, s * sub_t:(s + 1) * sub_t] = y.reshape(sub_t, n_batch, D_MODEL)

        _ffn_subblocks(get_h, put_y, n_batch, n_t // sub_t, carry_scr, *weights)
        for cp in y_copies(i, slot):
            cp.start()

    @pl.when(i == n_steps - 1)
    def _():
        convout_ref[...] = carry_scr[...]
        for cp in y_copies(i - 1, 1 - slot) + y_copies(i, slot):
            cp.wait()

    @pl.when(i == n_steps)
    def _():
        for cp in hist_in:
            cp.wait()

        def get_h(s):
            return hs_ref[s * SUB_ROWS:(s + 1) * SUB_ROWS, :]

        def put_y(s, y):
            ys_buf[s * SUB_ROWS:(s + 1) * SUB_ROWS, :] = y

        _ffn_subblocks(get_h, put_y, ns_batch, ns_t * ns_batch // SUB_ROWS, carry_s_scr, *weights)
        for cp in hist_out + ys_out:
            cp.start()
        for cp in hist_out + ys_out:
            cp.wait()


def _scan_inplace(u_scr, hst_ref, ar_ref, ai_ref, k, n_batch, n_t):
    re0, im0 = _state_lanes(k)

    def row_chunk(rc, _):
        r0 = pl.multiple_of(rc * SUBLANES, SUBLANES)
        for c in range(HALF_ST // SCAN_LANES):
            lo = c * SCAN_LANES
            re_l = slice(lo, lo + SCAN_LANES)
            im_l = slice(HALF_ST + lo, HALF_ST + lo + SCAN_LANES)
            ar = jnp.broadcast_to(ar_ref[k:k + 1, re_l], (SUBLANES, SCAN_LANES))
            ai = jnp.broadcast_to(ai_ref[k:k + 1, re_l], (SUBLANES, SCAN_LANES))
            hr0 = hst_ref[pl.ds(r0, SUBLANES), re0 + lo:re0 + lo + SCAN_LANES]
            hi0 = hst_ref[pl.ds(r0, SUBLANES), im0 + lo:im0 + lo + SCAN_LANES]

            def step(t, carry):
                hr, hi = carry
                row = pl.multiple_of(t * n_batch + r0, SUBLANES)
                ur = u_scr[pl.ds(row, SUBLANES), re_l]
                ui = u_scr[pl.ds(row, SUBLANES), im_l]
                nhr = ar * hr - ai * hi + ur
                nhi = ar * hi + ai * hr + ui
                u_scr[pl.ds(row, SUBLANES), re_l] = nhr
                u_scr[pl.ds(row, SUBLANES), im_l] = nhi
                return nhr, nhi

            hr, hi = lax.fori_loop(0, n_t, step, (hr0, hi0), unroll=min(n_t, 4))
            hst_ref[pl.ds(r0, SUBLANES), re0 + lo:re0 + lo + SCAN_LANES] = hr
            hst_ref[pl.ds(r0, SUBLANES), im0 + lo:im0 + lo + SCAN_LANES] = hi
        return 0

    lax.fori_loop(0, n_batch // SUBLANES, row_chunk, 0)


def _mixer_sample_kernel(n_batch, n_t,
                         x_ref, h0_ref, gmix_ref, win_ref, gv_ref, ws_ref, bs_ref,
                         ar_ref, ai_ref, bsub_ref, csub_ref, dskip_ref,
                         wglu_ref, bglu_ref, goa_ref, gob_ref, wout_ref,
                         hout_ref, st_ref, v_hbm, x_buf, v_buf, u_scr, hst_ref, sem):
    x_in = [pltpu.make_async_copy(x_ref.at[:, t, :], x_buf.at[pl.ds(t * n_batch, n_batch), :], sem.at[t])
            for t in range(n_t)]
    v_out = [pltpu.make_async_copy(
        v_buf.at[pl.ds(t * n_batch, n_batch), pl.ds(h * A_HEAD_DIM, A_HEAD_DIM)],
        v_hbm.at[:, t, h, :], sem.at[n_t + t * A_HEADS + h]) for t in range(n_t) for h in range(A_HEADS)]
    for cp in x_in:
        cp.start()
    hst_ref[...] = h0_ref[...]
    for cp in x_in:
        cp.wait()
    x = x_buf[...]
    n1 = _rms(x, gmix_ref[...]).astype(BF16)
    proj = _dot(n1, win_ref[...])

    a_heads = []
    for h in range(A_HEADS):
        hs = slice(h * A_HEAD_DIM, (h + 1) * A_HEAD_DIM)
        vn = _rms(proj[:, A_WIDTH + h * A_HEAD_DIM:A_WIDTH + (h + 1) * A_HEAD_DIM], gv_ref[h:h + 1, :])
        v_buf[:, hs] = vn
        slabs = []
        for t in range(n_t):
            s = jnp.broadcast_to(bs_ref[h * n_t + t:h * n_t + t + 1, :], (n_batch, A_HEAD_DIM))
            for src in range(t + 1):
                r = (h * n_t + t) * n_t + src
                s = s + ws_ref[r:r + 1, :] * vn[src * n_batch:(src + 1) * n_batch, :]
            slabs.append(proj[t * n_batch:(t + 1) * n_batch, hs] * s)
        a_heads.append(jnp.concatenate(slabs, axis=0))
    an = _rms(jnp.concatenate(a_heads, axis=-1), goa_ref[...])
    for cp in v_out:
        cp.start()

    xs = proj[:, 2 * A_WIDTH:]
    xs_b = xs.astype(BF16)
    y_parts = []
    for k in range(SSM_HALVES):
        u_scr[...] = _dot(xs_b[:, k * HALF_CH:(k + 1) * HALF_CH], bsub_ref[k])
        _scan_inplace(u_scr, hst_ref, ar_ref, ai_ref, k, n_batch, n_t)
        y_parts.append(_dot(u_scr[...].astype(BF16), csub_ref[k]))
    bn = _glu_norm(jnp.concatenate(y_parts, axis=-1), xs, dskip_ref, wglu_ref, bglu_ref, gob_ref)

    mix = jnp.concatenate([an, bn], axis=-1).astype(BF16)
    hout_ref[...] = x + _dot(mix, wout_ref[...])
    st_ref[...] = hst_ref[...]
    for cp in v_out:
        cp.wait()


def _const_spec(shape):
    zeros = (0,) * len(shape)
    return pl.BlockSpec(shape, lambda i: zeros, pipeline_mode=pl.Buffered(1))


def _params():
    return pltpu.CompilerParams(dimension_semantics=("arbitrary",), vmem_limit_bytes=VMEM_LIMIT_BYTES)


def _mixer_consts(h0, p, ws, bs):
    return [h0, p['g_mix'], p['w_in'], p['g_v'], ws, bs, p['ar'], p['ai'],
            p['bsub'], p['csub'], p['d_skip'], p['w_glu'], p['b_glu'],
            p['g_out_a'], p['g_out_b'], p['w_out']]


def _ffn_consts(conv0, p, wfi, wfo):
    return [conv0, p['g_ffn'], wfi, p['conv_w'], p['conv_b'], wfo, p['g_final']]


def _mixer_prompt_call(x, h0, p, ws, bs, w_ffn_in, w_ffn_out):
    n_batch, n_l, _ = x.shape
    n_steps = n_l // CHUNK
    assert n_steps >= 2
    blk = pl.BlockSpec((n_batch, CHUNK, D_MODEL), lambda i: (0, i, 0))
    wfi_blk = pl.BlockSpec((D_MODEL // n_steps, 2 * D_FF), lambda i: (i, 0))
    wfo_blk = pl.BlockSpec((D_FF // n_steps, D_MODEL), lambda i: (i, 0))
    consts = _mixer_consts(h0, p, ws, bs)
    return pl.pallas_call(
        functools.partial(_mixer_prompt_kernel, n_batch, n_steps),
        grid=(n_steps,),
        in_specs=[blk] + [_const_spec(c.shape) for c in consts] + [wfi_blk, wfo_blk],
        out_specs=[pl.BlockSpec(memory_space=pl.ANY),
                   pl.BlockSpec((n_batch, STATE_W), lambda i: (0, 0)), wfi_blk, wfo_blk],
        out_shape=[jax.ShapeDtypeStruct((n_l, n_batch, D_MODEL), F32),
                   jax.ShapeDtypeStruct((n_batch, STATE_W), F32),
                   jax.ShapeDtypeStruct(w_ffn_in.shape, BF16),
                   jax.ShapeDtypeStruct(w_ffn_out.shape, BF16)],
        scratch_shapes=[pltpu.VMEM((n_batch, STATE_W), F32),
                        pltpu.VMEM((2, n_batch, CHUNK, D_MODEL), F32),
                        pltpu.SemaphoreType.DMA((2,))],
        compiler_params=_params(),
        name="mixer_prompt",
    )(x, *consts, w_ffn_in, w_ffn_out)


def _ffn_call(h_tm, conv0, hs_tm, conv_s, p, wfi, wfo):
    n_l, n_batch, _ = h_tm.shape
    ns_batch, n_hist, _ = conv_s.shape
    rows_s = hs_tm.shape[0]
    ns_t = rows_s // ns_batch
    n_t = FFN_PROMPT_STEPS
    n_steps = n_l // n_t
    assert n_steps >= 2
    halo = n_hist * n_batch
    blk = (n_t, n_batch, D_MODEL)
    weights = _ffn_consts(conv0, p, wfi, wfo)[1:]
    return pl.pallas_call(
        functools.partial(_ffn_kernel, n_batch, n_t, n_steps, ns_batch, ns_t),
        grid=(n_steps + 1,),
        in_specs=[pl.BlockSpec(blk, lambda i: (jnp.minimum(i, n_steps - 1), 0, 0)),
                  _const_spec(conv0.shape), _const_spec(hs_tm.shape), pl.BlockSpec(memory_space=pl.ANY)]
        + [_const_spec(c.shape) for c in weights],
        out_specs=[pl.BlockSpec(memory_space=pl.ANY), pl.BlockSpec((halo, 2 * D_FF), lambda i: (0, 0)),
                   pl.BlockSpec(memory_space=pl.ANY), pl.BlockSpec(memory_space=pl.ANY)],
        out_shape=[jax.ShapeDtypeStruct((n_batch, n_l, D_MODEL), F32),
                   jax.ShapeDtypeStruct((halo, 2 * D_FF), F32),
                   jax.ShapeDtypeStruct((ns_batch, ns_t, D_MODEL), F32),
                   jax.ShapeDtypeStruct(conv_s.shape, F32)],
        scratch_shapes=[pltpu.VMEM((halo, 2 * D_FF), F32),
                        pltpu.VMEM((2,) + blk, F32),
                        pltpu.SemaphoreType.DMA((2,)),
                        pltpu.VMEM((n_hist * ns_batch, 2 * D_FF), F32),
                        pltpu.VMEM((rows_s, D_MODEL), F32),
                        pltpu.SemaphoreType.DMA((n_hist + ns_t,))],
        compiler_params=_params(),
        name="ffn",
    )(h_tm, conv0, hs_tm, conv_s, *weights)


def _mixer_sample_call(x, h0, p, ws, bs):
    n_batch, n_t, _ = x.shape
    rows = n_batch * n_t
    consts = _mixer_consts(h0, p, ws, bs)
    return pl.pallas_call(
        functools.partial(_mixer_sample_kernel, n_batch, n_t),
        grid=(1,),
        in_specs=[pl.BlockSpec(memory_space=pl.ANY)] + [_const_spec(c.shape) for c in consts],
        out_specs=[pl.BlockSpec((rows, D_MODEL), lambda i: (0, 0)),
                   pl.BlockSpec((n_batch, STATE_W), lambda i: (0, 0)),
                   pl.BlockSpec(memory_space=pl.ANY)],
        out_shape=[jax.ShapeDtypeStruct((rows, D_MODEL), F32),
                   jax.ShapeDtypeStruct((n_batch, STATE_W), F32),
                   jax.ShapeDtypeStruct((n_batch, n_t, A_HEADS, A_HEAD_DIM), F32)],
        scratch_shapes=[pltpu.VMEM((rows, D_MODEL), F32),
                        pltpu.VMEM((rows, A_WIDTH), F32),
                        pltpu.VMEM((rows, 2 * HALF_ST), F32),
                        pltpu.VMEM((n_batch, STATE_W), F32),
                        pltpu.SemaphoreType.DMA((n_t + n_t * A_HEADS,))],
        compiler_params=_params(),
        name="mixer_sample",
    )(x, *consts)


def _prep_params(g_mix, w_in, g_v, lam_re, lam_im, log_dt, b_re, b_im, c_re, c_im,
                 d_skip, w_glu, b_glu, g_out_a, g_out_b, w_out, g_ffn, conv_w, conv_b, g_final):
    lr = lam_re.astype(F32)
    li = lam_im.astype(F32)
    dt = jnp.exp(log_dt.astype(F32))[:, None]
    mag = jnp.exp(lr * dt)
    ar = mag * jnp.cos(li * dt)
    ai = mag * jnp.sin(li * dt)
    den = lr * lr + li * li
    fr = ((ar - 1.0) * lr + ai * li) / den
    fi = (ai * lr - (ar - 1.0) * li) / den
    bre = b_re.astype(F32)
    bim = b_im.astype(F32)
    bbr = fr[..., None] * bre - fi[..., None] * bim
    bbi = fr[..., None] * bim + fi[..., None] * bre

    def blockdiag(m, rows_per_group, cols_per_group):
        tiled = jnp.tile(m.reshape(HALF_GROUPS * rows_per_group, cols_per_group), (1, HALF_GROUPS))
        row_g = lax.broadcasted_iota(jnp.int32, tiled.shape, 0) // rows_per_group
        col_g = lax.broadcasted_iota(jnp.int32, tiled.shape, 1) // cols_per_group
        return jnp.where(row_g == col_g, tiled, 0.0)

    def blockdiag_in(m):
        return blockdiag(jnp.transpose(m, (0, 2, 1)), SSM_GROUP, SSM_STATE)

    def blockdiag_out(m):
        return blockdiag(jnp.transpose(m, (0, 2, 1)), SSM_STATE, SSM_GROUP)

    bsub, csub, ars, ais = [], [], [], []
    for k in range(SSM_HALVES):
        gs = slice(k * HALF_GROUPS, (k + 1) * HALF_GROUPS)
        bsub.append(jnp.concatenate([blockdiag_in(bbr[gs]), blockdiag_in(bbi[gs])], axis=1))
        csub.append(jnp.concatenate([blockdiag_out(c_re[gs].astype(F32)),
                                     -blockdiag_out(c_im[gs].astype(F32))], axis=0))
        ars.append(ar[gs].reshape(1, HALF_ST))
        ais.append(ai[gs].reshape(1, HALF_ST))

    return dict(
        g_mix=g_mix.reshape(1, D_MODEL), w_in=w_in.astype(BF16), g_v=g_v,
        ar=jnp.concatenate(ars, axis=0), ai=jnp.concatenate(ais, axis=0),
        bsub=jnp.stack(bsub).astype(BF16), csub=jnp.stack(csub).astype(BF16),
        d_skip=d_skip.reshape(1, SSM_WIDTH), w_glu=w_glu.astype(BF16), b_glu=b_glu.reshape(1, SSM_WIDTH),
        g_out_a=g_out_a.reshape(1, A_WIDTH), g_out_b=g_out_b.reshape(1, SSM_WIDTH),
        w_out=w_out.astype(BF16), g_ffn=g_ffn.reshape(1, D_MODEL),
        conv_w=conv_w, conv_b=conv_b.reshape(1, 2 * D_FF), g_final=g_final.reshape(1, D_MODEL))


def _state_in(s_re, s_im):
    nb = s_re.shape[0]
    return jnp.concatenate([s_re.reshape(nb, ALL_ST), s_im.reshape(nb, ALL_ST)], axis=1)


def _state_out(st):
    nb = st.shape[0]
    return (st[:, :ALL_ST].reshape(nb, SSM_GROUPS, SSM_STATE),
            st[:, ALL_ST:].reshape(nb, SSM_GROUPS, SSM_STATE))


def _conv_state_out(rows_tm, nb):
    return jnp.transpose(rows_tm.reshape(CONV_W - 1, nb, 2 * D_FF), (1, 0, 2))


def kernel(x_prompt, x_sample, state_ssm_re, state_ssm_im, state_conv, g_mix, w_in, g_v, w_s, b_s,
           lam_re, lam_im, log_dt, b_re, b_im, c_re, c_im, d_skip, w_glu, b_glu, g_out_a, g_out_b,
           w_out, g_ffn, w_ffn_in, conv_w, conv_b, w_ffn_out, g_final):
    p = _prep_params(g_mix, w_in, g_v, lam_re, lam_im, log_dt, b_re, b_im, c_re, c_im, d_skip,
                     w_glu, b_glu, g_out_a, g_out_b, w_out, g_ffn, conv_w, conv_b, g_final)
    tril = jnp.tril(w_s.astype(F32))

    nbp = x_prompt.shape[0]
    zero_state = jnp.zeros((nbp, STATE_W), state_ssm_re.dtype)
    zero_conv = jnp.zeros(((CONV_W - 1) * nbp, 2 * D_FF), state_conv.dtype)
    bs_p = jnp.broadcast_to(jnp.transpose(b_s)[:, :, None], (CHUNK, A_HEADS, A_HEAD_DIM)).reshape(CHUNK, A_WIDTH)
    h_p, st_p, wfi, wfo = _mixer_prompt_call(x_prompt, zero_state, p,
                                             tril.astype(BF16), bs_p, w_ffn_in, w_ffn_out)
    re_p, im_p = _state_out(st_p)

    nls = x_sample.shape[1]
    ws_s = jnp.broadcast_to(tril[:, :nls, :nls].reshape(A_HEADS * nls * nls, 1), (A_HEADS * nls * nls, A_HEAD_DIM))
    bs_s = jnp.broadcast_to(b_s[:, :nls].reshape(A_HEADS * nls, 1), (A_HEADS * nls, A_HEAD_DIM))
    h_s, st_s, v_s = _mixer_sample_call(x_sample, _state_in(state_ssm_re, state_ssm_im), p, ws_s, bs_s)
    re_s, im_s = _state_out(st_s)

    y_p, conv_p, y_s, conv_s = _ffn_call(h_p, zero_conv, h_s, state_conv, p, wfi, wfo)
    conv_p = _conv_state_out(conv_p, nbp)
    return (y_p, y_s, v_s, re_p, im_p, conv_p, re_s, im_s, conv_s)
```

```python
import functools
import math

import jax
import jax.numpy as jnp
from jax import lax
from jax.experimental import pallas as pl
from jax.experimental.pallas import tpu as pltpu

D_MODEL = 1024
A_HEADS = 4
A_HEAD_DIM = 128
A_WIDTH = 512
CHUNK = 128
SSM_WIDTH = 512
SSM_GROUP = 16
SSM_GROUPS = 32
SSM_STATE = 64
D_FF = 2816
CONV_W = 3
EPS = 1e-6

SSM_HALVES = 2
HALF_GROUPS = SSM_GROUPS // SSM_HALVES
HALF_CH = HALF_GROUPS * SSM_GROUP
HALF_ST = HALF_GROUPS * SSM_STATE
ALL_ST = SSM_HALVES * HALF_ST
STATE_W = 2 * ALL_ST

FF_CHUNK = 256
FF_NCHUNK = D_FF // FF_CHUNK
SUB_ROWS = 256
FFN_PROMPT_STEPS = 64

SUBLANES = 8
SCAN_LANES = 512
VMEM_LIMIT_BYTES = 60 * 1024 * 1024

BF16 = jnp.bfloat16
F32 = jnp.float32


def _state_lanes(k):
    return k * HALF_ST, ALL_ST + k * HALF_ST


def _rms(x, g):
    ms = jnp.mean(x * x, axis=-1, keepdims=True)
    return x * lax.rsqrt(ms + EPS) * g


def _gelu(x):
    c = math.sqrt(2.0 / math.pi)
    return x * (0.5 * (1.0 + jnp.tanh(c * (x + 0.044715 * (x * x * x)))))


def _dot(a, b):
    return jnp.dot(a, b, preferred_element_type=F32)


def _to_time_major(x, n_batch, n_t):
    w = x.shape[-1]
    return jnp.swapaxes(x.reshape(n_batch, n_t, w), 0, 1).reshape(n_t * n_batch, w)


def _to_batch_major(x, n_batch, n_t):
    w = x.shape[-1]
    return jnp.swapaxes(x.reshape(n_t, n_batch, w), 0, 1)


def _glu_norm(ys, xs, dskip_ref, wglu_ref, bglu_ref, gob_ref):
    g = _gelu(ys + dskip_ref[...] * xs)
    z = _dot(g.astype(BF16), wglu_ref[...]) + bglu_ref[...]
    return _rms(g * jax.nn.sigmoid(z), gob_ref[...])


def _ffn_subblocks(get_h, put_y, n_batch, n_sub, carry_scr,
                   gffn_ref, wfi_ref, cw_ref, cb_ref, wfo_ref, gfin_ref):
    halo = (CONV_W - 1) * n_batch
    hs, n2s = {}, {}

    def norm_in(s):
        hs[s] = get_h(s)
        n2s[s] = _rms(hs[s], gffn_ref[...]).astype(BF16)

    def col_slices(j):
        return [slice(c0, c0 + FF_CHUNK) for c0 in (j * FF_CHUNK, D_FF + j * FF_CHUNK)]

    def up_proj(j, s):
        return [jnp.dot(n2s[s], wfi_ref[:, cs], preferred_element_type=F32) for cs in col_slices(j)]

    order = [(j, s) for j in range(FF_NCHUNK) for s in range(n_sub)]
    accs = [jnp.zeros((SUB_ROWS, D_MODEL), F32) for _ in range(n_sub)]
    tails = [None, None]
    norm_in(0)
    ups_next = up_proj(*order[0])
    for s in range(1, n_sub):
        norm_in(s)
    for idx, (j, s) in enumerate(order):
        ups = ups_next
        if idx + 1 < len(order):
            ups_next = up_proj(*order[idx + 1])
        conv = []
        for ci, (cs, up) in enumerate(zip(col_slices(j), ups)):
            tail = carry_scr[:, cs] if s == 0 else tails[ci]
            padded = jnp.concatenate([tail, up], axis=0)
            tails[ci] = padded[SUB_ROWS:SUB_ROWS + halo, :]
            if s == n_sub - 1:
                carry_scr[:, cs] = tails[ci]
            out = cb_ref[:, cs]
            for tap in range(CONV_W):
                out = out + cw_ref[tap:tap + 1, cs] * padded[tap * n_batch:tap * n_batch + SUB_ROWS, :]
            conv.append(out)
        act = _gelu(conv[0]) * conv[1]
        accs[s] = accs[s] + jnp.dot(act.astype(BF16), wfo_ref[j * FF_CHUNK:(j + 1) * FF_CHUNK, :],
                                    preferred_element_type=F32)
        if j == FF_NCHUNK - 1:
            put_y(s, _rms(hs[s] + accs[s], gfin_ref[...]))


def _mixer_prompt_kernel(n_batch, n_steps,
                         x_ref, gmix_ref, win_ref, gv_ref, ws_ref, bs_ref,
                         ar_ref, ai_ref, bsub_ref, csub_ref, dskip_ref,
                         wglu_ref, bglu_ref, goa_ref, gob_ref, wout_ref, wfi32_ref, wfo32_ref,
                         hout_hbm, stre_ref, stim_ref, wfi16_ref, wfo16_ref, hst_ref, h_buf, sem):
    sub_t = SUB_ROWS // n_batch
    n_sub = CHUNK // sub_t
    i = pl.program_id(0)
    slot = lax.rem(i, 2)

    def h_copies(step, buf_slot):
        return [pltpu.make_async_copy(h_buf.at[buf_slot, b], hout_hbm.at[pl.ds(step * CHUNK, CHUNK), b, :],
                                      sem.at[buf_slot]) for b in range(n_batch)]

    @pl.when(i == 0)
    def _():
        hst_ref[...] = jnp.zeros(hst_ref.shape, F32)

    @pl.when(i >= 2)
    def _():
        for cp in h_copies(i - 2, slot):
            cp.wait()

    state, trans = [], []
    for k in range(SSM_HALVES):
        re0, im0 = _state_lanes(k)
        state.append((hst_ref[:, re0:re0 + HALF_ST], hst_ref[:, im0:im0 + HALF_ST]))
        trans.append((jnp.broadcast_to(ar_ref[k:k + 1, :], (n_batch, HALF_ST)),
                      jnp.broadcast_to(ai_ref[k:k + 1, :], (n_batch, HALF_ST))))

    x_sub, proj_sub, xs_sub, y_sub, bn_sub = {}, {}, {}, {}, {}
    a_heads = [[] for _ in range(n_sub)]

    def project(s):
        x = x_ref[:, s * sub_t:(s + 1) * sub_t, :].reshape(SUB_ROWS, D_MODEL)
        n1 = _rms(x, gmix_ref[...]).astype(BF16)
        x_sub[s] = x
        proj_sub[s] = _dot(n1, win_ref[...])
        xs_sub[s] = _to_time_major(proj_sub[s][:, 2 * A_WIDTH:], n_batch, sub_t)
        y_sub[s] = []

    def ssm_half(s, k):
        u = _dot(xs_sub[s].astype(BF16)[:, k * HALF_CH:(k + 1) * HALF_CH], bsub_ref[k])
        ar, ai = trans[k]
        hr, hi = state[k]
        slabs = []
        for t in range(sub_t):
            ur = u[t * n_batch:(t + 1) * n_batch, :HALF_ST]
            ui = u[t * n_batch:(t + 1) * n_batch, HALF_ST:]
            hr, hi = ar * hr - ai * hi + ur, ar * hi + ai * hr + ui
            slabs.append(jnp.concatenate([hr, hi], axis=-1))
        state[k] = (hr, hi)
        y_sub[s].append(_dot(jnp.concatenate(slabs, axis=0).astype(BF16), csub_ref[k]))

    def glu(s):
        bn_tm = _glu_norm(jnp.concatenate(y_sub[s], axis=-1), xs_sub[s], dskip_ref, wglu_ref, bglu_ref,
                          gob_ref)
        bn_sub[s] = _to_batch_major(bn_tm, n_batch, sub_t).reshape(SUB_ROWS, SSM_WIDTH)

    def gate_head(h):
        hs = slice(h * A_HEAD_DIM, (h + 1) * A_HEAD_DIM)
        vn = [_rms(proj_sub[s][:, A_WIDTH + h * A_HEAD_DIM:A_WIDTH + (h + 1) * A_HEAD_DIM],
                   gv_ref[h:h + 1, :]).astype(BF16) for s in range(n_sub)]
        v_wide = jnp.concatenate(
            [jnp.concatenate([vn[s][b * sub_t:(b + 1) * sub_t, :] for s in range(n_sub)], axis=0)
             for b in range(n_batch)], axis=-1)
        s_wide = _dot(ws_ref[h], v_wide)
        for s in range(n_sub):
            ts = slice(s * sub_t, (s + 1) * sub_t)
            bias = bs_ref[ts, hs]
            gate = jnp.concatenate([s_wide[ts, b * A_HEAD_DIM:(b + 1) * A_HEAD_DIM] + bias
                                    for b in range(n_batch)], axis=0)
            a_heads[s].append(proj_sub[s][:, hs] * gate)

    def out_proj(s):
        an = _rms(jnp.concatenate(a_heads[s], axis=-1), goa_ref[...])
        mix = jnp.concatenate([an, bn_sub[s]], axis=-1).astype(BF16)
        h_buf[slot, :, s * sub_t:(s + 1) * sub_t, :] = (
            x_sub[s] + _dot(mix, wout_ref[...])).reshape(n_batch, sub_t, D_MODEL)

    chain = [project] + [functools.partial(ssm_half, k=k) for k in range(SSM_HALVES)] + [glu]
    heads = list(range(A_HEADS))
    outs = list(range(n_sub))
    for wave in range(n_sub + len(chain) - 1):
        if wave >= n_sub:
            for h in heads[:2]:
                gate_head(h)
            heads = heads[2:]
        for s in range(n_sub):
            if 0 <= wave - s < len(chain):
                chain[wave - s](s)
        if not heads and outs and wave - outs[0] >= len(chain) - 1:
            out_proj(outs.pop(0))
    for h in heads:
        gate_head(h)
    for s in outs:
        out_proj(s)

    for k in range(SSM_HALVES):
        re0, im0 = _state_lanes(k)
        hst_ref[:, re0:re0 + HALF_ST] = state[k][0]
        hst_ref[:, im0:im0 + HALF_ST] = state[k][1]

    wfi16_ref[...] = wfi32_ref[...].astype(BF16)
    wfo16_ref[...] = wfo32_ref[...].astype(BF16)
    for cp in h_copies(i, slot):
        cp.start()

    @pl.when(i == n_steps - 1)
    def _():
        stre_ref[...] = hst_ref[:, :ALL_ST]
        stim_ref[...] = hst_ref[:, ALL_ST:]
        for cp in h_copies(i - 1, 1 - slot) + h_copies(i, slot):
            cp.wait()


def _ffn_kernel(n_batch, n_t, n_steps, ns_batch, ns_t,
                h_ref, hs_ref, convs_hbm, gffn_ref, wfi_ref, cw_ref, cb_ref, wfo_ref, gfin_ref,
                y_hbm, convp_out_hbm, ys_hbm, convs_out_hbm,
                carry_scr, y_buf, sem, sem_p, carry_s_scr, ys_buf, sem_s):
    sub_t = SUB_ROWS // n_batch
    i = pl.program_id(0)
    slot = lax.rem(i, 2)
    weights = (gffn_ref, wfi_ref, cw_ref, cb_ref, wfo_ref, gfin_ref)
    n_hist = CONV_W - 1

    def y_copies(step, buf_slot):
        return [pltpu.make_async_copy(y_buf.at[buf_slot, :, b, :], y_hbm.at[b, pl.ds(step * n_t, n_t), :],
                                      sem.at[buf_slot]) for b in range(n_batch)]

    histp_out = [pltpu.make_async_copy(carry_scr.at[pl.ds(k * n_batch, n_batch), :], convp_out_hbm.at[:, k, :],
                                       sem_p.at[k]) for k in range(n_hist)]
    hist_in = [pltpu.make_async_copy(convs_hbm.at[:, k, :], carry_s_scr.at[pl.ds(k * ns_batch, ns_batch), :],
                                     sem_s.at[k]) for k in range(n_hist)]
    hist_out = [pltpu.make_async_copy(carry_s_scr.at[pl.ds(k * ns_batch, ns_batch), :], convs_out_hbm.at[:, k, :],
                                      sem_s.at[k]) for k in range(n_hist)]
    ys_out = [pltpu.make_async_copy(ys_buf.at[pl.ds(t * ns_batch, ns_batch), :], ys_hbm.at[:, t, :],
                                    sem_s.at[n_hist + t]) for t in range(ns_t)]

    @pl.when(i == 0)
    def _():
        carry_scr[...] = jnp.zeros(carry_scr.shape, F32)
        for cp in hist_in:
            cp.start()

    @pl.when(jnp.logical_and(i >= 2, i < n_steps))
    def _():
        for cp in y_copies(i - 2, slot):
            cp.wait()

    @pl.when(i < n_steps)
    def _():
        def get_h(s):
            return h_ref[s * sub_t:(s + 1) * sub_t].reshape(SUB_ROWS, D_MODEL)

        def put_y(s, y):
            y_buf[slot, s * sub_t:(s + 1) * sub_t] = y.reshape(sub_t, n_batch, D_MODEL)

        _ffn_subblocks(get_h, put_y, n_batch, n_t // sub_t, carry_scr, *weights)
        for cp in y_copies(i, slot):
            cp.start()

    @pl.when(i == n_steps - 1)
    def _():
        for cp in histp_out:
            cp.start()
        for cp in y_copies(i - 1, 1 - slot) + y_copies(i, slot) + histp_out:
            cp.wait()

    @pl.when(i == n_steps)
    def _():
        for cp in hist_in:
            cp.wait()

        def get_h(s):
            return hs_ref[s * SUB_ROWS:(s + 1) * SUB_ROWS, :]

        def put_y(s, y):
            ys_buf[s * SUB_ROWS:(s + 1) * SUB_ROWS, :] = y

        _ffn_subblocks(get_h, put_y, ns_batch, ns_t * ns_batch // SUB_ROWS, carry_s_scr, *weights)
        for cp in hist_out + ys_out:
            cp.start()
        for cp in hist_out + ys_out:
            cp.wait()


def _scan_inplace(u_scr, hst_ref, ar_ref, ai_ref, k, n_batch, n_t):
    re0, im0 = _state_lanes(k)

    def row_chunk(rc, _):
        r0 = pl.multiple_of(rc * SUBLANES, SUBLANES)
        for c in range(HALF_ST // SCAN_LANES):
            lo = c * SCAN_LANES
            re_l = slice(lo, lo + SCAN_LANES)
            im_l = slice(HALF_ST + lo, HALF_ST + lo + SCAN_LANES)
            ar = jnp.broadcast_to(ar_ref[k:k + 1, re_l], (SUBLANES, SCAN_LANES))
            ai = jnp.broadcast_to(ai_ref[k:k + 1, re_l], (SUBLANES, SCAN_LANES))
            hr0 = hst_ref[pl.ds(r0, SUBLANES), re0 + lo:re0 + lo + SCAN_LANES]
            hi0 = hst_ref[pl.ds(r0, SUBLANES), im0 + lo:im0 + lo + SCAN_LANES]

            def step(t, carry):
                hr, hi = carry
                row = pl.multiple_of(t * n_batch + r0, SUBLANES)
                ur = u_scr[pl.ds(row, SUBLANES), re_l]
                ui = u_scr[pl.ds(row, SUBLANES), im_l]
                nhr = ar * hr - ai * hi + ur
                nhi = ar * hi + ai * hr + ui
                u_scr[pl.ds(row, SUBLANES), re_l] = nhr
                u_scr[pl.ds(row, SUBLANES), im_l] = nhi
                return nhr, nhi

            hr, hi = lax.fori_loop(0, n_t, step, (hr0, hi0), unroll=min(n_t, 4))
            hst_ref[pl.ds(r0, SUBLANES), re0 + lo:re0 + lo + SCAN_LANES] = hr
            hst_ref[pl.ds(r0, SUBLANES), im0 + lo:im0 + lo + SCAN_LANES] = hi
        return 0

    lax.fori_loop(0, n_batch // SUBLANES, row_chunk, 0)


def _mixer_sample_kernel(n_batch, n_t,
                         x_ref, h0re_ref, h0im_ref, gmix_ref, win_ref, gv_ref, ws_ref, bs_ref,
                         ar_ref, ai_ref, bsub_ref, csub_ref, dskip_ref,
                         wglu_ref, bglu_ref, goa_ref, gob_ref, wout_ref,
                         hout_ref, stre_ref, stim_ref, v_hbm, x_buf, v_buf, u_scr, hst_ref, sem):
    x_in = [pltpu.make_async_copy(x_ref.at[:, t, :], x_buf.at[pl.ds(t * n_batch, n_batch), :], sem.at[t])
            for t in range(n_t)]
    v_out = [pltpu.make_async_copy(
        v_buf.at[pl.ds(t * n_batch, n_batch), pl.ds(h * A_HEAD_DIM, A_HEAD_DIM)],
        v_hbm.at[:, t, h, :], sem.at[n_t + t * A_HEADS + h]) for t in range(n_t) for h in range(A_HEADS)]
    for cp in x_in:
        cp.start()
    hst_ref[:, :ALL_ST] = h0re_ref[...]
    hst_ref[:, ALL_ST:] = h0im_ref[...]
    for cp in x_in:
        cp.wait()
    x = x_buf[...]
    n1 = _rms(x, gmix_ref[...]).astype(BF16)
    proj = _dot(n1, win_ref[...])

    a_heads = []
    for h in range(A_HEADS):
        hs = slice(h * A_HEAD_DIM, (h + 1) * A_HEAD_DIM)
        vn = _rms(proj[:, A_WIDTH + h * A_HEAD_DIM:A_WIDTH + (h + 1) * A_HEAD_DIM], gv_ref[h:h + 1, :])
        v_buf[:, hs] = vn
        slabs = []
        for t in range(n_t):
            s = jnp.broadcast_to(bs_ref[h * n_t + t:h * n_t + t + 1, :], (n_batch, A_HEAD_DIM))
            for src in range(t + 1):
                r = (h * n_t + t) * n_t + src
                s = s + ws_ref[r:r + 1, :] * vn[src * n_batch:(src + 1) * n_batch, :]
            slabs.append(proj[t * n_batch:(t + 1) * n_batch, hs] * s)
        a_heads.append(jnp.concatenate(slabs, axis=0))
    an = _rms(jnp.concatenate(a_heads, axis=-1), goa_ref[...])
    for cp in v_out:
        cp.start()

    xs = proj[:, 2 * A_WIDTH:]
    xs_b = xs.astype(BF16)
    y_parts = []
    for k in range(SSM_HALVES):
        u_scr[...] = _dot(xs_b[:, k * HALF_CH:(k + 1) * HALF_CH], bsub_ref[k])
        _scan_inplace(u_scr, hst_ref, ar_ref, ai_ref, k, n_batch, n_t)
        y_parts.append(_dot(u_scr[...].astype(BF16), csub_ref[k]))
    bn = _glu_norm(jnp.concatenate(y_parts, axis=-1), xs, dskip_ref, wglu_ref, bglu_ref, gob_ref)

    mix = jnp.concatenate([an, bn], axis=-1).astype(BF16)
    hout_ref[...] = x + _dot(mix, wout_ref[...])
    stre_ref[...] = hst_ref[:, :ALL_ST]
    stim_ref[...] = hst_ref[:, ALL_ST:]
    for cp in v_out:
        cp.wait()


def _const_spec(shape):
    zeros = (0,) * len(shape)
    return pl.BlockSpec(shape, lambda i: zeros, pipeline_mode=pl.Buffered(1))


def _params():
    return pltpu.CompilerParams(dimension_semantics=("arbitrary",), vmem_limit_bytes=VMEM_LIMIT_BYTES)


def _mixer_consts(p, ws, bs):
    return [p['g_mix'], p['w_in'], p['g_v'], ws, bs, p['ar'], p['ai'],
            p['bsub'], p['csub'], p['d_skip'], p['w_glu'], p['b_glu'],
            p['g_out_a'], p['g_out_b'], p['w_out']]


def _ffn_consts(p, wfi, wfo):
    return [p['g_ffn'], wfi, p['conv_w'], p['conv_b'], wfo, p['g_final']]


def _mixer_prompt_call(x, p, ws, bs, w_ffn_in, w_ffn_out):
    n_batch, n_l, _ = x.shape
    n_steps = n_l // CHUNK
    assert n_steps >= 2
    blk = pl.BlockSpec((n_batch, CHUNK, D_MODEL), lambda i: (0, i, 0))
    st_blk = pl.BlockSpec((n_batch, ALL_ST), lambda i: (0, 0))
    wfi_blk = pl.BlockSpec((D_MODEL // n_steps, 2 * D_FF), lambda i: (i, 0))
    wfo_blk = pl.BlockSpec((D_FF // n_steps, D_MODEL), lambda i: (i, 0))
    consts = _mixer_consts(p, ws, bs)
    return pl.pallas_call(
        functools.partial(_mixer_prompt_kernel, n_batch, n_steps),
        grid=(n_steps,),
        in_specs=[blk] + [_const_spec(c.shape) for c in consts] + [wfi_blk, wfo_blk],
        out_specs=[pl.BlockSpec(memory_space=pl.ANY), st_blk, st_blk, wfi_blk, wfo_blk],
        out_shape=[jax.ShapeDtypeStruct((n_l, n_batch, D_MODEL), F32),
                   jax.ShapeDtypeStruct((n_batch, ALL_ST), F32),
                   jax.ShapeDtypeStruct((n_batch, ALL_ST), F32),
                   jax.ShapeDtypeStruct(w_ffn_in.shape, BF16),
                   jax.ShapeDtypeStruct(w_ffn_out.shape, BF16)],
        scratch_shapes=[pltpu.VMEM((n_batch, STATE_W), F32),
                        pltpu.VMEM((2, n_batch, CHUNK, D_MODEL), F32),
                        pltpu.SemaphoreType.DMA((2,))],
        compiler_params=_params(),
        name="mixer_prompt",
    )(x, *consts, w_ffn_in, w_ffn_out)


def _ffn_call(h_tm, hs_tm, conv_s, p, wfi, wfo):
    n_l, n_batch, _ = h_tm.shape
    ns_batch, n_hist, _ = conv_s.shape
    rows_s = hs_tm.shape[0]
    ns_t = rows_s // ns_batch
    n_t = FFN_PROMPT_STEPS
    n_steps = n_l // n_t
    assert n_steps >= 2
    blk = (n_t, n_batch, D_MODEL)
    weights = _ffn_consts(p, wfi, wfo)
    return pl.pallas_call(
        functools.partial(_ffn_kernel, n_batch, n_t, n_steps, ns_batch, ns_t),
        grid=(n_steps + 1,),
        in_specs=[pl.BlockSpec(blk, lambda i: (jnp.minimum(i, n_steps - 1), 0, 0)),
                  _const_spec(hs_tm.shape), pl.BlockSpec(memory_space=pl.ANY)]
        + [_const_spec(c.shape) for c in weights],
        out_specs=[pl.BlockSpec(memory_space=pl.ANY)] * 4,
        out_shape=[jax.ShapeDtypeStruct((n_batch, n_l, D_MODEL), F32),
                   jax.ShapeDtypeStruct((n_batch, n_hist, 2 * D_FF), F32),
                   jax.ShapeDtypeStruct((ns_batch, ns_t, D_MODEL), F32),
                   jax.ShapeDtypeStruct(conv_s.shape, F32)],
        scratch_shapes=[pltpu.VMEM((n_hist * n_batch, 2 * D_FF), F32),
                        pltpu.VMEM((2,) + blk, F32),
                        pltpu.SemaphoreType.DMA((2,)),
                        pltpu.SemaphoreType.DMA((n_hist,)),
                        pltpu.VMEM((n_hist * ns_batch, 2 * D_FF), F32),
                        pltpu.VMEM((rows_s, D_MODEL), F32),
                        pltpu.SemaphoreType.DMA((n_hist + ns_t,))],
        compiler_params=_params(),
        name="ffn",
    )(h_tm, hs_tm, conv_s, *weights)


def _mixer_sample_call(x, h0_re, h0_im, p, ws, bs):
    n_batch, n_t, _ = x.shape
    rows = n_batch * n_t
    consts = [h0_re, h0_im] + _mixer_consts(p, ws, bs)
    st_blk = pl.BlockSpec((n_batch, ALL_ST), lambda i: (0, 0))
    return pl.pallas_call(
        functools.partial(_mixer_sample_kernel, n_batch, n_t),
        grid=(1,),
        in_specs=[pl.BlockSpec(memory_space=pl.ANY)] + [_const_spec(c.shape) for c in consts],
        out_specs=[pl.BlockSpec((rows, D_MODEL), lambda i: (0, 0)), st_blk, st_blk,
                   pl.BlockSpec(memory_space=pl.ANY)],
        out_shape=[jax.ShapeDtypeStruct((rows, D_MODEL), F32),
                   jax.ShapeDtypeStruct((n_batch, ALL_ST), F32),
                   jax.ShapeDtypeStruct((n_batch, ALL_ST), F32),
                   jax.ShapeDtypeStruct((n_batch, n_t, A_HEADS, A_HEAD_DIM), F32)],
        scratch_shapes=[pltpu.VMEM((rows, D_MODEL), F32),
                        pltpu.VMEM((rows, A_WIDTH), F32),
                        pltpu.VMEM((rows, 2 * HALF_ST), F32),
                        pltpu.VMEM((n_batch, STATE_W), F32),
                        pltpu.SemaphoreType.DMA((n_t + n_t * A_HEADS,))],
        compiler_params=_params(),
        name="mixer_sample",
    )(x, *consts)


def _prep_params(g_mix, w_in, g_v, lam_re, lam_im, log_dt, b_re, b_im, c_re, c_im,
                 d_skip, w_glu, b_glu, g_out_a, g_out_b, w_out, g_ffn, conv_w, conv_b, g_final):
    lr = lam_re.astype(F32)
    li = lam_im.astype(F32)
    dt = jnp.exp(log_dt.astype(F32))[:, None]
    mag = jnp.exp(lr * dt)
    ar = mag * jnp.cos(li * dt)
    ai = mag * jnp.sin(li * dt)
    den = lr * lr + li * li
    fr = ((ar - 1.0) * lr + ai * li) / den
    fi = (ai * lr - (ar - 1.0) * li) / den
    bre = b_re.astype(F32)
    bim = b_im.astype(F32)
    bbr = fr[..., None] * bre - fi[..., None] * bim
    bbi = fr[..., None] * bim + fi[..., None] * bre

    def per_half_t(m):
        m = m.astype(F32).reshape(SSM_HALVES, HALF_GROUPS, m.shape[1], m.shape[2])
        return jnp.transpose(m, (0, 1, 3, 2))

    def blockdiag(shape, q_axis, g_axis, g2_axis, m_q0, m_q1):
        q = lax.broadcasted_iota(jnp.int32, shape, q_axis)
        same = lax.broadcasted_iota(jnp.int32, shape, g_axis) == lax.broadcasted_iota(jnp.int32, shape, g2_axis)
        return jnp.where(same, jnp.where(q == 0, m_q0, m_q1), 0.0).astype(BF16)

    bsub = blockdiag((SSM_HALVES, HALF_GROUPS, SSM_GROUP, 2, HALF_GROUPS, SSM_STATE), 3, 1, 4,
                     per_half_t(bbr)[:, :, :, None, None, :], per_half_t(bbi)[:, :, :, None, None, :])
    csub = blockdiag((SSM_HALVES, 2, HALF_GROUPS, SSM_STATE, HALF_GROUPS, SSM_GROUP), 1, 4, 2,
                     per_half_t(c_re)[:, None, :, :, None, :], -per_half_t(c_im)[:, None, :, :, None, :])

    return dict(
        g_mix=g_mix.reshape(1, D_MODEL), w_in=w_in.astype(BF16), g_v=g_v,
        ar=ar.reshape(SSM_HALVES, HALF_ST), ai=ai.reshape(SSM_HALVES, HALF_ST),
        bsub=bsub.reshape(SSM_HALVES, HALF_CH, 2 * HALF_ST), csub=csub.reshape(SSM_HALVES, 2 * HALF_ST, HALF_CH),
        d_skip=d_skip.reshape(1, SSM_WIDTH), w_glu=w_glu.astype(BF16), b_glu=b_glu.reshape(1, SSM_WIDTH),
        g_out_a=g_out_a.reshape(1, A_WIDTH), g_out_b=g_out_b.reshape(1, SSM_WIDTH),
        w_out=w_out.astype(BF16), g_ffn=g_ffn.reshape(1, D_MODEL),
        conv_w=conv_w, conv_b=conv_b.reshape(1, 2 * D_FF), g_final=g_final.reshape(1, D_MODEL))


def _state_rows(s):
    return s.reshape(s.shape[0], ALL_ST)


def _state_groups(s):
    return s.reshape(s.shape[0], SSM_GROUPS, SSM_STATE)


def kernel(x_prompt, x_sample, state_ssm_re, state_ssm_im, state_conv, g_mix, w_in, g_v, w_s, b_s,
           lam_re, lam_im, log_dt, b_re, b_im, c_re, c_im, d_skip, w_glu, b_glu, g_out_a, g_out_b,
           w_out, g_ffn, w_ffn_in, conv_w, conv_b, w_ffn_out, g_final):
    p = _prep_params(g_mix, w_in, g_v, lam_re, lam_im, log_dt, b_re, b_im, c_re, c_im, d_skip,
                     w_glu, b_glu, g_out_a, g_out_b, w_out, g_ffn, conv_w, conv_b, g_final)
    tril = jnp.tril(w_s.astype(F32))

    bs_p = jnp.broadcast_to(jnp.transpose(b_s)[:, :, None], (CHUNK, A_HEADS, A_HEAD_DIM)).reshape(CHUNK, A_WIDTH)
    h_p, re_p, im_p, wfi, wfo = _mixer_prompt_call(x_prompt, p, tril.astype(BF16), bs_p, w_ffn_in, w_ffn_out)

    nls = x_sample.shape[1]
    ws_s = jnp.broadcast_to(tril[:, :nls, :nls].reshape(A_HEADS * nls * nls, 1), (A_HEADS * nls * nls, A_HEAD_DIM))
    bs_s = jnp.broadcast_to(b_s[:, :nls].reshape(A_HEADS * nls, 1), (A_HEADS * nls, A_HEAD_DIM))
    h_s, re_s, im_s, v_s = _mixer_sample_call(x_sample, _state_rows(state_ssm_re), _state_rows(state_ssm_im),
                                              p, ws_s, bs_s)

    y_p, conv_p, y_s, conv_s = _ffn_call(h_p, h_s, state_conv, p, wfi, wfo)
    return (y_p, y_s, v_s, _state_groups(re_p), _state_groups(im_p), conv_p,
            _state_groups(re_s), _state_groups(im_s), conv_s)
```

```python
import functools
import math

import jax
import jax.numpy as jnp
from jax import lax
from jax.experimental import pallas as pl
from jax.experimental.pallas import tpu as pltpu

D_MODEL = 1024
A_HEADS = 4
A_HEAD_DIM = 128
A_WIDTH = 512
CHUNK = 128
SSM_WIDTH = 512
SSM_GROUP = 16
SSM_GROUPS = 32
SSM_STATE = 64
D_FF = 2816
CONV_W = 3
EPS = 1e-6

SSM_HALVES = 2
HALF_GROUPS = SSM_GROUPS // SSM_HALVES
HALF_CH = HALF_GROUPS * SSM_GROUP
HALF_ST = HALF_GROUPS * SSM_STATE
ALL_ST = SSM_HALVES * HALF_ST
STATE_W = 2 * ALL_ST

FF_CHUNK = 256
FF_NCHUNK = D_FF // FF_CHUNK
SUB_ROWS = 256
FFN_PROMPT_STEPS = 64

SUBLANES = 8
SCAN_LANES = 512
VMEM_LIMIT_BYTES = 60 * 1024 * 1024

BF16 = jnp.bfloat16
F32 = jnp.float32


def _state_lanes(k):
    return k * HALF_ST, ALL_ST + k * HALF_ST


def _rms(x, g):
    ms = jnp.mean(x * x, axis=-1, keepdims=True)
    return x * lax.rsqrt(ms + EPS) * g


def _gelu(x):
    c = math.sqrt(2.0 / math.pi)
    return x * (0.5 * (1.0 + jnp.tanh(c * (x + 0.044715 * (x * x * x)))))


def _dot(a, b):
    return jnp.dot(a, b, preferred_element_type=F32)


def _to_time_major(x, n_batch, n_t):
    w = x.shape[-1]
    return jnp.swapaxes(x.reshape(n_batch, n_t, w), 0, 1).reshape(n_t * n_batch, w)


def _to_batch_major(x, n_batch, n_t):
    w = x.shape[-1]
    return jnp.swapaxes(x.reshape(n_t, n_batch, w), 0, 1)


def _glu_norm(ys, xs, dskip_ref, wglu_ref, bglu_ref, gob_ref):
    g = _gelu(ys + dskip_ref[...] * xs)
    z = _dot(g.astype(BF16), wglu_ref[...]) + bglu_ref[...]
    return _rms(g * jax.nn.sigmoid(z), gob_ref[...])


def _ffn_subblocks(get_h, put_y, n_batch, n_sub, carry_scr,
                   gffn_ref, wfi_ref, cw_ref, cb_ref, wfo_ref, gfin_ref):
    halo = (CONV_W - 1) * n_batch
    hs, n2s = {}, {}

    def norm_in(s):
        hs[s] = get_h(s)
        n2s[s] = _rms(hs[s], gffn_ref[...]).astype(BF16)

    def col_slices(j):
        return [slice(c0, c0 + FF_CHUNK) for c0 in (j * FF_CHUNK, D_FF + j * FF_CHUNK)]

    def up_proj(j, s):
        return [jnp.dot(n2s[s], wfi_ref[:, cs], preferred_element_type=F32) for cs in col_slices(j)]

    order = [(j, s) for j in range(FF_NCHUNK) for s in range(n_sub)]
    accs = [jnp.zeros((SUB_ROWS, D_MODEL), F32) for _ in range(n_sub)]
    tails = [None, None]
    norm_in(0)
    ups_next = up_proj(*order[0])
    for s in range(1, n_sub):
        norm_in(s)
    for idx, (j, s) in enumerate(order):
        ups = ups_next
        if idx + 1 < len(order):
            ups_next = up_proj(*order[idx + 1])
        conv = []
        for ci, (cs, up) in enumerate(zip(col_slices(j), ups)):
            tail = carry_scr[:, cs] if s == 0 else tails[ci]
            padded = jnp.concatenate([tail, up], axis=0)
            tails[ci] = padded[SUB_ROWS:SUB_ROWS + halo, :]
            if s == n_sub - 1:
                carry_scr[:, cs] = tails[ci]
            out = cb_ref[:, cs]
            for tap in range(CONV_W):
                out = out + cw_ref[tap:tap + 1, cs] * padded[tap * n_batch:tap * n_batch + SUB_ROWS, :]
            conv.append(out)
        act = _gelu(conv[0]) * conv[1]
        accs[s] = accs[s] + jnp.dot(act.astype(BF16), wfo_ref[j * FF_CHUNK:(j + 1) * FF_CHUNK, :],
                                    preferred_element_type=F32)
        if j == FF_NCHUNK - 1:
            put_y(s, _rms(hs[s] + accs[s], gfin_ref[...]))


def _mixer_prompt_kernel(n_batch, n_steps,
                         x_ref, gmix_ref, win_ref, gv_ref, ws_ref, bs_ref,
                         ar_ref, ai_ref, bsub_ref, csub_ref, dskip_ref,
                         wglu_ref, bglu_ref, goa_ref, gob_ref, wout_ref, wfi32_ref, wfo32_ref,
                         hout_hbm, st_ref, wfi16_ref, wfo16_ref, hst_ref, h_buf, sem):
    sub_t = SUB_ROWS // n_batch
    n_sub = CHUNK // sub_t
    i = pl.program_id(0)
    slot = lax.rem(i, 2)

    def h_copies(step, buf_slot):
        return [pltpu.make_async_copy(h_buf.at[buf_slot, b], hout_hbm.at[pl.ds(step * CHUNK, CHUNK), b, :],
                                      sem.at[buf_slot]) for b in range(n_batch)]

    @pl.when(i == 0)
    def _():
        hst_ref[...] = jnp.zeros(hst_ref.shape, F32)

    @pl.when(i >= 2)
    def _():
        for cp in h_copies(i - 2, slot):
            cp.wait()

    state, trans = [], []
    for k in range(SSM_HALVES):
        re0, im0 = _state_lanes(k)
        state.append((hst_ref[:, re0:re0 + HALF_ST], hst_ref[:, im0:im0 + HALF_ST]))
        trans.append((jnp.broadcast_to(ar_ref[k:k + 1, :], (n_batch, HALF_ST)),
                      jnp.broadcast_to(ai_ref[k:k + 1, :], (n_batch, HALF_ST))))

    x_sub, proj_sub, xs_sub, y_sub, bn_sub = {}, {}, {}, {}, {}
    a_heads = [[] for _ in range(n_sub)]

    def project(s):
        x = x_ref[:, s * sub_t:(s + 1) * sub_t, :].reshape(SUB_ROWS, D_MODEL)
        n1 = _rms(x, gmix_ref[...]).astype(BF16)
        x_sub[s] = x
        proj_sub[s] = _dot(n1, win_ref[...])
        xs_sub[s] = _to_time_major(proj_sub[s][:, 2 * A_WIDTH:], n_batch, sub_t)
        y_sub[s] = []

    def ssm_half(s, k):
        u = _dot(xs_sub[s].astype(BF16)[:, k * HALF_CH:(k + 1) * HALF_CH], bsub_ref[k])
        ar, ai = trans[k]
        hr, hi = state[k]
        slabs = []
        for t in range(sub_t):
            ur = u[t * n_batch:(t + 1) * n_batch, :HALF_ST]
            ui = u[t * n_batch:(t + 1) * n_batch, HALF_ST:]
            hr, hi = ar * hr - ai * hi + ur, ar * hi + ai * hr + ui
            slabs.append(jnp.concatenate([hr, hi], axis=-1))
        state[k] = (hr, hi)
        y_sub[s].append(_dot(jnp.concatenate(slabs, axis=0).astype(BF16), csub_ref[k]))

    def glu(s):
        bn_tm = _glu_norm(jnp.concatenate(y_sub[s], axis=-1), xs_sub[s], dskip_ref, wglu_ref, bglu_ref,
                          gob_ref)
        bn_sub[s] = _to_batch_major(bn_tm, n_batch, sub_t).reshape(SUB_ROWS, SSM_WIDTH)

    def gate_head(h):
        hs = slice(h * A_HEAD_DIM, (h + 1) * A_HEAD_DIM)
        vn = [_rms(proj_sub[s][:, A_WIDTH + h * A_HEAD_DIM:A_WIDTH + (h + 1) * A_HEAD_DIM],
                   gv_ref[h:h + 1, :]).astype(BF16) for s in range(n_sub)]
        v_wide = jnp.concatenate(
            [jnp.concatenate([vn[s][b * sub_t:(b + 1) * sub_t, :] for s in range(n_sub)], axis=0)
             for b in range(n_batch)], axis=-1)
        s_wide = _dot(ws_ref[h], v_wide)
        for s in range(n_sub):
            ts = slice(s * sub_t, (s + 1) * sub_t)
            bias = bs_ref[ts, hs]
            gate = jnp.concatenate([s_wide[ts, b * A_HEAD_DIM:(b + 1) * A_HEAD_DIM] + bias
                                    for b in range(n_batch)], axis=0)
            a_heads[s].append(proj_sub[s][:, hs] * gate)

    def out_proj(s):
        an = _rms(jnp.concatenate(a_heads[s], axis=-1), goa_ref[...])
        mix = jnp.concatenate([an, bn_sub[s]], axis=-1).astype(BF16)
        h_buf[slot, :, s * sub_t:(s + 1) * sub_t, :] = (
            x_sub[s] + _dot(mix, wout_ref[...])).reshape(n_batch, sub_t, D_MODEL)

    chain = [project] + [functools.partial(ssm_half, k=k) for k in range(SSM_HALVES)] + [glu]
    heads = list(range(A_HEADS))
    outs = list(range(n_sub))
    for wave in range(n_sub + len(chain) - 1):
        if wave >= n_sub:
            for h in heads[:2]:
                gate_head(h)
            heads = heads[2:]
        for s in range(n_sub):
            if 0 <= wave - s < len(chain):
                chain[wave - s](s)
        if not heads and outs and wave - outs[0] >= len(chain) - 1:
            out_proj(outs.pop(0))
    for h in heads:
        gate_head(h)
    for s in outs:
        out_proj(s)

    for k in range(SSM_HALVES):
        re0, im0 = _state_lanes(k)
        hst_ref[:, re0:re0 + HALF_ST] = state[k][0]
        hst_ref[:, im0:im0 + HALF_ST] = state[k][1]

    wfi16_ref[...] = wfi32_ref[...].astype(BF16)
    wfo16_ref[...] = wfo32_ref[...].astype(BF16)
    for cp in h_copies(i, slot):
        cp.start()

    @pl.when(i == n_steps - 1)
    def _():
        st_ref[...] = hst_ref[...]
        for cp in h_copies(i - 1, 1 - slot) + h_copies(i, slot):
            cp.wait()


def _ffn_kernel(n_batch, n_t, n_steps, ns_batch, ns_t,
                h_ref, hs_ref, convs_hbm, gffn_ref, wfi_ref, cw_ref, cb_ref, wfo_ref, gfin_ref,
                y_hbm, convp_out_hbm, ys_hbm, convs_out_hbm,
                carry_scr, y_buf, sem, sem_p, carry_s_scr, ys_buf, sem_s):
    sub_t = SUB_ROWS // n_batch
    i = pl.program_id(0)
    slot = lax.rem(i, 2)
    weights = (gffn_ref, wfi_ref, cw_ref, cb_ref, wfo_ref, gfin_ref)
    n_hist = CONV_W - 1

    def y_copies(step, buf_slot):
        return [pltpu.make_async_copy(y_buf.at[buf_slot, :, b, :], y_hbm.at[b, pl.ds(step * n_t, n_t), :],
                                      sem.at[buf_slot]) for b in range(n_batch)]

    histp_out = [pltpu.make_async_copy(carry_scr.at[pl.ds(k * n_batch, n_batch), :], convp_out_hbm.at[:, k, :],
                                       sem_p.at[k]) for k in range(n_hist)]
    hist_in = [pltpu.make_async_copy(convs_hbm.at[:, k, :], carry_s_scr.at[pl.ds(k * ns_batch, ns_batch), :],
                                     sem_s.at[k]) for k in range(n_hist)]
    hist_out = [pltpu.make_async_copy(carry_s_scr.at[pl.ds(k * ns_batch, ns_batch), :], convs_out_hbm.at[:, k, :],
                                      sem_s.at[k]) for k in range(n_hist)]
    ys_out = [pltpu.make_async_copy(ys_buf.at[pl.ds(t * ns_batch, ns_batch), :], ys_hbm.at[:, t, :],
                                    sem_s.at[n_hist + t]) for t in range(ns_t)]

    @pl.when(i == 0)
    def _():
        carry_scr[...] = jnp.zeros(carry_scr.shape, F32)
        for cp in hist_in:
            cp.start()

    @pl.when(jnp.logical_and(i >= 2, i < n_steps))
    def _():
        for cp in y_copies(i - 2, slot):
            cp.wait()

    @pl.when(i < n_steps)
    def _():
        def get_h(s):
            return h_ref[s * sub_t:(s + 1) * sub_t].reshape(SUB_ROWS, D_MODEL)

        def put_y(s, y):
            y_buf[slot, s * sub_t:(s + 1) * sub_t] = y.reshape(sub_t, n_batch, D_MODEL)

        _ffn_subblocks(get_h, put_y, n_batch, n_t // sub_t, carry_scr, *weights)
        for cp in y_copies(i, slot):
            cp.start()

    @pl.when(i == n_steps - 1)
    def _():
        for cp in histp_out:
            cp.start()
        for cp in y_copies(i - 1, 1 - slot) + y_copies(i, slot) + histp_out:
            cp.wait()

    @pl.when(i == n_steps)
    def _():
        for cp in hist_in:
            cp.wait()

        def get_h(s):
            return hs_ref[s * SUB_ROWS:(s + 1) * SUB_ROWS, :]

        def put_y(s, y):
            ys_buf[s * SUB_ROWS:(s + 1) * SUB_ROWS, :] = y

        _ffn_subblocks(get_h, put_y, ns_batch, ns_t * ns_batch // SUB_ROWS, carry_s_scr, *weights)
        for cp in hist_out + ys_out:
            cp.start()
        for cp in hist_out + ys_out:
            cp.wait()


def _scan_inplace(u_scr, hst_ref, ar_ref, ai_ref, k, n_batch, n_t):
    re0, im0 = _state_lanes(k)

    def row_chunk(rc, _):
        r0 = pl.multiple_of(rc * SUBLANES, SUBLANES)
        for c in range(HALF_ST // SCAN_LANES):
            lo = c * SCAN_LANES
            re_l = slice(lo, lo + SCAN_LANES)
            im_l = slice(HALF_ST + lo, HALF_ST + lo + SCAN_LANES)
            ar = jnp.broadcast_to(ar_ref[k:k + 1, re_l], (SUBLANES, SCAN_LANES))
            ai = jnp.broadcast_to(ai_ref[k:k + 1, re_l], (SUBLANES, SCAN_LANES))
            hr0 = hst_ref[pl.ds(r0, SUBLANES), re0 + lo:re0 + lo + SCAN_LANES]
            hi0 = hst_ref[pl.ds(r0, SUBLANES), im0 + lo:im0 + lo + SCAN_LANES]

            def step(t, carry):
                hr, hi = carry
                row = pl.multiple_of(t * n_batch + r0, SUBLANES)
                ur = u_scr[pl.ds(row, SUBLANES), re_l]
                ui = u_scr[pl.ds(row, SUBLANES), im_l]
                nhr = ar * hr - ai * hi + ur
                nhi = ar * hi + ai * hr + ui
                u_scr[pl.ds(row, SUBLANES), re_l] = nhr
                u_scr[pl.ds(row, SUBLANES), im_l] = nhi
                return nhr, nhi

            hr, hi = lax.fori_loop(0, n_t, step, (hr0, hi0), unroll=min(n_t, 4))
            hst_ref[pl.ds(r0, SUBLANES), re0 + lo:re0 + lo + SCAN_LANES] = hr
            hst_ref[pl.ds(r0, SUBLANES), im0 + lo:im0 + lo + SCAN_LANES] = hi
        return 0

    lax.fori_loop(0, n_batch // SUBLANES, row_chunk, 0)


def _mixer_sample_kernel(n_batch, n_t,
                         x_ref, h0_ref, gmix_ref, win_ref, gv_ref, ws_ref, bs_ref,
                         ar_ref, ai_ref, bsub_ref, csub_ref, dskip_ref,
                         wglu_ref, bglu_ref, goa_ref, gob_ref, wout_ref,
                         hout_ref, st_ref, v_hbm, x_buf, v_buf, u_scr, hst_ref, sem):
    x_in = [pltpu.make_async_copy(x_ref.at[:, t, :], x_buf.at[pl.ds(t * n_batch, n_batch), :], sem.at[t])
            for t in range(n_t)]
    v_out = [pltpu.make_async_copy(
        v_buf.at[pl.ds(t * n_batch, n_batch), pl.ds(h * A_HEAD_DIM, A_HEAD_DIM)],
        v_hbm.at[:, t, h, :], sem.at[n_t + t * A_HEADS + h]) for t in range(n_t) for h in range(A_HEADS)]
    for cp in x_in:
        cp.start()
    hst_ref[...] = h0_ref[...]
    for cp in x_in:
        cp.wait()
    x = x_buf[...]
    n1 = _rms(x, gmix_ref[...]).astype(BF16)
    proj = _dot(n1, win_ref[...])

    a_heads = []
    for h in range(A_HEADS):
        hs = slice(h * A_HEAD_DIM, (h + 1) * A_HEAD_DIM)
        vn = _rms(proj[:, A_WIDTH + h * A_HEAD_DIM:A_WIDTH + (h + 1) * A_HEAD_DIM], gv_ref[h:h + 1, :])
        v_buf[:, hs] = vn
        slabs = []
        for t in range(n_t):
            s = jnp.broadcast_to(bs_ref[h * n_t + t:h * n_t + t + 1, :], (n_batch, A_HEAD_DIM))
            for src in range(t + 1):
                r = (h * n_t + t) * n_t + src
                s = s + ws_ref[r:r + 1, :] * vn[src * n_batch:(src + 1) * n_batch, :]
            slabs.append(proj[t * n_batch:(t + 1) * n_batch, hs] * s)
        a_heads.append(jnp.concatenate(slabs, axis=0))
    an = _rms(jnp.concatenate(a_heads, axis=-1), goa_ref[...])
    for cp in v_out:
        cp.start()

    xs = proj[:, 2 * A_WIDTH:]
    xs_b = xs.astype(BF16)
    y_parts = []
    for k in range(SSM_HALVES):
        u_scr[...] = _dot(xs_b[:, k * HALF_CH:(k + 1) * HALF_CH], bsub_ref[k])
        _scan_inplace(u_scr, hst_ref, ar_ref, ai_ref, k, n_batch, n_t)
        y_parts.append(_dot(u_scr[...].astype(BF16), csub_ref[k]))
    bn = _glu_norm(jnp.concatenate(y_parts, axis=-1), xs, dskip_ref, wglu_ref, bglu_ref, gob_ref)

    mix = jnp.concatenate([an, bn], axis=-1).astype(BF16)
    hout_ref[...] = x + _dot(mix, wout_ref[...])
    st_ref[...] = hst_ref[...]
    for cp in v_out:
        cp.wait()


def _const_spec(shape):
    zeros = (0,) * len(shape)
    return pl.BlockSpec(shape, lambda i: zeros, pipeline_mode=pl.Buffered(1))


def _params():
    return pltpu.CompilerParams(dimension_semantics=("arbitrary",), vmem_limit_bytes=VMEM_LIMIT_BYTES)


def _mixer_consts(p, ws, bs):
    return [p['g_mix'], p['w_in'], p['g_v'], ws, bs, p['ar'], p['ai'],
            p['bsub'], p['csub'], p['d_skip'], p['w_glu'], p['b_glu'],
            p['g_out_a'], p['g_out_b'], p['w_out']]


def _ffn_consts(p, wfi, wfo):
    return [p['g_ffn'], wfi, p['conv_w'], p['conv_b'], wfo, p['g_final']]


def _mixer_prompt_call(x, p, ws, bs, w_ffn_in, w_ffn_out):
    n_batch, n_l, _ = x.shape
    n_steps = n_l // CHUNK
    assert n_steps >= 2
    blk = pl.BlockSpec((n_batch, CHUNK, D_MODEL), lambda i: (0, i, 0))
    st_blk = pl.BlockSpec((n_batch, STATE_W), lambda i: (0, 0))
    wfi_blk = pl.BlockSpec((D_MODEL // n_steps, 2 * D_FF), lambda i: (i, 0))
    wfo_blk = pl.BlockSpec((D_FF // n_steps, D_MODEL), lambda i: (i, 0))
    consts = _mixer_consts(p, ws, bs)
    return pl.pallas_call(
        functools.partial(_mixer_prompt_kernel, n_batch, n_steps),
        grid=(n_steps,),
        in_specs=[blk] + [_const_spec(c.shape) for c in consts] + [wfi_blk, wfo_blk],
        out_specs=[pl.BlockSpec(memory_space=pl.ANY), st_blk, wfi_blk, wfo_blk],
        out_shape=[jax.ShapeDtypeStruct((n_l, n_batch, D_MODEL), F32),
                   jax.ShapeDtypeStruct((n_batch, STATE_W), F32),
                   jax.ShapeDtypeStruct(w_ffn_in.shape, BF16),
                   jax.ShapeDtypeStruct(w_ffn_out.shape, BF16)],
        scratch_shapes=[pltpu.VMEM((n_batch, STATE_W), F32),
                        pltpu.VMEM((2, n_batch, CHUNK, D_MODEL), F32),
                        pltpu.SemaphoreType.DMA((2,))],
        compiler_params=_params(),
        name="mixer_prompt",
    )(x, *consts, w_ffn_in, w_ffn_out)


def _ffn_call(h_tm, hs_tm, conv_s, p, wfi, wfo):
    n_l, n_batch, _ = h_tm.shape
    ns_batch, n_hist, _ = conv_s.shape
    rows_s = hs_tm.shape[0]
    ns_t = rows_s // ns_batch
    n_t = FFN_PROMPT_STEPS
    n_steps = n_l // n_t
    assert n_steps >= 2
    blk = (n_t, n_batch, D_MODEL)
    weights = _ffn_consts(p, wfi, wfo)
    return pl.pallas_call(
        functools.partial(_ffn_kernel, n_batch, n_t, n_steps, ns_batch, ns_t),
        grid=(n_steps + 1,),
        in_specs=[pl.BlockSpec(blk, lambda i: (jnp.minimum(i, n_steps - 1), 0, 0)),
                  _const_spec(hs_tm.shape), pl.BlockSpec(memory_space=pl.ANY)]
        + [_const_spec(c.shape) for c in weights],
        out_specs=[pl.BlockSpec(memory_space=pl.ANY)] * 4,
        out_shape=[jax.ShapeDtypeStruct((n_batch, n_l, D_MODEL), F32),
                   jax.ShapeDtypeStruct((n_batch, n_hist, 2 * D_FF), F32),
                   jax.ShapeDtypeStruct((ns_batch, ns_t, D_MODEL), F32),
                   jax.ShapeDtypeStruct(conv_s.shape, F32)],
        scratch_shapes=[pltpu.VMEM((n_hist * n_batch, 2 * D_FF), F32),
                        pltpu.VMEM((2,) + blk, F32),
                        pltpu.SemaphoreType.DMA((2,)),
                        pltpu.SemaphoreType.DMA((n_hist,)),
                        pltpu.VMEM((n_hist * ns_batch, 2 * D_FF), F32),
                        pltpu.VMEM((rows_s, D_MODEL), F32),
                        pltpu.SemaphoreType.DMA((n_hist + ns_t,))],
        compiler_params=_params(),
        name="ffn",
    )(h_tm, hs_tm, conv_s, *weights)


def _mixer_sample_call(x, h0, p, ws, bs):
    n_batch, n_t, _ = x.shape
    rows = n_batch * n_t
    consts = [h0] + _mixer_consts(p, ws, bs)
    st_blk = pl.BlockSpec((n_batch, STATE_W), lambda i: (0, 0))
    return pl.pallas_call(
        functools.partial(_mixer_sample_kernel, n_batch, n_t),
        grid=(1,),
        in_specs=[pl.BlockSpec(memory_space=pl.ANY)] + [_const_spec(c.shape) for c in consts],
        out_specs=[pl.BlockSpec((rows, D_MODEL), lambda i: (0, 0)), st_blk,
                   pl.BlockSpec(memory_space=pl.ANY)],
        out_shape=[jax.ShapeDtypeStruct((rows, D_MODEL), F32),
                   jax.ShapeDtypeStruct((n_batch, STATE_W), F32),
                   jax.ShapeDtypeStruct((n_batch, n_t, A_HEADS, A_HEAD_DIM), F32)],
        scratch_shapes=[pltpu.VMEM((rows, D_MODEL), F32),
                        pltpu.VMEM((rows, A_WIDTH), F32),
                        pltpu.VMEM((rows, 2 * HALF_ST), F32),
                        pltpu.VMEM((n_batch, STATE_W), F32),
                        pltpu.SemaphoreType.DMA((n_t + n_t * A_HEADS,))],
        compiler_params=_params(),
        name="mixer_sample",
    )(x, *consts)


def _prep_params(g_mix, w_in, g_v, lam_re, lam_im, log_dt, b_re, b_im, c_re, c_im,
                 d_skip, w_glu, b_glu, g_out_a, g_out_b, w_out, g_ffn, conv_w, conv_b, g_final):
    lr = lam_re.astype(F32)
    li = lam_im.astype(F32)
    dt = jnp.exp(log_dt.astype(F32))[:, None]
    mag = jnp.exp(lr * dt)
    ar = mag * jnp.cos(li * dt)
    ai = mag * jnp.sin(li * dt)
    den = lr * lr + li * li
    fr = ((ar - 1.0) * lr + ai * li) / den
    fi = (ai * lr - (ar - 1.0) * li) / den
    bre = b_re.astype(F32)
    bim = b_im.astype(F32)
    bbr = fr[..., None] * bre - fi[..., None] * bim
    bbi = fr[..., None] * bim + fi[..., None] * bre

    def blockdiag(m, rows_per_group, cols_per_group):
        tiled = jnp.tile(m.reshape(HALF_GROUPS * rows_per_group, cols_per_group), (1, HALF_GROUPS))
        row_g = lax.broadcasted_iota(jnp.int32, tiled.shape, 0) // rows_per_group
        col_g = lax.broadcasted_iota(jnp.int32, tiled.shape, 1) // cols_per_group
        return jnp.where(row_g == col_g, tiled, 0.0)

    def blockdiag_in(m):
        return blockdiag(jnp.transpose(m, (0, 2, 1)), SSM_GROUP, SSM_STATE)

    def blockdiag_out(m):
        return blockdiag(jnp.transpose(m, (0, 2, 1)), SSM_STATE, SSM_GROUP)

    bsub, csub, ars, ais = [], [], [], []
    for k in range(SSM_HALVES):
        gs = slice(k * HALF_GROUPS, (k + 1) * HALF_GROUPS)
        bsub.append(jnp.concatenate([blockdiag_in(bbr[gs]), blockdiag_in(bbi[gs])], axis=1))
        csub.append(jnp.concatenate([blockdiag_out(c_re[gs].astype(F32)),
                                     -blockdiag_out(c_im[gs].astype(F32))], axis=0))
        ars.append(ar[gs].reshape(1, HALF_ST))
        ais.append(ai[gs].reshape(1, HALF_ST))

    return dict(
        g_mix=g_mix.reshape(1, D_MODEL), w_in=w_in.astype(BF16), g_v=g_v,
        ar=jnp.concatenate(ars, axis=0), ai=jnp.concatenate(ais, axis=0),
        bsub=jnp.stack(bsub).astype(BF16), csub=jnp.stack(csub).astype(BF16),
        d_skip=d_skip.reshape(1, SSM_WIDTH), w_glu=w_glu.astype(BF16), b_glu=b_glu.reshape(1, SSM_WIDTH),
        g_out_a=g_out_a.reshape(1, A_WIDTH), g_out_b=g_out_b.reshape(1, SSM_WIDTH),
        w_out=w_out.astype(BF16), g_ffn=g_ffn.reshape(1, D_MODEL),
        conv_w=conv_w, conv_b=conv_b.reshape(1, 2 * D_FF), g_final=g_final.reshape(1, D_MODEL))


def _state_in(s_re, s_im):
    nb = s_re.shape[0]
    return jnp.concatenate([s_re.reshape(nb, ALL_ST), s_im.reshape(nb, ALL_ST)], axis=1)


def _state_out(st):
    nb = st.shape[0]
    return (st[:, :ALL_ST].reshape(nb, SSM_GROUPS, SSM_STATE),
            st[:, ALL_ST:].reshape(nb, SSM_GROUPS, SSM_STATE))


def kernel(x_prompt, x_sample, state_ssm_re, state_ssm_im, state_conv, g_mix, w_in, g_v, w_s, b_s,
           lam_re, lam_im, log_dt, b_re, b_im, c_re, c_im, d_skip, w_glu, b_glu, g_out_a, g_out_b,
           w_out, g_ffn, w_ffn_in, conv_w, conv_b, w_ffn_out, g_final):
    p = _prep_params(g_mix, w_in, g_v, lam_re, lam_im, log_dt, b_re, b_im, c_re, c_im, d_skip,
                     w_glu, b_glu, g_out_a, g_out_b, w_out, g_ffn, conv_w, conv_b, g_final)
    tril = jnp.tril(w_s.astype(F32))

    bs_p = jnp.broadcast_to(jnp.transpose(b_s)[:, :, None], (CHUNK, A_HEADS, A_HEAD_DIM)).reshape(CHUNK, A_WIDTH)
    h_p, st_p, wfi, wfo = _mixer_prompt_call(x_prompt, p, tril.astype(BF16), bs_p, w_ffn_in, w_ffn_out)
    re_p, im_p = _state_out(st_p)

    nls = x_sample.shape[1]
    ws_s = jnp.broadcast_to(tril[:, :nls, :nls].reshape(A_HEADS * nls * nls, 1), (A_HEADS * nls * nls, A_HEAD_DIM))
    bs_s = jnp.broadcast_to(b_s[:, :nls].reshape(A_HEADS * nls, 1), (A_HEADS * nls, A_HEAD_DIM))
    h_s, st_s, v_s = _mixer_sample_call(x_sample, _state_in(state_ssm_re, state_ssm_im), p, ws_s, bs_s)
    re_s, im_s = _state_out(st_s)

    y_p, conv_p, y_s, conv_s = _ffn_call(h_p, h_s, state_conv, p, wfi, wfo)
    return (y_p, y_s, v_s, re_p, im_p, conv_p, re_s, im_s, conv_s)
```

```python
import functools
import math

import jax
import jax.numpy as jnp
from jax import lax
from jax.experimental import pallas as pl
from jax.experimental.pallas import tpu as pltpu

D_MODEL = 1024
A_HEADS = 4
A_HEAD_DIM = 128
A_WIDTH = 512
CHUNK = 128
SSM_WIDTH = 512
SSM_GROUP = 16
SSM_GROUPS = 32
SSM_STATE = 64
D_FF = 2816
CONV_W = 3
EPS = 1e-6

SSM_HALVES = 2
HALF_GROUPS = SSM_GROUPS // SSM_HALVES
HALF_CH = HALF_GROUPS * SSM_GROUP
HALF_ST = HALF_GROUPS * SSM_STATE
ALL_ST = SSM_HALVES * HALF_ST
STATE_W = 2 * ALL_ST

FF_CHUNK = 256
FF_NCHUNK = D_FF // FF_CHUNK
SUB_ROWS = 256
FFN_PROMPT_STEPS = 64

SUBLANES = 8
SCAN_LANES = 512
VMEM_LIMIT_BYTES = 60 * 1024 * 1024

BF16 = jnp.bfloat16
F32 = jnp.float32


def _state_lanes(k):
    return k * HALF_ST, ALL_ST + k * HALF_ST


def _rms(x, g):
    ms = jnp.mean(x * x, axis=-1, keepdims=True)
    return x * lax.rsqrt(ms + EPS) * g


def _gelu(x):
    c = math.sqrt(2.0 / math.pi)
    return x * (0.5 * (1.0 + jnp.tanh(c * (x + 0.044715 * (x * x * x)))))


def _dot(a, b):
    return jnp.dot(a, b, preferred_element_type=F32)


def _to_time_major(x, n_batch, n_t):
    w = x.shape[-1]
    return jnp.swapaxes(x.reshape(n_batch, n_t, w), 0, 1).reshape(n_t * n_batch, w)


def _to_batch_major(x, n_batch, n_t):
    w = x.shape[-1]
    return jnp.swapaxes(x.reshape(n_t, n_batch, w), 0, 1)


def _glu_norm(ys, xs, dskip_ref, wglu_ref, bglu_ref, gob_ref):
    g = _gelu(ys + dskip_ref[...] * xs)
    z = _dot(g.astype(BF16), wglu_ref[...]) + bglu_ref[...]
    return _rms(g * jax.nn.sigmoid(z), gob_ref[...])


def _ffn_subblocks(get_h, put_y, n_batch, n_sub, carry_scr,
                   gffn_ref, wfi_ref, cw_ref, cb_ref, wfo_ref, gfin_ref):
    halo = (CONV_W - 1) * n_batch
    hs, n2s = {}, {}

    def norm_in(s):
        hs[s] = get_h(s)
        n2s[s] = _rms(hs[s], gffn_ref[...]).astype(BF16)

    def col_slices(j):
        return [slice(c0, c0 + FF_CHUNK) for c0 in (j * FF_CHUNK, D_FF + j * FF_CHUNK)]

    def up_proj(j, s):
        return [jnp.dot(n2s[s], wfi_ref[:, cs], preferred_element_type=F32) for cs in col_slices(j)]

    order = [(j, s) for j in range(FF_NCHUNK) for s in range(n_sub)]
    accs = [jnp.zeros((SUB_ROWS, D_MODEL), F32) for _ in range(n_sub)]
    tails = [None, None]
    norm_in(0)
    ups_next = up_proj(*order[0])
    for s in range(1, n_sub):
        norm_in(s)
    for idx, (j, s) in enumerate(order):
        ups = ups_next
        if idx + 1 < len(order):
            ups_next = up_proj(*order[idx + 1])
        conv = []
        for ci, (cs, up) in enumerate(zip(col_slices(j), ups)):
            tail = carry_scr[:, cs] if s == 0 else tails[ci]
            padded = jnp.concatenate([tail, up], axis=0)
            tails[ci] = padded[SUB_ROWS:SUB_ROWS + halo, :]
            if s == n_sub - 1:
                carry_scr[:, cs] = tails[ci]
            out = cb_ref[:, cs]
            for tap in range(CONV_W):
                out = out + cw_ref[tap:tap + 1, cs] * padded[tap * n_batch:tap * n_batch + SUB_ROWS, :]
            conv.append(out)
        act = _gelu(conv[0]) * conv[1]
        accs[s] = accs[s] + jnp.dot(act.astype(BF16), wfo_ref[j * FF_CHUNK:(j + 1) * FF_CHUNK, :],
                                    preferred_element_type=F32)
        if j == FF_NCHUNK - 1:
            put_y(s, _rms(hs[s] + accs[s], gfin_ref[...]))


def _mixer_prompt_step(n_batch, n_steps,
                       x_ref, gmix_ref, win_ref, gv_ref, ws_ref, bs_ref,
                       ar_ref, ai_ref, bsub_ref, csub_ref, dskip_ref,
                       wglu_ref, bglu_ref, goa_ref, gob_ref, wout_ref, wfi32_ref, wfo32_ref,
                       hout_hbm, st_ref, wfi16_ref, wfo16_ref, hst_ref, h_buf, sem):
    sub_t = SUB_ROWS // n_batch
    n_sub = CHUNK // sub_t
    i = pl.program_id(0)
    slot = lax.rem(i, 2)

    def h_copies(step, buf_slot):
        return [pltpu.make_async_copy(h_buf.at[buf_slot, b], hout_hbm.at[pl.ds(step * CHUNK, CHUNK), b, :],
                                      sem.at[buf_slot]) for b in range(n_batch)]

    @pl.when(i == 0)
    def _():
        hst_ref[...] = jnp.zeros(hst_ref.shape, F32)

    @pl.when(i >= 2)
    def _():
        for cp in h_copies(i - 2, slot):
            cp.wait()

    state, trans = [], []
    for k in range(SSM_HALVES):
        re0, im0 = _state_lanes(k)
        state.append((hst_ref[:, re0:re0 + HALF_ST], hst_ref[:, im0:im0 + HALF_ST]))
        trans.append((jnp.broadcast_to(ar_ref[k:k + 1, :], (n_batch, HALF_ST)),
                      jnp.broadcast_to(ai_ref[k:k + 1, :], (n_batch, HALF_ST))))

    x_sub, proj_sub, xs_sub, y_sub, bn_sub = {}, {}, {}, {}, {}
    a_heads = [[] for _ in range(n_sub)]

    def project(s):
        x = x_ref[:, s * sub_t:(s + 1) * sub_t, :].reshape(SUB_ROWS, D_MODEL)
        n1 = _rms(x, gmix_ref[...]).astype(BF16)
        x_sub[s] = x
        proj_sub[s] = _dot(n1, win_ref[...])
        xs_sub[s] = _to_time_major(proj_sub[s][:, 2 * A_WIDTH:], n_batch, sub_t)
        y_sub[s] = []

    def ssm_half(s, k):
        u = _dot(xs_sub[s].astype(BF16)[:, k * HALF_CH:(k + 1) * HALF_CH], bsub_ref[k])
        ar, ai = trans[k]
        hr, hi = state[k]
        slabs = []
        for t in range(sub_t):
            ur = u[t * n_batch:(t + 1) * n_batch, :HALF_ST]
            ui = u[t * n_batch:(t + 1) * n_batch, HALF_ST:]
            hr, hi = ar * hr - ai * hi + ur, ar * hi + ai * hr + ui
            slabs.append(jnp.concatenate([hr, hi], axis=-1))
        state[k] = (hr, hi)
        y_sub[s].append(_dot(jnp.concatenate(slabs, axis=0).astype(BF16), csub_ref[k]))

    def glu(s):
        bn_tm = _glu_norm(jnp.concatenate(y_sub[s], axis=-1), xs_sub[s], dskip_ref, wglu_ref, bglu_ref,
                          gob_ref)
        bn_sub[s] = _to_batch_major(bn_tm, n_batch, sub_t).reshape(SUB_ROWS, SSM_WIDTH)

    def gate_head(h):
        hs = slice(h * A_HEAD_DIM, (h + 1) * A_HEAD_DIM)
        vn = [_rms(proj_sub[s][:, A_WIDTH + h * A_HEAD_DIM:A_WIDTH + (h + 1) * A_HEAD_DIM],
                   gv_ref[h:h + 1, :]).astype(BF16) for s in range(n_sub)]
        v_wide = jnp.concatenate(
            [jnp.concatenate([vn[s][b * sub_t:(b + 1) * sub_t, :] for s in range(n_sub)], axis=0)
             for b in range(n_batch)], axis=-1)
        s_wide = _dot(ws_ref[h], v_wide)
        for s in range(n_sub):
            ts = slice(s * sub_t, (s + 1) * sub_t)
            bias = bs_ref[ts, hs]
            gate = jnp.concatenate([s_wide[ts, b * A_HEAD_DIM:(b + 1) * A_HEAD_DIM] + bias
                                    for b in range(n_batch)], axis=0)
            a_heads[s].append(proj_sub[s][:, hs] * gate)

    def out_proj(s):
        an = _rms(jnp.concatenate(a_heads[s], axis=-1), goa_ref[...])
        mix = jnp.concatenate([an, bn_sub[s]], axis=-1).astype(BF16)
        h_buf[slot, :, s * sub_t:(s + 1) * sub_t, :] = (
            x_sub[s] + _dot(mix, wout_ref[...])).reshape(n_batch, sub_t, D_MODEL)

    chain = [project] + [functools.partial(ssm_half, k=k) for k in range(SSM_HALVES)] + [glu]
    heads = list(range(A_HEADS))
    outs = list(range(n_sub))
    for wave in range(n_sub + len(chain) - 1):
        if wave >= n_sub:
            for h in heads[:2]:
                gate_head(h)
            heads = heads[2:]
        for s in range(n_sub):
            if 0 <= wave - s < len(chain):
                chain[wave - s](s)
        if not heads and outs and wave - outs[0] >= len(chain) - 1:
            out_proj(outs.pop(0))
    for h in heads:
        gate_head(h)
    for s in outs:
        out_proj(s)

    for k in range(SSM_HALVES):
        re0, im0 = _state_lanes(k)
        hst_ref[:, re0:re0 + HALF_ST] = state[k][0]
        hst_ref[:, im0:im0 + HALF_ST] = state[k][1]

    wfi16_ref[...] = wfi32_ref[...].astype(BF16)
    wfo16_ref[...] = wfo32_ref[...].astype(BF16)
    for cp in h_copies(i, slot):
        cp.start()

    @pl.when(i == n_steps - 1)
    def _():
        st_ref[...] = hst_ref[...]
        for cp in h_copies(i - 1, 1 - slot) + h_copies(i, slot):
            cp.wait()


def _ffn_kernel(n_batch, n_t, n_steps, ns_batch, ns_t,
                h_ref, hs_ref, convs_hbm, gffn_ref, wfi_ref, cw_ref, cb_ref, wfo_ref, gfin_ref,
                y_hbm, convp_out_hbm, ys_hbm, convs_out_hbm,
                carry_scr, y_buf, sem, sem_p, carry_s_scr, ys_buf, sem_s):
    sub_t = SUB_ROWS // n_batch
    i = pl.program_id(0)
    slot = lax.rem(i, 2)
    weights = (gffn_ref, wfi_ref, cw_ref, cb_ref, wfo_ref, gfin_ref)
    n_hist = CONV_W - 1

    def y_copies(step, buf_slot):
        return [pltpu.make_async_copy(y_buf.at[buf_slot, :, b, :], y_hbm.at[b, pl.ds(step * n_t, n_t), :],
                                      sem.at[buf_slot]) for b in range(n_batch)]

    histp_out = [pltpu.make_async_copy(carry_scr.at[pl.ds(k * n_batch, n_batch), :], convp_out_hbm.at[:, k, :],
                                       sem_p.at[k]) for k in range(n_hist)]
    hist_in = [pltpu.make_async_copy(convs_hbm.at[:, k, :], carry_s_scr.at[pl.ds(k * ns_batch, ns_batch), :],
                                     sem_s.at[k]) for k in range(n_hist)]
    hist_out = [pltpu.make_async_copy(carry_s_scr.at[pl.ds(k * ns_batch, ns_batch), :], convs_out_hbm.at[:, k, :],
                                      sem_s.at[k]) for k in range(n_hist)]
    ys_out = [pltpu.make_async_copy(ys_buf.at[pl.ds(t * ns_batch, ns_batch), :], ys_hbm.at[:, t, :],
                                    sem_s.at[n_hist + t]) for t in range(ns_t)]

    @pl.when(i == 0)
    def _():
        carry_scr[...] = jnp.zeros(carry_scr.shape, F32)
        for cp in hist_in:
            cp.start()

    @pl.when(jnp.logical_and(i >= 2, i < n_steps))
    def _():
        for cp in y_copies(i - 2, slot):
            cp.wait()

    @pl.when(i < n_steps)
    def _():
        def get_h(s):
            return h_ref[s * sub_t:(s + 1) * sub_t].reshape(SUB_ROWS, D_MODEL)

        def put_y(s, y):
            y_buf[slot, s * sub_t:(s + 1) * sub_t] = y.reshape(sub_t, n_batch, D_MODEL)

        _ffn_subblocks(get_h, put_y, n_batch, n_t // sub_t, carry_scr, *weights)
        for cp in y_copies(i, slot):
            cp.start()

    @pl.when(i == n_steps - 1)
    def _():
        for cp in histp_out:
            cp.start()
        for cp in y_copies(i - 1, 1 - slot) + y_copies(i, slot) + histp_out:
            cp.wait()

    @pl.when(i == n_steps)
    def _():
        for cp in hist_in:
            cp.wait()

        def get_h(s):
            return hs_ref[s * SUB_ROWS:(s + 1) * SUB_ROWS, :]

        def put_y(s, y):
            ys_buf[s * SUB_ROWS:(s + 1) * SUB_ROWS, :] = y

        _ffn_subblocks(get_h, put_y, ns_batch, ns_t * ns_batch // SUB_ROWS, carry_s_scr, *weights)
        for cp in hist_out + ys_out:
            cp.start()
        for cp in hist_out + ys_out:
            cp.wait()


def _scan_inplace(u_scr, hst_ref, ar_ref, ai_ref, k, n_batch, n_t):
    re0, im0 = _state_lanes(k)

    def row_chunk(rc, _):
        r0 = pl.multiple_of(rc * SUBLANES, SUBLANES)
        for c in range(HALF_ST // SCAN_LANES):
            lo = c * SCAN_LANES
            re_l = slice(lo, lo + SCAN_LANES)
            im_l = slice(HALF_ST + lo, HALF_ST + lo + SCAN_LANES)
            ar = jnp.broadcast_to(ar_ref[k:k + 1, re_l], (SUBLANES, SCAN_LANES))
            ai = jnp.broadcast_to(ai_ref[k:k + 1, re_l], (SUBLANES, SCAN_LANES))
            hr0 = hst_ref[pl.ds(r0, SUBLANES), re0 + lo:re0 + lo + SCAN_LANES]
            hi0 = hst_ref[pl.ds(r0, SUBLANES), im0 + lo:im0 + lo + SCAN_LANES]

            def step(t, carry):
                hr, hi = carry
                row = pl.multiple_of(t * n_batch + r0, SUBLANES)
                ur = u_scr[pl.ds(row, SUBLANES), re_l]
                ui = u_scr[pl.ds(row, SUBLANES), im_l]
                nhr = ar * hr - ai * hi + ur
                nhi = ar * hi + ai * hr + ui
                u_scr[pl.ds(row, SUBLANES), re_l] = nhr
                u_scr[pl.ds(row, SUBLANES), im_l] = nhi
                return nhr, nhi

            hr, hi = lax.fori_loop(0, n_t, step, (hr0, hi0), unroll=min(n_t, 4))
            hst_ref[pl.ds(r0, SUBLANES), re0 + lo:re0 + lo + SCAN_LANES] = hr
            hst_ref[pl.ds(r0, SUBLANES), im0 + lo:im0 + lo + SCAN_LANES] = hi
        return 0

    lax.fori_loop(0, n_batch // SUBLANES, row_chunk, 0)


def _sample_in_copies(n_batch, n_t, x_hbm, h0_hbm, x_buf, hst_ref, sem):
    return [pltpu.make_async_copy(x_hbm.at[:, t, :], x_buf.at[pl.ds(t * n_batch, n_batch), :], sem.at[t])
            for t in range(n_t)] + [pltpu.make_async_copy(h0_hbm, hst_ref, sem.at[n_t])]


def _mixer_sample_step(n_batch, n_t,
                       x_hbm, h0_hbm, gmix_ref, win_ref, gv_ref, ws_ref, bs_ref,
                       ar_ref, ai_ref, bsub_ref, csub_ref, dskip_ref,
                       wglu_ref, bglu_ref, goa_ref, gob_ref, wout_ref,
                       hout_hbm, st_hbm, v_hbm, x_buf, v_buf, u_scr, hst_ref, sem):
    v_out = [pltpu.make_async_copy(
        v_buf.at[pl.ds(t * n_batch, n_batch), pl.ds(h * A_HEAD_DIM, A_HEAD_DIM)],
        v_hbm.at[:, t, h, :], sem.at[n_t + 1 + t * A_HEADS + h]) for t in range(n_t) for h in range(A_HEADS)]
    for cp in _sample_in_copies(n_batch, n_t, x_hbm, h0_hbm, x_buf, hst_ref, sem):
        cp.wait()
    x = x_buf[...]
    n1 = _rms(x, gmix_ref[...]).astype(BF16)
    proj = _dot(n1, win_ref[...])

    a_heads = []
    for h in range(A_HEADS):
        hs = slice(h * A_HEAD_DIM, (h + 1) * A_HEAD_DIM)
        vn = _rms(proj[:, A_WIDTH + h * A_HEAD_DIM:A_WIDTH + (h + 1) * A_HEAD_DIM], gv_ref[h:h + 1, :])
        v_buf[:, hs] = vn
        slabs = []
        for t in range(n_t):
            s = jnp.broadcast_to(bs_ref[h * n_t + t:h * n_t + t + 1, :], (n_batch, A_HEAD_DIM))
            for src in range(t + 1):
                r = (h * n_t + t) * n_t + src
                s = s + ws_ref[r:r + 1, :] * vn[src * n_batch:(src + 1) * n_batch, :]
            slabs.append(proj[t * n_batch:(t + 1) * n_batch, hs] * s)
        a_heads.append(jnp.concatenate(slabs, axis=0))
    an = _rms(jnp.concatenate(a_heads, axis=-1), goa_ref[...])
    for cp in v_out:
        cp.start()

    xs = proj[:, 2 * A_WIDTH:]
    xs_b = xs.astype(BF16)
    y_parts = []
    for k in range(SSM_HALVES):
        u_scr[...] = _dot(xs_b[:, k * HALF_CH:(k + 1) * HALF_CH], bsub_ref[k])
        _scan_inplace(u_scr, hst_ref, ar_ref, ai_ref, k, n_batch, n_t)
        y_parts.append(_dot(u_scr[...].astype(BF16), csub_ref[k]))
    bn = _glu_norm(jnp.concatenate(y_parts, axis=-1), xs, dskip_ref, wglu_ref, bglu_ref, gob_ref)

    mix = jnp.concatenate([an, bn], axis=-1).astype(BF16)
    x_buf[...] = x + _dot(mix, wout_ref[...])
    n_in_v = n_t + 1 + n_t * A_HEADS
    outs = [pltpu.make_async_copy(x_buf, hout_hbm, sem.at[n_in_v]),
            pltpu.make_async_copy(hst_ref, st_hbm, sem.at[n_in_v + 1])]
    for cp in outs:
        cp.start()
    for cp in v_out + outs:
        cp.wait()


def _mixer_kernel(n_batch, n_steps, ns_batch, ns_t,
                  x_ref, xs_hbm, h0s_hbm, gmix_ref, win_ref, gv_ref, ws_ref, bs_ref,
                  ar_ref, ai_ref, bsub_ref, csub_ref, dskip_ref,
                  wglu_ref, bglu_ref, goa_ref, gob_ref, wout_ref, wss_ref, bss_ref, wfi32_ref, wfo32_ref,
                  hout_hbm, st_ref, wfi16_ref, wfo16_ref, hs_hbm, sts_hbm, v_hbm,
                  hst_ref, h_buf, sem, x_buf, v_buf, u_scr, hsts_ref, sem_s):
    i = pl.program_id(0)
    shared = (ar_ref, ai_ref, bsub_ref, csub_ref, dskip_ref, wglu_ref, bglu_ref, goa_ref, gob_ref, wout_ref)

    @pl.when(i == 0)
    def _():
        for cp in _sample_in_copies(ns_batch, ns_t, xs_hbm, h0s_hbm, x_buf, hsts_ref, sem_s):
            cp.start()

    @pl.when(i < n_steps)
    def _():
        _mixer_prompt_step(n_batch, n_steps, x_ref, gmix_ref, win_ref, gv_ref, ws_ref, bs_ref, *shared,
                           wfi32_ref, wfo32_ref, hout_hbm, st_ref, wfi16_ref, wfo16_ref, hst_ref, h_buf, sem)

    @pl.when(i == n_steps)
    def _():
        _mixer_sample_step(ns_batch, ns_t, xs_hbm, h0s_hbm, gmix_ref, win_ref, gv_ref, wss_ref, bss_ref, *shared,
                           hs_hbm, sts_hbm, v_hbm, x_buf, v_buf, u_scr, hsts_ref, sem_s)


def _const_spec(shape):
    zeros = (0,) * len(shape)
    return pl.BlockSpec(shape, lambda i: zeros, pipeline_mode=pl.Buffered(1))


def _params():
    return pltpu.CompilerParams(dimension_semantics=("arbitrary",), vmem_limit_bytes=VMEM_LIMIT_BYTES)


def _mixer_consts(p, ws, bs):
    return [p['g_mix'], p['w_in'], p['g_v'], ws, bs, p['ar'], p['ai'],
            p['bsub'], p['csub'], p['d_skip'], p['w_glu'], p['b_glu'],
            p['g_out_a'], p['g_out_b'], p['w_out']]


def _ffn_consts(p, wfi, wfo):
    return [p['g_ffn'], wfi, p['conv_w'], p['conv_b'], wfo, p['g_final']]


def _mixer_call(x, xs, h0s, p, ws, bs, wss, bss, w_ffn_in, w_ffn_out):
    n_batch, n_l, _ = x.shape
    ns_batch, ns_t, _ = xs.shape
    rows_s = ns_batch * ns_t
    n_steps = n_l // CHUNK
    assert n_steps >= 2

    def last(i):
        return jnp.minimum(i, n_steps - 1)

    any_spec = pl.BlockSpec(memory_space=pl.ANY)
    blk = pl.BlockSpec((n_batch, CHUNK, D_MODEL), lambda i: (0, last(i), 0))
    st_blk = pl.BlockSpec((n_batch, STATE_W), lambda i: (0, 0))
    wfi_blk = pl.BlockSpec((D_MODEL // n_steps, 2 * D_FF), lambda i: (last(i), 0))
    wfo_blk = pl.BlockSpec((D_FF // n_steps, D_MODEL), lambda i: (last(i), 0))
    consts = _mixer_consts(p, ws, bs) + [wss, bss]
    return pl.pallas_call(
        functools.partial(_mixer_kernel, n_batch, n_steps, ns_batch, ns_t),
        grid=(n_steps + 1,),
        in_specs=[blk, any_spec, any_spec] + [_const_spec(c.shape) for c in consts] + [wfi_blk, wfo_blk],
        out_specs=[any_spec, st_blk, wfi_blk, wfo_blk, any_spec, any_spec, any_spec],
        out_shape=[jax.ShapeDtypeStruct((n_l, n_batch, D_MODEL), F32),
                   jax.ShapeDtypeStruct((n_batch, STATE_W), F32),
                   jax.ShapeDtypeStruct(w_ffn_in.shape, BF16),
                   jax.ShapeDtypeStruct(w_ffn_out.shape, BF16),
                   jax.ShapeDtypeStruct((rows_s, D_MODEL), F32),
                   jax.ShapeDtypeStruct((ns_batch, STATE_W), F32),
                   jax.ShapeDtypeStruct((ns_batch, ns_t, A_HEADS, A_HEAD_DIM), F32)],
        scratch_shapes=[pltpu.VMEM((n_batch, STATE_W), F32),
                        pltpu.VMEM((2, n_batch, CHUNK, D_MODEL), F32),
                        pltpu.SemaphoreType.DMA((2,)),
                        pltpu.VMEM((rows_s, D_MODEL), F32),
                        pltpu.VMEM((rows_s, A_WIDTH), F32),
                        pltpu.VMEM((rows_s, 2 * HALF_ST), F32),
                        pltpu.VMEM((ns_batch, STATE_W), F32),
                        pltpu.SemaphoreType.DMA((ns_t + 1 + ns_t * A_HEADS + 2,))],
        compiler_params=_params(),
        name="mixer",
    )(x, xs, h0s, *consts, w_ffn_in, w_ffn_out)


def _ffn_call(h_tm, hs_tm, conv_s, p, wfi, wfo):
    n_l, n_batch, _ = h_tm.shape
    ns_batch, n_hist, _ = conv_s.shape
    rows_s = hs_tm.shape[0]
    ns_t = rows_s // ns_batch
    n_t = FFN_PROMPT_STEPS
    n_steps = n_l // n_t
    assert n_steps >= 2
    blk = (n_t, n_batch, D_MODEL)
    weights = _ffn_consts(p, wfi, wfo)
    return pl.pallas_call(
        functools.partial(_ffn_kernel, n_batch, n_t, n_steps, ns_batch, ns_t),
        grid=(n_steps + 1,),
        in_specs=[pl.BlockSpec(blk, lambda i: (jnp.minimum(i, n_steps - 1), 0, 0)),
                  _const_spec(hs_tm.shape), pl.BlockSpec(memory_space=pl.ANY)]
        + [_const_spec(c.shape) for c in weights],
        out_specs=[pl.BlockSpec(memory_space=pl.ANY)] * 4,
        out_shape=[jax.ShapeDtypeStruct((n_batch, n_l, D_MODEL), F32),
                   jax.ShapeDtypeStruct((n_batch, n_hist, 2 * D_FF), F32),
                   jax.ShapeDtypeStruct((ns_batch, ns_t, D_MODEL), F32),
                   jax.ShapeDtypeStruct(conv_s.shape, F32)],
        scratch_shapes=[pltpu.VMEM((n_hist * n_batch, 2 * D_FF), F32),
                        pltpu.VMEM((2,) + blk, F32),
                        pltpu.SemaphoreType.DMA((2,)),
                        pltpu.SemaphoreType.DMA((n_hist,)),
                        pltpu.VMEM((n_hist * ns_batch, 2 * D_FF), F32),
                        pltpu.VMEM((rows_s, D_MODEL), F32),
                        pltpu.SemaphoreType.DMA((n_hist + ns_t,))],
        compiler_params=_params(),
        name="ffn",
    )(h_tm, hs_tm, conv_s, *weights)


def _prep_params(g_mix, w_in, g_v, lam_re, lam_im, log_dt, b_re, b_im, c_re, c_im,
                 d_skip, w_glu, b_glu, g_out_a, g_out_b, w_out, g_ffn, conv_w, conv_b, g_final):
    lr = lam_re.astype(F32)
    li = lam_im.astype(F32)
    dt = jnp.exp(log_dt.astype(F32))[:, None]
    mag = jnp.exp(lr * dt)
    ar = mag * jnp.cos(li * dt)
    ai = mag * jnp.sin(li * dt)
    den = lr * lr + li * li
    fr = ((ar - 1.0) * lr + ai * li) / den
    fi = (ai * lr - (ar - 1.0) * li) / den
    bre = b_re.astype(F32)
    bim = b_im.astype(F32)
    bbr = fr[..., None] * bre - fi[..., None] * bim
    bbi = fr[..., None] * bim + fi[..., None] * bre

    def blockdiag(m, rows_per_group, cols_per_group):
        tiled = jnp.tile(m.reshape(HALF_GROUPS * rows_per_group, cols_per_group), (1, HALF_GROUPS))
        row_g = lax.broadcasted_iota(jnp.int32, tiled.shape, 0) // rows_per_group
        col_g = lax.broadcasted_iota(jnp.int32, tiled.shape, 1) // cols_per_group
        return jnp.where(row_g == col_g, tiled, 0.0)

    def blockdiag_in(m):
        return blockdiag(jnp.transpose(m, (0, 2, 1)), SSM_GROUP, SSM_STATE)

    def blockdiag_out(m):
        return blockdiag(jnp.transpose(m, (0, 2, 1)), SSM_STATE, SSM_GROUP)

    bsub, csub, ars, ais = [], [], [], []
    for k in range(SSM_HALVES):
        gs = slice(k * HALF_GROUPS, (k + 1) * HALF_GROUPS)
        bsub.append(jnp.concatenate([blockdiag_in(bbr[gs]), blockdiag_in(bbi[gs])], axis=1))
        csub.append(jnp.concatenate([blockdiag_out(c_re[gs].astype(F32)),
                                     -blockdiag_out(c_im[gs].astype(F32))], axis=0))
        ars.append(ar[gs].reshape(1, HALF_ST))
        ais.append(ai[gs].reshape(1, HALF_ST))

    return dict(
        g_mix=g_mix.reshape(1, D_MODEL), w_in=w_in.astype(BF16), g_v=g_v,
        ar=jnp.concatenate(ars, axis=0), ai=jnp.concatenate(ais, axis=0),
        bsub=jnp.stack(bsub).astype(BF16), csub=jnp.stack(csub).astype(BF16),
        d_skip=d_skip.reshape(1, SSM_WIDTH), w_glu=w_glu.astype(BF16), b_glu=b_glu.reshape(1, SSM_WIDTH),
        g_out_a=g_out_a.reshape(1, A_WIDTH), g_out_b=g_out_b.reshape(1, SSM_WIDTH),
        w_out=w_out.astype(BF16), g_ffn=g_ffn.reshape(1, D_MODEL),
        conv_w=conv_w, conv_b=conv_b.reshape(1, 2 * D_FF), g_final=g_final.reshape(1, D_MODEL))


def _state_in(s_re, s_im):
    nb = s_re.shape[0]
    return jnp.concatenate([s_re.reshape(nb, ALL_ST), s_im.reshape(nb, ALL_ST)], axis=1)


def _state_out(st):
    nb = st.shape[0]
    return (st[:, :ALL_ST].reshape(nb, SSM_GROUPS, SSM_STATE),
            st[:, ALL_ST:].reshape(nb, SSM_GROUPS, SSM_STATE))


def kernel(x_prompt, x_sample, state_ssm_re, state_ssm_im, state_conv, g_mix, w_in, g_v, w_s, b_s,
           lam_re, lam_im, log_dt, b_re, b_im, c_re, c_im, d_skip, w_glu, b_glu, g_out_a, g_out_b,
           w_out, g_ffn, w_ffn_in, conv_w, conv_b, w_ffn_out, g_final):
    p = _prep_params(g_mix, w_in, g_v, lam_re, lam_im, log_dt, b_re, b_im, c_re, c_im, d_skip,
                     w_glu, b_glu, g_out_a, g_out_b, w_out, g_ffn, conv_w, conv_b, g_final)
    tril = jnp.tril(w_s.astype(F32))

    bs_p = jnp.broadcast_to(jnp.transpose(b_s)[:, :, None], (CHUNK, A_HEADS, A_HEAD_DIM)).reshape(CHUNK, A_WIDTH)
    nls = x_sample.shape[1]
    ws_s = jnp.broadcast_to(tril[:, :nls, :nls].reshape(A_HEADS * nls * nls, 1), (A_HEADS * nls * nls, A_HEAD_DIM))
    bs_s = jnp.broadcast_to(b_s[:, :nls].reshape(A_HEADS * nls, 1), (A_HEADS * nls, A_HEAD_DIM))
    h_p, st_p, wfi, wfo, h_s, st_s, v_s = _mixer_call(
        x_prompt, x_sample, _state_in(state_ssm_re, state_ssm_im), p,
        tril.astype(BF16), bs_p, ws_s, bs_s, w_ffn_in, w_ffn_out)
    re_p, im_p = _state_out(st_p)
    re_s, im_s = _state_out(st_s)

    y_p, conv_p, y_s, conv_s = _ffn_call(h_p, h_s, state_conv, p, wfi, wfo)
    return (y_p, y_s, v_s, re_p, im_p, conv_p, re_s, im_s, conv_s)
```

```python
import functools
import math

import jax
import jax.numpy as jnp
from jax import lax
from jax.experimental import pallas as pl
from jax.experimental.pallas import tpu as pltpu

D_MODEL = 1024
A_HEADS = 4
A_HEAD_DIM = 128
A_WIDTH = 512
CHUNK = 128
SSM_WIDTH = 512
SSM_GROUP = 16
SSM_GROUPS = 32
SSM_STATE = 64
D_FF = 2816
CONV_W = 3
EPS = 1e-6

SSM_HALVES = 2
HALF_GROUPS = SSM_GROUPS // SSM_HALVES
HALF_CH = HALF_GROUPS * SSM_GROUP
HALF_ST = HALF_GROUPS * SSM_STATE
ALL_ST = SSM_HALVES * HALF_ST
STATE_W = 2 * ALL_ST

FF_CHUNK = 256
FF_NCHUNK = D_FF // FF_CHUNK
SUB_ROWS = 256
FFN_PROMPT_STEPS = 64

SUBLANES = 8
SCAN_LANES = 512
VMEM_LIMIT_BYTES = 60 * 1024 * 1024

BF16 = jnp.bfloat16
F32 = jnp.float32


def _state_lanes(k):
    return k * HALF_ST, ALL_ST + k * HALF_ST


def _rms(x, g):
    ms = jnp.mean(x * x, axis=-1, keepdims=True)
    return x * lax.rsqrt(ms + EPS) * g


def _gelu(x):
    c = math.sqrt(2.0 / math.pi)
    return x * (0.5 * (1.0 + jnp.tanh(c * (x + 0.044715 * (x * x * x)))))


def _dot(a, b):
    return jnp.dot(a, b, preferred_element_type=F32)


def _to_time_major(x, n_batch, n_t):
    w = x.shape[-1]
    return jnp.swapaxes(x.reshape(n_batch, n_t, w), 0, 1).reshape(n_t * n_batch, w)


def _to_batch_major(x, n_batch, n_t):
    w = x.shape[-1]
    return jnp.swapaxes(x.reshape(n_t, n_batch, w), 0, 1)


def _glu_norm(ys, xs, dskip_ref, wglu_ref, bglu_ref, gob_ref):
    g = _gelu(ys + dskip_ref[...] * xs)
    z = _dot(g.astype(BF16), wglu_ref[...]) + bglu_ref[...]
    return _rms(g * jax.nn.sigmoid(z), gob_ref[...])


def _ffn_subblocks(get_h, put_y, n_batch, n_sub, carry_scr,
                   gffn_ref, wfi_ref, cw_ref, cb_ref, wfo_ref, gfin_ref):
    halo = (CONV_W - 1) * n_batch
    hs, n2s = {}, {}

    def norm_in(s):
        hs[s] = get_h(s)
        n2s[s] = _rms(hs[s], gffn_ref[...]).astype(BF16)

    def col_slices(j):
        return [slice(c0, c0 + FF_CHUNK) for c0 in (j * FF_CHUNK, D_FF + j * FF_CHUNK)]

    def up_proj(j, s):
        return [jnp.dot(n2s[s], wfi_ref[:, cs], preferred_element_type=F32) for cs in col_slices(j)]

    order = [(j, s) for j in range(FF_NCHUNK) for s in range(n_sub)]
    accs = [jnp.zeros((SUB_ROWS, D_MODEL), F32) for _ in range(n_sub)]
    tails = [None, None]
    norm_in(0)
    ups_next = up_proj(*order[0])
    for s in range(1, n_sub):
        norm_in(s)
    for idx, (j, s) in enumerate(order):
        ups = ups_next
        if idx + 1 < len(order):
            ups_next = up_proj(*order[idx + 1])
        conv = []
        for ci, (cs, up) in enumerate(zip(col_slices(j), ups)):
            tail = carry_scr[:, cs] if s == 0 else tails[ci]
            padded = jnp.concatenate([tail, up], axis=0)
            tails[ci] = padded[SUB_ROWS:SUB_ROWS + halo, :]
            if s == n_sub - 1:
                carry_scr[:, cs] = tails[ci]
            out = cb_ref[:, cs]
            for tap in range(CONV_W):
                out = out + cw_ref[tap:tap + 1, cs] * padded[tap * n_batch:tap * n_batch + SUB_ROWS, :]
            conv.append(out)
        act = _gelu(conv[0]) * conv[1]
        accs[s] = accs[s] + jnp.dot(act.astype(BF16), wfo_ref[j * FF_CHUNK:(j + 1) * FF_CHUNK, :],
                                    preferred_element_type=F32)
        if j == FF_NCHUNK - 1:
            put_y(s, _rms(hs[s] + accs[s], gfin_ref[...]))


def _mixer_prompt_step(n_batch, n_steps, i,
                       x_ref, gmix_ref, win_ref, gv_ref, ws_ref, bs_ref,
                       ar_ref, ai_ref, bsub_ref, csub_ref, dskip_ref,
                       wglu_ref, bglu_ref, goa_ref, gob_ref, wout_ref, wfi32_ref, wfo32_ref,
                       hout_hbm, st_ref, wfi16_ref, wfo16_ref, hst_ref, h_buf, sem):
    sub_t = SUB_ROWS // n_batch
    n_sub = CHUNK // sub_t
    slot = lax.rem(i, 2)

    def h_copies(step, buf_slot):
        return [pltpu.make_async_copy(h_buf.at[buf_slot, b], hout_hbm.at[pl.ds(step * CHUNK, CHUNK), b, :],
                                      sem.at[buf_slot]) for b in range(n_batch)]

    @pl.when(i == 0)
    def _():
        hst_ref[...] = jnp.zeros(hst_ref.shape, F32)

    @pl.when(i >= 2)
    def _():
        for cp in h_copies(i - 2, slot):
            cp.wait()

    state, trans = [], []
    for k in range(SSM_HALVES):
        re0, im0 = _state_lanes(k)
        state.append((hst_ref[:, re0:re0 + HALF_ST], hst_ref[:, im0:im0 + HALF_ST]))
        trans.append((jnp.broadcast_to(ar_ref[k:k + 1, :], (n_batch, HALF_ST)),
                      jnp.broadcast_to(ai_ref[k:k + 1, :], (n_batch, HALF_ST))))

    x_sub, proj_sub, xs_sub, y_sub, bn_sub = {}, {}, {}, {}, {}
    a_heads = [[] for _ in range(n_sub)]

    def project(s):
        x = x_ref[:, s * sub_t:(s + 1) * sub_t, :].reshape(SUB_ROWS, D_MODEL)
        n1 = _rms(x, gmix_ref[...]).astype(BF16)
        x_sub[s] = x
        proj_sub[s] = _dot(n1, win_ref[...])
        xs_sub[s] = _to_time_major(proj_sub[s][:, 2 * A_WIDTH:], n_batch, sub_t)
        y_sub[s] = []

    def ssm_half(s, k):
        u = _dot(xs_sub[s].astype(BF16)[:, k * HALF_CH:(k + 1) * HALF_CH], bsub_ref[k])
        ar, ai = trans[k]
        hr, hi = state[k]
        slabs = []
        for t in range(sub_t):
            ur = u[t * n_batch:(t + 1) * n_batch, :HALF_ST]
            ui = u[t * n_batch:(t + 1) * n_batch, HALF_ST:]
            hr, hi = ar * hr - ai * hi + ur, ar * hi + ai * hr + ui
            slabs.append(jnp.concatenate([hr, hi], axis=-1))
        state[k] = (hr, hi)
        y_sub[s].append(_dot(jnp.concatenate(slabs, axis=0).astype(BF16), csub_ref[k]))

    def glu(s):
        bn_tm = _glu_norm(jnp.concatenate(y_sub[s], axis=-1), xs_sub[s], dskip_ref, wglu_ref, bglu_ref,
                          gob_ref)
        bn_sub[s] = _to_batch_major(bn_tm, n_batch, sub_t).reshape(SUB_ROWS, SSM_WIDTH)

    def gate_head(h):
        hs = slice(h * A_HEAD_DIM, (h + 1) * A_HEAD_DIM)
        vn = [_rms(proj_sub[s][:, A_WIDTH + h * A_HEAD_DIM:A_WIDTH + (h + 1) * A_HEAD_DIM],
                   gv_ref[h:h + 1, :]).astype(BF16) for s in range(n_sub)]
        v_wide = jnp.concatenate(
            [jnp.concatenate([vn[s][b * sub_t:(b + 1) * sub_t, :] for s in range(n_sub)], axis=0)
             for b in range(n_batch)], axis=-1)
        s_wide = _dot(ws_ref[h], v_wide)
        for s in range(n_sub):
            ts = slice(s * sub_t, (s + 1) * sub_t)
            bias = bs_ref[ts, hs]
            gate = jnp.concatenate([s_wide[ts, b * A_HEAD_DIM:(b + 1) * A_HEAD_DIM] + bias
                                    for b in range(n_batch)], axis=0)
            a_heads[s].append(proj_sub[s][:, hs] * gate)

    def out_proj(s):
        an = _rms(jnp.concatenate(a_heads[s], axis=-1), goa_ref[...])
        mix = jnp.concatenate([an, bn_sub[s]], axis=-1).astype(BF16)
        h_buf[slot, :, s * sub_t:(s + 1) * sub_t, :] = (
            x_sub[s] + _dot(mix, wout_ref[...])).reshape(n_batch, sub_t, D_MODEL)

    chain = [project] + [functools.partial(ssm_half, k=k) for k in range(SSM_HALVES)] + [glu]
    heads = list(range(A_HEADS))
    outs = list(range(n_sub))
    for wave in range(n_sub + len(chain) - 1):
        if wave >= n_sub:
            for h in heads[:2]:
                gate_head(h)
            heads = heads[2:]
        for s in range(n_sub):
            if 0 <= wave - s < len(chain):
                chain[wave - s](s)
        if not heads and outs and wave - outs[0] >= len(chain) - 1:
            out_proj(outs.pop(0))
    for h in heads:
        gate_head(h)
    for s in outs:
        out_proj(s)

    for k in range(SSM_HALVES):
        re0, im0 = _state_lanes(k)
        hst_ref[:, re0:re0 + HALF_ST] = state[k][0]
        hst_ref[:, im0:im0 + HALF_ST] = state[k][1]

    wfi16_ref[...] = wfi32_ref[...].astype(BF16)
    wfo16_ref[...] = wfo32_ref[...].astype(BF16)
    for cp in h_copies(i, slot):
        cp.start()

    @pl.when(i == n_steps - 1)
    def _():
        st_ref[...] = hst_ref[...]
        for cp in h_copies(i - 1, 1 - slot) + h_copies(i, slot):
            cp.wait()


def _ffn_kernel(n_batch, n_t, n_steps, ns_batch, ns_t,
                h_ref, hs_ref, convs_hbm, gffn_ref, wfi_ref, cw_ref, cb_ref, wfo_ref, gfin_ref,
                y_hbm, convp_out_hbm, ys_hbm, convs_out_hbm,
                carry_scr, y_buf, sem, sem_p, carry_s_scr, ys_buf, sem_s):
    sub_t = SUB_ROWS // n_batch
    i = pl.program_id(0)
    slot = lax.rem(i, 2)
    weights = (gffn_ref, wfi_ref, cw_ref, cb_ref, wfo_ref, gfin_ref)
    n_hist = CONV_W - 1

    def y_copies(step, buf_slot):
        return [pltpu.make_async_copy(y_buf.at[buf_slot, :, b, :], y_hbm.at[b, pl.ds(step * n_t, n_t), :],
                                      sem.at[buf_slot]) for b in range(n_batch)]

    histp_out = [pltpu.make_async_copy(carry_scr.at[pl.ds(k * n_batch, n_batch), :], convp_out_hbm.at[:, k, :],
                                       sem_p.at[k]) for k in range(n_hist)]
    hist_in = [pltpu.make_async_copy(convs_hbm.at[:, k, :], carry_s_scr.at[pl.ds(k * ns_batch, ns_batch), :],
                                     sem_s.at[k]) for k in range(n_hist)]
    hist_out = [pltpu.make_async_copy(carry_s_scr.at[pl.ds(k * ns_batch, ns_batch), :], convs_out_hbm.at[:, k, :],
                                      sem_s.at[k]) for k in range(n_hist)]
    ys_out = [pltpu.make_async_copy(ys_buf.at[pl.ds(t * ns_batch, ns_batch), :], ys_hbm.at[:, t, :],
                                    sem_s.at[n_hist + t]) for t in range(ns_t)]

    @pl.when(i == 0)
    def _():
        carry_scr[...] = jnp.zeros(carry_scr.shape, F32)
        for cp in hist_in:
            cp.start()

    @pl.when(jnp.logical_and(i >= 2, i < n_steps))
    def _():
        for cp in y_copies(i - 2, slot):
            cp.wait()

    @pl.when(i < n_steps)
    def _():
        def get_h(s):
            return h_ref[s * sub_t:(s + 1) * sub_t].reshape(SUB_ROWS, D_MODEL)

        def put_y(s, y):
            y_buf[slot, s * sub_t:(s + 1) * sub_t] = y.reshape(sub_t, n_batch, D_MODEL)

        _ffn_subblocks(get_h, put_y, n_batch, n_t // sub_t, carry_scr, *weights)
        for cp in y_copies(i, slot):
            cp.start()

    @pl.when(i == n_steps - 1)
    def _():
        for cp in histp_out:
            cp.start()
        for cp in y_copies(i - 1, 1 - slot) + y_copies(i, slot) + histp_out:
            cp.wait()

    @pl.when(i == n_steps)
    def _():
        for cp in hist_in:
            cp.wait()

        def get_h(s):
            return hs_ref[s * SUB_ROWS:(s + 1) * SUB_ROWS, :]

        def put_y(s, y):
            ys_buf[s * SUB_ROWS:(s + 1) * SUB_ROWS, :] = y

        _ffn_subblocks(get_h, put_y, ns_batch, ns_t * ns_batch // SUB_ROWS, carry_s_scr, *weights)
        for cp in hist_out + ys_out:
            cp.start()
        for cp in hist_out + ys_out:
            cp.wait()


def _scan_inplace(u_scr, hst_ref, ar_ref, ai_ref, k, n_batch, n_t):
    re0, im0 = _state_lanes(k)

    def row_chunk(rc, _):
        r0 = pl.multiple_of(rc * SUBLANES, SUBLANES)
        for c in range(HALF_ST // SCAN_LANES):
            lo = c * SCAN_LANES
            re_l = slice(lo, lo + SCAN_LANES)
            im_l = slice(HALF_ST + lo, HALF_ST + lo + SCAN_LANES)
            ar = jnp.broadcast_to(ar_ref[k:k + 1, re_l], (SUBLANES, SCAN_LANES))
            ai = jnp.broadcast_to(ai_ref[k:k + 1, re_l], (SUBLANES, SCAN_LANES))
            hr0 = hst_ref[pl.ds(r0, SUBLANES), re0 + lo:re0 + lo + SCAN_LANES]
            hi0 = hst_ref[pl.ds(r0, SUBLANES), im0 + lo:im0 + lo + SCAN_LANES]

            def step(t, carry):
                hr, hi = carry
                row = pl.multiple_of(t * n_batch + r0, SUBLANES)
                ur = u_scr[pl.ds(row, SUBLANES), re_l]
                ui = u_scr[pl.ds(row, SUBLANES), im_l]
                nhr = ar * hr - ai * hi + ur
                nhi = ar * hi + ai * hr + ui
                u_scr[pl.ds(row, SUBLANES), re_l] = nhr
                u_scr[pl.ds(row, SUBLANES), im_l] = nhi
                return nhr, nhi

            hr, hi = lax.fori_loop(0, n_t, step, (hr0, hi0), unroll=min(n_t, 4))
            hst_ref[pl.ds(r0, SUBLANES), re0 + lo:re0 + lo + SCAN_LANES] = hr
            hst_ref[pl.ds(r0, SUBLANES), im0 + lo:im0 + lo + SCAN_LANES] = hi
        return 0

    lax.fori_loop(0, n_batch // SUBLANES, row_chunk, 0)


def _sample_in_copies(n_batch, n_t, x_hbm, h0_hbm, x_buf, hst_ref, sem):
    return [pltpu.make_async_copy(x_hbm.at[:, t, :], x_buf.at[pl.ds(t * n_batch, n_batch), :], sem.at[t])
            for t in range(n_t)] + [pltpu.make_async_copy(h0_hbm, hst_ref, sem.at[n_t])]


def _mixer_sample_step(n_batch, n_t,
                       x_hbm, h0_hbm, gmix_ref, win_ref, gv_ref, ws_ref, bs_ref,
                       ar_ref, ai_ref, bsub_ref, csub_ref, dskip_ref,
                       wglu_ref, bglu_ref, goa_ref, gob_ref, wout_ref,
                       hout_hbm, st_hbm, v_hbm, x_buf, v_buf, u_scr, hst_ref, sem):
    v_out = [pltpu.make_async_copy(
        v_buf.at[pl.ds(t * n_batch, n_batch), pl.ds(h * A_HEAD_DIM, A_HEAD_DIM)],
        v_hbm.at[:, t, h, :], sem.at[n_t + 1 + t * A_HEADS + h]) for t in range(n_t) for h in range(A_HEADS)]
    for cp in _sample_in_copies(n_batch, n_t, x_hbm, h0_hbm, x_buf, hst_ref, sem):
        cp.wait()
    x = x_buf[...]
    n1 = _rms(x, gmix_ref[...]).astype(BF16)
    proj = _dot(n1, win_ref[...])

    a_heads = []
    for h in range(A_HEADS):
        hs = slice(h * A_HEAD_DIM, (h + 1) * A_HEAD_DIM)
        vn = _rms(proj[:, A_WIDTH + h * A_HEAD_DIM:A_WIDTH + (h + 1) * A_HEAD_DIM], gv_ref[h:h + 1, :])
        v_buf[:, hs] = vn
        slabs = []
        for t in range(n_t):
            s = jnp.broadcast_to(bs_ref[h * n_t + t:h * n_t + t + 1, :], (n_batch, A_HEAD_DIM))
            for src in range(t + 1):
                r = (h * n_t + t) * n_t + src
                s = s + ws_ref[r:r + 1, :] * vn[src * n_batch:(src + 1) * n_batch, :]
            slabs.append(proj[t * n_batch:(t + 1) * n_batch, hs] * s)
        a_heads.append(jnp.concatenate(slabs, axis=0))
    an = _rms(jnp.concatenate(a_heads, axis=-1), goa_ref[...])
    for cp in v_out:
        cp.start()

    xs = proj[:, 2 * A_WIDTH:]
    xs_b = xs.astype(BF16)
    y_parts = []
    for k in range(SSM_HALVES):
        u_scr[...] = _dot(xs_b[:, k * HALF_CH:(k + 1) * HALF_CH], bsub_ref[k])
        _scan_inplace(u_scr, hst_ref, ar_ref, ai_ref, k, n_batch, n_t)
        y_parts.append(_dot(u_scr[...].astype(BF16), csub_ref[k]))
    bn = _glu_norm(jnp.concatenate(y_parts, axis=-1), xs, dskip_ref, wglu_ref, bglu_ref, gob_ref)

    mix = jnp.concatenate([an, bn], axis=-1).astype(BF16)
    x_buf[...] = x + _dot(mix, wout_ref[...])
    n_in_v = n_t + 1 + n_t * A_HEADS
    outs = [pltpu.make_async_copy(x_buf, hout_hbm, sem.at[n_in_v]),
            pltpu.make_async_copy(hst_ref, st_hbm, sem.at[n_in_v + 1])]
    for cp in outs:
        cp.start()
    for cp in v_out + outs:
        cp.wait()


def _mixer_kernel(n_batch, n_steps, ns_batch, ns_t,
                  x_ref, xs_hbm, h0s_hbm, gmix_ref, win_ref, gv_ref, ws_ref, bs_ref,
                  ar_ref, ai_ref, bsub_ref, csub_ref, dskip_ref,
                  wglu_ref, bglu_ref, goa_ref, gob_ref, wout_ref, wss_ref, bss_ref, wfi32_ref, wfo32_ref,
                  hout_hbm, st_ref, wfi16_ref, wfo16_ref, hs_hbm, sts_hbm, v_hbm,
                  hst_ref, h_buf, sem, x_buf, v_buf, u_scr, hsts_ref, sem_s):
    i = pl.program_id(0)
    shared = (ar_ref, ai_ref, bsub_ref, csub_ref, dskip_ref, wglu_ref, bglu_ref, goa_ref, gob_ref, wout_ref)

    @pl.when(i == 0)
    def _():
        for cp in _sample_in_copies(ns_batch, ns_t, xs_hbm, h0s_hbm, x_buf, hsts_ref, sem_s):
            cp.start()
        _mixer_sample_step(ns_batch, ns_t, xs_hbm, h0s_hbm, gmix_ref, win_ref, gv_ref, wss_ref, bss_ref, *shared,
                           hs_hbm, sts_hbm, v_hbm, x_buf, v_buf, u_scr, hsts_ref, sem_s)

    @pl.when(i > 0)
    def _():
        _mixer_prompt_step(n_batch, n_steps, i - 1, x_ref, gmix_ref, win_ref, gv_ref, ws_ref, bs_ref, *shared,
                           wfi32_ref, wfo32_ref, hout_hbm, st_ref, wfi16_ref, wfo16_ref, hst_ref, h_buf, sem)


def _const_spec(shape):
    zeros = (0,) * len(shape)
    return pl.BlockSpec(shape, lambda i: zeros, pipeline_mode=pl.Buffered(1))


def _params():
    return pltpu.CompilerParams(dimension_semantics=("arbitrary",), vmem_limit_bytes=VMEM_LIMIT_BYTES)


def _mixer_consts(p, ws, bs):
    return [p['g_mix'], p['w_in'], p['g_v'], ws, bs, p['ar'], p['ai'],
            p['bsub'], p['csub'], p['d_skip'], p['w_glu'], p['b_glu'],
            p['g_out_a'], p['g_out_b'], p['w_out']]


def _ffn_consts(p, wfi, wfo):
    return [p['g_ffn'], wfi, p['conv_w'], p['conv_b'], wfo, p['g_final']]


def _mixer_call(x, xs, h0s, p, ws, bs, wss, bss, w_ffn_in, w_ffn_out):
    n_batch, n_l, _ = x.shape
    ns_batch, ns_t, _ = xs.shape
    rows_s = ns_batch * ns_t
    n_steps = n_l // CHUNK
    assert n_steps >= 2

    def chunk(i):
        return jnp.maximum(i - 1, 0)

    any_spec = pl.BlockSpec(memory_space=pl.ANY)
    blk = pl.BlockSpec((n_batch, CHUNK, D_MODEL), lambda i: (0, chunk(i), 0))
    st_blk = pl.BlockSpec((n_batch, STATE_W), lambda i: (0, 0))
    wfi_blk = pl.BlockSpec((D_MODEL // n_steps, 2 * D_FF), lambda i: (chunk(i), 0))
    wfo_blk = pl.BlockSpec((D_FF // n_steps, D_MODEL), lambda i: (chunk(i), 0))
    consts = _mixer_consts(p, ws, bs) + [wss, bss]
    return pl.pallas_call(
        functools.partial(_mixer_kernel, n_batch, n_steps, ns_batch, ns_t),
        grid=(n_steps + 1,),
        in_specs=[blk, any_spec, any_spec] + [_const_spec(c.shape) for c in consts] + [wfi_blk, wfo_blk],
        out_specs=[any_spec, st_blk, wfi_blk, wfo_blk, any_spec, any_spec, any_spec],
        out_shape=[jax.ShapeDtypeStruct((n_l, n_batch, D_MODEL), F32),
                   jax.ShapeDtypeStruct((n_batch, STATE_W), F32),
                   jax.ShapeDtypeStruct(w_ffn_in.shape, BF16),
                   jax.ShapeDtypeStruct(w_ffn_out.shape, BF16),
                   jax.ShapeDtypeStruct((rows_s, D_MODEL), F32),
                   jax.ShapeDtypeStruct((ns_batch, STATE_W), F32),
                   jax.ShapeDtypeStruct((ns_batch, ns_t, A_HEADS, A_HEAD_DIM), F32)],
        scratch_shapes=[pltpu.VMEM((n_batch, STATE_W), F32),
                        pltpu.VMEM((2, n_batch, CHUNK, D_MODEL), F32),
                        pltpu.SemaphoreType.DMA((2,)),
                        pltpu.VMEM((rows_s, D_MODEL), F32),
                        pltpu.VMEM((rows_s, A_WIDTH), F32),
                        pltpu.VMEM((rows_s, 2 * HALF_ST), F32),
                        pltpu.VMEM((ns_batch, STATE_W), F32),
                        pltpu.SemaphoreType.DMA((ns_t + 1 + ns_t * A_HEADS + 2,))],
        compiler_params=_params(),
        name="mixer",
    )(x, xs, h0s, *consts, w_ffn_in, w_ffn_out)


def _ffn_call(h_tm, hs_tm, conv_s, p, wfi, wfo):
    n_l, n_batch, _ = h_tm.shape
    ns_batch, n_hist, _ = conv_s.shape
    rows_s = hs_tm.shape[0]
    ns_t = rows_s // ns_batch
    n_t = FFN_PROMPT_STEPS
    n_steps = n_l // n_t
    assert n_steps >= 2
    blk = (n_t, n_batch, D_MODEL)
    weights = _ffn_consts(p, wfi, wfo)
    return pl.pallas_call(
        functools.partial(_ffn_kernel, n_batch, n_t, n_steps, ns_batch, ns_t),
        grid=(n_steps + 1,),
        in_specs=[pl.BlockSpec(blk, lambda i: (jnp.minimum(i, n_steps - 1), 0, 0)),
                  _const_spec(hs_tm.shape), pl.BlockSpec(memory_space=pl.ANY)]
        + [_const_spec(c.shape) for c in weights],
        out_specs=[pl.BlockSpec(memory_space=pl.ANY)] * 4,
        out_shape=[jax.ShapeDtypeStruct((n_batch, n_l, D_MODEL), F32),
                   jax.ShapeDtypeStruct((n_batch, n_hist, 2 * D_FF), F32),
                   jax.ShapeDtypeStruct((ns_batch, ns_t, D_MODEL), F32),
                   jax.ShapeDtypeStruct(conv_s.shape, F32)],
        scratch_shapes=[pltpu.VMEM((n_hist * n_batch, 2 * D_FF), F32),
                        pltpu.VMEM((2,) + blk, F32),
                        pltpu.SemaphoreType.DMA((2,)),
                        pltpu.SemaphoreType.DMA((n_hist,)),
                        pltpu.VMEM((n_hist * ns_batch, 2 * D_FF), F32),
                        pltpu.VMEM((rows_s, D_MODEL), F32),
                        pltpu.SemaphoreType.DMA((n_hist + ns_t,))],
        compiler_params=_params(),
        name="ffn",
    )(h_tm, hs_tm, conv_s, *weights)


def _prep_params(g_mix, w_in, g_v, lam_re, lam_im, log_dt, b_re, b_im, c_re, c_im,
                 d_skip, w_glu, b_glu, g_out_a, g_out_b, w_out, g_ffn, conv_w, conv_b, g_final):
    lr = lam_re.astype(F32)
    li = lam_im.astype(F32)
    dt = jnp.exp(log_dt.astype(F32))[:, None]
    mag = jnp.exp(lr * dt)
    ar = mag * jnp.cos(li * dt)
    ai = mag * jnp.sin(li * dt)
    den = lr * lr + li * li
    fr = ((ar - 1.0) * lr + ai * li) / den
    fi = (ai * lr - (ar - 1.0) * li) / den
    bre = b_re.astype(F32)
    bim = b_im.astype(F32)
    bbr = fr[..., None] * bre - fi[..., None] * bim
    bbi = fr[..., None] * bim + fi[..., None] * bre

    def blockdiag(m, rows_per_group, cols_per_group):
        tiled = jnp.tile(m.reshape(HALF_GROUPS * rows_per_group, cols_per_group), (1, HALF_GROUPS))
        row_g = lax.broadcasted_iota(jnp.int32, tiled.shape, 0) // rows_per_group
        col_g = lax.broadcasted_iota(jnp.int32, tiled.shape, 1) // cols_per_group
        return jnp.where(row_g == col_g, tiled, 0.0)

    def blockdiag_in(m):
        return blockdiag(jnp.transpose(m, (0, 2, 1)), SSM_GROUP, SSM_STATE)

    def blockdiag_out(m):
        return blockdiag(jnp.transpose(m, (0, 2, 1)), SSM_STATE, SSM_GROUP)

    bsub, csub, ars, ais = [], [], [], []
    for k in range(SSM_HALVES):
        gs = slice(k * HALF_GROUPS, (k + 1) * HALF_GROUPS)
        bsub.append(jnp.concatenate([blockdiag_in(bbr[gs]), blockdiag_in(bbi[gs])], axis=1))
        csub.append(jnp.concatenate([blockdiag_out(c_re[gs].astype(F32)),
                                     -blockdiag_out(c_im[gs].astype(F32))], axis=0))
        ars.append(ar[gs].reshape(1, HALF_ST))
        ais.append(ai[gs].reshape(1, HALF_ST))

    return dict(
        g_mix=g_mix.reshape(1, D_MODEL), w_in=w_in.astype(BF16), g_v=g_v,
        ar=jnp.concatenate(ars, axis=0), ai=jnp.concatenate(ais, axis=0),
        bsub=jnp.stack(bsub).astype(BF16), csub=jnp.stack(csub).astype(BF16),
        d_skip=d_skip.reshape(1, SSM_WIDTH), w_glu=w_glu.astype(BF16), b_glu=b_glu.reshape(1, SSM_WIDTH),
        g_out_a=g_out_a.reshape(1, A_WIDTH), g_out_b=g_out_b.reshape(1, SSM_WIDTH),
        w_out=w_out.astype(BF16), g_ffn=g_ffn.reshape(1, D_MODEL),
        conv_w=conv_w, conv_b=conv_b.reshape(1, 2 * D_FF), g_final=g_final.reshape(1, D_MODEL))


def _state_in(s_re, s_im):
    nb = s_re.shape[0]
    return jnp.concatenate([s_re.reshape(nb, ALL_ST), s_im.reshape(nb, ALL_ST)], axis=1)


def _state_out(st):
    nb = st.shape[0]
    return (st[:, :ALL_ST].reshape(nb, SSM_GROUPS, SSM_STATE),
            st[:, ALL_ST:].reshape(nb, SSM_GROUPS, SSM_STATE))


def kernel(x_prompt, x_sample, state_ssm_re, state_ssm_im, state_conv, g_mix, w_in, g_v, w_s, b_s,
           lam_re, lam_im, log_dt, b_re, b_im, c_re, c_im, d_skip, w_glu, b_glu, g_out_a, g_out_b,
           w_out, g_ffn, w_ffn_in, conv_w, conv_b, w_ffn_out, g_final):
    p = _prep_params(g_mix, w_in, g_v, lam_re, lam_im, log_dt, b_re, b_im, c_re, c_im, d_skip,
                     w_glu, b_glu, g_out_a, g_out_b, w_out, g_ffn, conv_w, conv_b, g_final)
    tril = jnp.tril(w_s.astype(F32))

    bs_p = jnp.broadcast_to(jnp.transpose(b_s)[:, :, None], (CHUNK, A_HEADS, A_HEAD_DIM)).reshape(CHUNK, A_WIDTH)
    nls = x_sample.shape[1]
    ws_s = jnp.broadcast_to(tril[:, :nls, :nls].reshape(A_HEADS * nls * nls, 1), (A_HEADS * nls * nls, A_HEAD_DIM))
    bs_s = jnp.broadcast_to(b_s[:, :nls].reshape(A_HEADS * nls, 1), (A_HEADS * nls, A_HEAD_DIM))
    h_p, st_p, wfi, wfo, h_s, st_s, v_s = _mixer_call(
        x_prompt, x_sample, _state_in(state_ssm_re, state_ssm_im), p,
        tril.astype(BF16), bs_p, ws_s, bs_s, w_ffn_in, w_ffn_out)
    re_p, im_p = _state_out(st_p)
    re_s, im_s = _state_out(st_s)

    y_p, conv_p, y_s, conv_s = _ffn_call(h_p, h_s, state_conv, p, wfi, wfo)
    return (y_p, y_s, v_s, re_p, im_p, conv_p, re_s, im_s, conv_s)
```

```python
import functools
import math

import jax
import jax.numpy as jnp
from jax import lax
from jax.experimental import pallas as pl
from jax.experimental.pallas import tpu as pltpu

D_MODEL = 1024
A_HEADS = 4
A_HEAD_DIM = 128
A_WIDTH = 512
CHUNK = 128
SSM_WIDTH = 512
SSM_GROUP = 16
SSM_GROUPS = 32
SSM_STATE = 64
D_FF = 2816
CONV_W = 3
EPS = 1e-6

SSM_HALVES = 2
HALF_GROUPS = SSM_GROUPS // SSM_HALVES
HALF_CH = HALF_GROUPS * SSM_GROUP
HALF_ST = HALF_GROUPS * SSM_STATE
ALL_ST = SSM_HALVES * HALF_ST
STATE_W = 2 * ALL_ST

FF_CHUNK = 256
FF_NCHUNK = D_FF // FF_CHUNK
SUB_ROWS = 256
FFN_PROMPT_STEPS = 128
FFN_GROUP = 2

SUBLANES = 8
SCAN_LANES = 512
VMEM_LIMIT_BYTES = 60 * 1024 * 1024

BF16 = jnp.bfloat16
F32 = jnp.float32


def _state_lanes(k):
    return k * HALF_ST, ALL_ST + k * HALF_ST


def _rms(x, g):
    ms = jnp.mean(x * x, axis=-1, keepdims=True)
    return x * lax.rsqrt(ms + EPS) * g


def _gelu(x):
    c = math.sqrt(2.0 / math.pi)
    return x * (0.5 * (1.0 + jnp.tanh(c * (x + 0.044715 * (x * x * x)))))


def _dot(a, b):
    return jnp.dot(a, b, preferred_element_type=F32)


def _to_time_major(x, n_batch, n_t):
    w = x.shape[-1]
    return jnp.swapaxes(x.reshape(n_batch, n_t, w), 0, 1).reshape(n_t * n_batch, w)


def _to_batch_major(x, n_batch, n_t):
    w = x.shape[-1]
    return jnp.swapaxes(x.reshape(n_t, n_batch, w), 0, 1)


def _glu_norm(ys, xs, dskip_ref, wglu_ref, bglu_ref, gob_ref):
    g = _gelu(ys + dskip_ref[...] * xs)
    z = _dot(g.astype(BF16), wglu_ref[...]) + bglu_ref[...]
    return _rms(g * jax.nn.sigmoid(z), gob_ref[...])


def _ffn_subblocks(get_h, put_y, n_batch, n_sub, carry_scr,
                   gffn_ref, wfi_ref, cw_ref, cb_ref, wfo_ref, gfin_ref):
    halo = (CONV_W - 1) * n_batch
    hs, n2s = {}, {}

    def norm_in(s):
        hs[s] = get_h(s)
        n2s[s] = _rms(hs[s], gffn_ref[...]).astype(BF16)

    def col_slices(j):
        return [slice(c0, c0 + FF_CHUNK) for c0 in (j * FF_CHUNK, D_FF + j * FF_CHUNK)]

    def up_proj(j, s):
        return [jnp.dot(n2s[s], wfi_ref[:, cs], preferred_element_type=F32) for cs in col_slices(j)]

    order = [(j, s) for g0 in range(0, n_sub, FFN_GROUP) for j in range(FF_NCHUNK)
             for s in range(g0, g0 + FFN_GROUP)]
    accs = [jnp.zeros((SUB_ROWS, D_MODEL), F32) for _ in range(n_sub)]
    tails = [None, None]
    norm_in(0)
    ups_next = up_proj(*order[0])
    for s in range(1, FFN_GROUP):
        norm_in(s)
    for idx, (j, s) in enumerate(order):
        ups = ups_next
        if idx + 1 < len(order):
            if order[idx + 1][1] not in hs:
                norm_in(order[idx + 1][1])
            ups_next = up_proj(*order[idx + 1])
        conv = []
        for ci, (cs, up) in enumerate(zip(col_slices(j), ups)):
            tail = carry_scr[:, cs] if s % FFN_GROUP == 0 else tails[ci]
            padded = jnp.concatenate([tail, up], axis=0)
            tails[ci] = padded[SUB_ROWS:SUB_ROWS + halo, :]
            if s % FFN_GROUP == FFN_GROUP - 1:
                carry_scr[:, cs] = tails[ci]
            out = cb_ref[:, cs]
            for tap in range(CONV_W):
                out = out + cw_ref[tap:tap + 1, cs] * padded[tap * n_batch:tap * n_batch + SUB_ROWS, :]
            conv.append(out)
        act = _gelu(conv[0]) * conv[1]
        accs[s] = accs[s] + jnp.dot(act.astype(BF16), wfo_ref[j * FF_CHUNK:(j + 1) * FF_CHUNK, :],
                                    preferred_element_type=F32)
        if j == FF_NCHUNK - 1:
            put_y(s, _rms(hs[s] + accs[s], gfin_ref[...]))


def _mixer_prompt_kernel(n_batch, n_steps,
                         x_ref, gmix_ref, win_ref, gv_ref, ws_ref, bs_ref,
                         ar_ref, ai_ref, bsub_ref, csub_ref, dskip_ref,
                         wglu_ref, bglu_ref, goa_ref, gob_ref, wout_ref, wfi32_ref, wfo32_ref,
                         hout_hbm, st_ref, wfi16_ref, wfo16_ref, hst_ref, h_buf, sem):
    sub_t = SUB_ROWS // n_batch
    n_sub = CHUNK // sub_t
    i = pl.program_id(0)
    slot = lax.rem(i, 2)

    def h_copies(step, buf_slot):
        return [pltpu.make_async_copy(h_buf.at[buf_slot, b], hout_hbm.at[pl.ds(step * CHUNK, CHUNK), b, :],
                                      sem.at[buf_slot]) for b in range(n_batch)]

    @pl.when(i == 0)
    def _():
        hst_ref[...] = jnp.zeros(hst_ref.shape, F32)

    @pl.when(i >= 2)
    def _():
        for cp in h_copies(i - 2, slot):
            cp.wait()

    state, trans = [], []
    for k in range(SSM_HALVES):
        re0, im0 = _state_lanes(k)
        state.append((hst_ref[:, re0:re0 + HALF_ST], hst_ref[:, im0:im0 + HALF_ST]))
        trans.append((jnp.broadcast_to(ar_ref[k:k + 1, :], (n_batch, HALF_ST)),
                      jnp.broadcast_to(ai_ref[k:k + 1, :], (n_batch, HALF_ST))))

    x_sub, proj_sub, xs_sub, y_sub, bn_sub = {}, {}, {}, {}, {}
    a_heads = [[] for _ in range(n_sub)]

    def project(s):
        x = x_ref[:, s * sub_t:(s + 1) * sub_t, :].reshape(SUB_ROWS, D_MODEL)
        n1 = _rms(x, gmix_ref[...]).astype(BF16)
        x_sub[s] = x
        proj_sub[s] = _dot(n1, win_ref[...])
        xs_sub[s] = _to_time_major(proj_sub[s][:, 2 * A_WIDTH:], n_batch, sub_t)
        y_sub[s] = []

    def ssm_half(s, k):
        u = _dot(xs_sub[s].astype(BF16)[:, k * HALF_CH:(k + 1) * HALF_CH], bsub_ref[k])
        ar, ai = trans[k]
        hr, hi = state[k]
        slabs = []
        for t in range(sub_t):
            ur = u[t * n_batch:(t + 1) * n_batch, :HALF_ST]
            ui = u[t * n_batch:(t + 1) * n_batch, HALF_ST:]
            hr, hi = ar * hr - ai * hi + ur, ar * hi + ai * hr + ui
            slabs.append(jnp.concatenate([hr, hi], axis=-1))
        state[k] = (hr, hi)
        y_sub[s].append(_dot(jnp.concatenate(slabs, axis=0).astype(BF16), csub_ref[k]))

    def glu(s):
        bn_tm = _glu_norm(jnp.concatenate(y_sub[s], axis=-1), xs_sub[s], dskip_ref, wglu_ref, bglu_ref,
                          gob_ref)
        bn_sub[s] = _to_batch_major(bn_tm, n_batch, sub_t).reshape(SUB_ROWS, SSM_WIDTH)

    def gate_head(h):
        hs = slice(h * A_HEAD_DIM, (h + 1) * A_HEAD_DIM)
        vn = [_rms(proj_sub[s][:, A_WIDTH + h * A_HEAD_DIM:A_WIDTH + (h + 1) * A_HEAD_DIM],
                   gv_ref[h:h + 1, :]).astype(BF16) for s in range(n_sub)]
        v_wide = jnp.concatenate(
            [jnp.concatenate([vn[s][b * sub_t:(b + 1) * sub_t, :] for s in range(n_sub)], axis=0)
             for b in range(n_batch)], axis=-1)
        s_wide = _dot(ws_ref[h], v_wide)
        for s in range(n_sub):
            ts = slice(s * sub_t, (s + 1) * sub_t)
            bias = bs_ref[ts, hs]
            gate = jnp.concatenate([s_wide[ts, b * A_HEAD_DIM:(b + 1) * A_HEAD_DIM] + bias
                                    for b in range(n_batch)], axis=0)
            a_heads[s].append(proj_sub[s][:, hs] * gate)

    def out_proj(s):
        an = _rms(jnp.concatenate(a_heads[s], axis=-1), goa_ref[...])
        mix = jnp.concatenate([an, bn_sub[s]], axis=-1).astype(BF16)
        h_buf[slot, :, s * sub_t:(s + 1) * sub_t, :] = (
            x_sub[s] + _dot(mix, wout_ref[...])).reshape(n_batch, sub_t, D_MODEL)

    chain = [project] + [functools.partial(ssm_half, k=k) for k in range(SSM_HALVES)] + [glu]
    heads = list(range(A_HEADS))
    outs = list(range(n_sub))
    for wave in range(n_sub + len(chain) - 1):
        if wave >= n_sub:
            for h in heads[:2]:
                gate_head(h)
            heads = heads[2:]
        for s in range(n_sub):
            if 0 <= wave - s < len(chain):
                chain[wave - s](s)
        if not heads and outs and wave - outs[0] >= len(chain) - 1:
            out_proj(outs.pop(0))
    for h in heads:
        gate_head(h)
    for s in outs:
        out_proj(s)

    for k in range(SSM_HALVES):
        re0, im0 = _state_lanes(k)
        hst_ref[:, re0:re0 + HALF_ST] = state[k][0]
        hst_ref[:, im0:im0 + HALF_ST] = state[k][1]

    wfi16_ref[...] = wfi32_ref[...].astype(BF16)
    wfo16_ref[...] = wfo32_ref[...].astype(BF16)
    for cp in h_copies(i, slot):
        cp.start()

    @pl.when(i == n_steps - 1)
    def _():
        st_ref[...] = hst_ref[...]
        for cp in h_copies(i - 1, 1 - slot) + h_copies(i, slot):
            cp.wait()


def _ffn_kernel(n_batch, n_t, n_steps, ns_batch, ns_t,
                h_ref, hs_ref, convs_hbm, gffn_ref, wfi_ref, cw_ref, cb_ref, wfo_ref, gfin_ref,
                y_hbm, convp_out_hbm, ys_hbm, convs_out_hbm,
                carry_scr, y_buf, sem, sem_p, carry_s_scr, ys_buf, sem_s):
    sub_t = SUB_ROWS // n_batch
    i = pl.program_id(0)
    slot = lax.rem(i, 2)
    weights = (gffn_ref, wfi_ref, cw_ref, cb_ref, wfo_ref, gfin_ref)
    n_hist = CONV_W - 1

    def y_copies(step, buf_slot):
        return [pltpu.make_async_copy(y_buf.at[buf_slot, :, b, :], y_hbm.at[b, pl.ds(step * n_t, n_t), :],
                                      sem.at[buf_slot]) for b in range(n_batch)]

    histp_out = [pltpu.make_async_copy(carry_scr.at[pl.ds(k * n_batch, n_batch), :], convp_out_hbm.at[:, k, :],
                                       sem_p.at[k]) for k in range(n_hist)]
    hist_in = [pltpu.make_async_copy(convs_hbm.at[:, k, :], carry_s_scr.at[pl.ds(k * ns_batch, ns_batch), :],
                                     sem_s.at[k]) for k in range(n_hist)]
    hist_out = [pltpu.make_async_copy(carry_s_scr.at[pl.ds(k * ns_batch, ns_batch), :], convs_out_hbm.at[:, k, :],
                                      sem_s.at[k]) for k in range(n_hist)]
    ys_out = [pltpu.make_async_copy(ys_buf.at[pl.ds(t * ns_batch, ns_batch), :], ys_hbm.at[:, t, :],
                                    sem_s.at[n_hist + t]) for t in range(ns_t)]

    @pl.when(i == 0)
    def _():
        carry_scr[...] = jnp.zeros(carry_scr.shape, F32)
        for cp in hist_in:
            cp.start()

    @pl.when(jnp.logical_and(i >= 2, i < n_steps))
    def _():
        for cp in y_copies(i - 2, slot):
            cp.wait()

    @pl.when(i < n_steps)
    def _():
        def get_h(s):
            return h_ref[s * sub_t:(s + 1) * sub_t].reshape(SUB_ROWS, D_MODEL)

        def put_y(s, y):
            y_buf[slot, s * sub_t:(s + 1) * sub_t] = y.reshape(sub_t, n_batch, D_MODEL)

        _ffn_subblocks(get_h, put_y, n_batch, n_t // sub_t, carry_scr, *weights)
        for cp in y_copies(i, slot):
            cp.start()

    @pl.when(i == n_steps - 1)
    def _():
        for cp in histp_out:
            cp.start()
        for cp in y_copies(i - 1, 1 - slot) + y_copies(i, slot) + histp_out:
            cp.wait()

    @pl.when(i == n_steps)
    def _():
        for cp in hist_in:
            cp.wait()

        def get_h(s):
            return hs_ref[s * SUB_ROWS:(s + 1) * SUB_ROWS, :]

        def put_y(s, y):
            ys_buf[s * SUB_ROWS:(s + 1) * SUB_ROWS, :] = y

        _ffn_subblocks(get_h, put_y, ns_batch, ns_t * ns_batch // SUB_ROWS, carry_s_scr, *weights)
        for cp in hist_out + ys_out:
            cp.start()
        for cp in hist_out + ys_out:
            cp.wait()


def _scan_inplace(u_scr, hst_ref, ar_ref, ai_ref, k, n_batch, n_t):
    re0, im0 = _state_lanes(k)

    def row_chunk(rc, _):
        r0 = pl.multiple_of(rc * SUBLANES, SUBLANES)
        for c in range(HALF_ST // SCAN_LANES):
            lo = c * SCAN_LANES
            re_l = slice(lo, lo + SCAN_LANES)
            im_l = slice(HALF_ST + lo, HALF_ST + lo + SCAN_LANES)
            ar = jnp.broadcast_to(ar_ref[k:k + 1, re_l], (SUBLANES, SCAN_LANES))
            ai = jnp.broadcast_to(ai_ref[k:k + 1, re_l], (SUBLANES, SCAN_LANES))
            hr0 = hst_ref[pl.ds(r0, SUBLANES), re0 + lo:re0 + lo + SCAN_LANES]
            hi0 = hst_ref[pl.ds(r0, SUBLANES), im0 + lo:im0 + lo + SCAN_LANES]

            def step(t, carry):
                hr, hi = carry
                row = pl.multiple_of(t * n_batch + r0, SUBLANES)
                ur = u_scr[pl.ds(row, SUBLANES), re_l]
                ui = u_scr[pl.ds(row, SUBLANES), im_l]
                nhr = ar * hr - ai * hi + ur
                nhi = ar * hi + ai * hr + ui
                u_scr[pl.ds(row, SUBLANES), re_l] = nhr
                u_scr[pl.ds(row, SUBLANES), im_l] = nhi
                return nhr, nhi

            hr, hi = lax.fori_loop(0, n_t, step, (hr0, hi0), unroll=min(n_t, 4))
            hst_ref[pl.ds(r0, SUBLANES), re0 + lo:re0 + lo + SCAN_LANES] = hr
            hst_ref[pl.ds(r0, SUBLANES), im0 + lo:im0 + lo + SCAN_LANES] = hi
        return 0

    lax.fori_loop(0, n_batch // SUBLANES, row_chunk, 0)


def _mixer_sample_kernel(n_batch, n_t,
                         x_ref, h0_ref, gmix_ref, win_ref, gv_ref, ws_ref, bs_ref,
                         ar_ref, ai_ref, bsub_ref, csub_ref, dskip_ref,
                         wglu_ref, bglu_ref, goa_ref, gob_ref, wout_ref,
                         hout_ref, st_ref, v_hbm, x_buf, v_buf, u_scr, hst_ref, sem):
    x_in = [pltpu.make_async_copy(x_ref.at[:, t, :], x_buf.at[pl.ds(t * n_batch, n_batch), :], sem.at[t])
            for t in range(n_t)]
    v_out = [pltpu.make_async_copy(
        v_buf.at[pl.ds(t * n_batch, n_batch), pl.ds(h * A_HEAD_DIM, A_HEAD_DIM)],
        v_hbm.at[:, t, h, :], sem.at[n_t + t * A_HEADS + h]) for t in range(n_t) for h in range(A_HEADS)]
    for cp in x_in:
        cp.start()
    hst_ref[...] = h0_ref[...]
    for cp in x_in:
        cp.wait()
    x = x_buf[...]
    n1 = _rms(x, gmix_ref[...]).astype(BF16)
    proj = _dot(n1, win_ref[...])

    a_heads = []
    for h in range(A_HEADS):
        hs = slice(h * A_HEAD_DIM, (h + 1) * A_HEAD_DIM)
        vn = _rms(proj[:, A_WIDTH + h * A_HEAD_DIM:A_WIDTH + (h + 1) * A_HEAD_DIM], gv_ref[h:h + 1, :])
        v_buf[:, hs] = vn
        slabs = []
        for t in range(n_t):
            s = jnp.broadcast_to(bs_ref[h * n_t + t:h * n_t + t + 1, :], (n_batch, A_HEAD_DIM))
            for src in range(t + 1):
                r = (h * n_t + t) * n_t + src
                s = s + ws_ref[r:r + 1, :] * vn[src * n_batch:(src + 1) * n_batch, :]
            slabs.append(proj[t * n_batch:(t + 1) * n_batch, hs] * s)
        a_heads.append(jnp.concatenate(slabs, axis=0))
    an = _rms(jnp.concatenate(a_heads, axis=-1), goa_ref[...])
    for cp in v_out:
        cp.start()

    xs = proj[:, 2 * A_WIDTH:]
    xs_b = xs.astype(BF16)
    y_parts = []
    for k in range(SSM_HALVES):
        u_scr[...] = _dot(xs_b[:, k * HALF_CH:(k + 1) * HALF_CH], bsub_ref[k])
        _scan_inplace(u_scr, hst_ref, ar_ref, ai_ref, k, n_batch, n_t)
        y_parts.append(_dot(u_scr[...].astype(BF16), csub_ref[k]))
    bn = _glu_norm(jnp.concatenate(y_parts, axis=-1), xs, dskip_ref, wglu_ref, bglu_ref, gob_ref)

    mix = jnp.concatenate([an, bn], axis=-1).astype(BF16)
    hout_ref[...] = x + _dot(mix, wout_ref[...])
    st_ref[...] = hst_ref[...]
    for cp in v_out:
        cp.wait()


def _const_spec(shape):
    zeros = (0,) * len(shape)
    return pl.BlockSpec(shape, lambda i: zeros, pipeline_mode=pl.Buffered(1))


def _params():
    return pltpu.CompilerParams(dimension_semantics=("arbitrary",), vmem_limit_bytes=VMEM_LIMIT_BYTES)


def _mixer_consts(p, ws, bs):
    return [p['g_mix'], p['w_in'], p['g_v'], ws, bs, p['ar'], p['ai'],
            p['bsub'], p['csub'], p['d_skip'], p['w_glu'], p['b_glu'],
            p['g_out_a'], p['g_out_b'], p['w_out']]


def _ffn_consts(p, wfi, wfo):
    return [p['g_ffn'], wfi, p['conv_w'], p['conv_b'], wfo, p['g_final']]


def _mixer_prompt_call(x, p, ws, bs, w_ffn_in, w_ffn_out):
    n_batch, n_l, _ = x.shape
    n_steps = n_l // CHUNK
    assert n_steps >= 2
    blk = pl.BlockSpec((n_batch, CHUNK, D_MODEL), lambda i: (0, i, 0))
    st_blk = pl.BlockSpec((n_batch, STATE_W), lambda i: (0, 0))
    wfi_blk = pl.BlockSpec((D_MODEL // n_steps, 2 * D_FF), lambda i: (i, 0))
    wfo_blk = pl.BlockSpec((D_FF // n_steps, D_MODEL), lambda i: (i, 0))
    consts = _mixer_consts(p, ws, bs)
    return pl.pallas_call(
        functools.partial(_mixer_prompt_kernel, n_batch, n_steps),
        grid=(n_steps,),
        in_specs=[blk] + [_const_spec(c.shape) for c in consts] + [wfi_blk, wfo_blk],
        out_specs=[pl.BlockSpec(memory_space=pl.ANY), st_blk, wfi_blk, wfo_blk],
        out_shape=[jax.ShapeDtypeStruct((n_l, n_batch, D_MODEL), F32),
                   jax.ShapeDtypeStruct((n_batch, STATE_W), F32),
                   jax.ShapeDtypeStruct(w_ffn_in.shape, BF16),
                   jax.ShapeDtypeStruct(w_ffn_out.shape, BF16)],
        scratch_shapes=[pltpu.VMEM((n_batch, STATE_W), F32),
                        pltpu.VMEM((2, n_batch, CHUNK, D_MODEL), F32),
                        pltpu.SemaphoreType.DMA((2,))],
        compiler_params=_params(),
        name="mixer_prompt",
    )(x, *consts, w_ffn_in, w_ffn_out)


def _ffn_call(h_tm, hs_tm, conv_s, p, wfi, wfo):
    n_l, n_batch, _ = h_tm.shape
    ns_batch, n_hist, _ = conv_s.shape
    rows_s = hs_tm.shape[0]
    ns_t = rows_s // ns_batch
    n_t = FFN_PROMPT_STEPS
    n_steps = n_l // n_t
    assert n_steps >= 2
    blk = (n_t, n_batch, D_MODEL)
    weights = _ffn_consts(p, wfi, wfo)
    return pl.pallas_call(
        functools.partial(_ffn_kernel, n_batch, n_t, n_steps, ns_batch, ns_t),
        grid=(n_steps + 1,),
        in_specs=[pl.BlockSpec(blk, lambda i: (jnp.minimum(i, n_steps - 1), 0, 0)),
                  _const_spec(hs_tm.shape), pl.BlockSpec(memory_space=pl.ANY)]
        + [_const_spec(c.shape) for c in weights],
        out_specs=[pl.BlockSpec(memory_space=pl.ANY)] * 4,
        out_shape=[jax.ShapeDtypeStruct((n_batch, n_l, D_MODEL), F32),
                   jax.ShapeDtypeStruct((n_batch, n_hist, 2 * D_FF), F32),
                   jax.ShapeDtypeStruct((ns_batch, ns_t, D_MODEL), F32),
                   jax.ShapeDtypeStruct(conv_s.shape, F32)],
        scratch_shapes=[pltpu.VMEM((n_hist * n_batch, 2 * D_FF), F32),
                        pltpu.VMEM((2,) + blk, F32),
                        pltpu.SemaphoreType.DMA((2,)),
                        pltpu.SemaphoreType.DMA((n_hist,)),
                        pltpu.VMEM((n_hist * ns_batch, 2 * D_FF), F32),
                        pltpu.VMEM((rows_s, D_MODEL), F32),
                        pltpu.SemaphoreType.DMA((n_hist + ns_t,))],
        compiler_params=_params(),
        name="ffn",
    )(h_tm, hs_tm, conv_s, *weights)


def _mixer_sample_call(x, h0, p, ws, bs):
    n_batch, n_t, _ = x.shape
    rows = n_batch * n_t
    consts = [h0] + _mixer_consts(p, ws, bs)
    st_blk = pl.BlockSpec((n_batch, STATE_W), lambda i: (0, 0))
    return pl.pallas_call(
        functools.partial(_mixer_sample_kernel, n_batch, n_t),
        grid=(1,),
        in_specs=[pl.BlockSpec(memory_space=pl.ANY)] + [_const_spec(c.shape) for c in consts],
        out_specs=[pl.BlockSpec((rows, D_MODEL), lambda i: (0, 0)), st_blk,
                   pl.BlockSpec(memory_space=pl.ANY)],
        out_shape=[jax.ShapeDtypeStruct((rows, D_MODEL), F32),
                   jax.ShapeDtypeStruct((n_batch, STATE_W), F32),
                   jax.ShapeDtypeStruct((n_batch, n_t, A_HEADS, A_HEAD_DIM), F32)],
        scratch_shapes=[pltpu.VMEM((rows, D_MODEL), F32),
                        pltpu.VMEM((rows, A_WIDTH), F32),
                        pltpu.VMEM((rows, 2 * HALF_ST), F32),
                        pltpu.VMEM((n_batch, STATE_W), F32),
                        pltpu.SemaphoreType.DMA((n_t + n_t * A_HEADS,))],
        compiler_params=_params(),
        name="mixer_sample",
    )(x, *consts)


def _prep_params(g_mix, w_in, g_v, lam_re, lam_im, log_dt, b_re, b_im, c_re, c_im,
                 d_skip, w_glu, b_glu, g_out_a, g_out_b, w_out, g_ffn, conv_w, conv_b, g_final):
    lr = lam_re.astype(F32)
    li = lam_im.astype(F32)
    dt = jnp.exp(log_dt.astype(F32))[:, None]
    mag = jnp.exp(lr * dt)
    ar = mag * jnp.cos(li * dt)
    ai = mag * jnp.sin(li * dt)
    den = lr * lr + li * li
    fr = ((ar - 1.0) * lr + ai * li) / den
    fi = (ai * lr - (ar - 1.0) * li) / den
    bre = b_re.astype(F32)
    bim = b_im.astype(F32)
    bbr = fr[..., None] * bre - fi[..., None] * bim
    bbi = fr[..., None] * bim + fi[..., None] * bre

    def blockdiag(m, rows_per_group, cols_per_group):
        tiled = jnp.tile(m.reshape(HALF_GROUPS * rows_per_group, cols_per_group), (1, HALF_GROUPS))
        row_g = lax.broadcasted_iota(jnp.int32, tiled.shape, 0) // rows_per_group
        col_g = lax.broadcasted_iota(jnp.int32, tiled.shape, 1) // cols_per_group
        return jnp.where(row_g == col_g, tiled, 0.0)

    def blockdiag_in(m):
        return blockdiag(jnp.transpose(m, (0, 2, 1)), SSM_GROUP, SSM_STATE)

    def blockdiag_out(m):
        return blockdiag(jnp.transpose(m, (0, 2, 1)), SSM_STATE, SSM_GROUP)

    bsub, csub, ars, ais = [], [], [], []
    for k in range(SSM_HALVES):
        gs = slice(k * HALF_GROUPS, (k + 1) * HALF_GROUPS)
        bsub.append(jnp.concatenate([blockdiag_in(bbr[gs]), blockdiag_in(bbi[gs])], axis=1))
        csub.append(jnp.concatenate([blockdiag_out(c_re[gs].astype(F32)),
                                     -blockdiag_out(c_im[gs].astype(F32))], axis=0))
        ars.append(ar[gs].reshape(1, HALF_ST))
        ais.append(ai[gs].reshape(1, HALF_ST))

    return dict(
        g_mix=g_mix.reshape(1, D_MODEL), w_in=w_in.astype(BF16), g_v=g_v,
        ar=jnp.concatenate(ars, axis=0), ai=jnp.concatenate(ais, axis=0),
        bsub=jnp.stack(bsub).astype(BF16), csub=jnp.stack(csub).astype(BF16),
        d_skip=d_skip.reshape(1, SSM_WIDTH), w_glu=w_glu.astype(BF16), b_glu=b_glu.reshape(1, SSM_WIDTH),
        g_out_a=g_out_a.reshape(1, A_WIDTH), g_out_b=g_out_b.reshape(1, SSM_WIDTH),
        w_out=w_out.astype(BF16), g_ffn=g_ffn.reshape(1, D_MODEL),
        conv_w=conv_w, conv_b=conv_b.reshape(1, 2 * D_FF), g_final=g_final.reshape(1, D_MODEL))


def _state_in(s_re, s_im):
    nb = s_re.shape[0]
    return jnp.concatenate([s_re.reshape(nb, ALL_ST), s_im.reshape(nb, ALL_ST)], axis=1)


def _state_out(st):
    nb = st.shape[0]
    return (st[:, :ALL_ST].reshape(nb, SSM_GROUPS, SSM_STATE),
            st[:, ALL_ST:].reshape(nb, SSM_GROUPS, SSM_STATE))


def kernel(x_prompt, x_sample, state_ssm_re, state_ssm_im, state_conv, g_mix, w_in, g_v, w_s, b_s,
           lam_re, lam_im, log_dt, b_re, b_im, c_re, c_im, d_skip, w_glu, b_glu, g_out_a, g_out_b,
           w_out, g_ffn, w_ffn_in, conv_w, conv_b, w_ffn_out, g_final):
    p = _prep_params(g_mix, w_in, g_v, lam_re, lam_im, log_dt, b_re, b_im, c_re, c_im, d_skip,
                     w_glu, b_glu, g_out_a, g_out_b, w_out, g_ffn, conv_w, conv_b, g_final)
    tril = jnp.tril(w_s.astype(F32))

    bs_p = jnp.broadcast_to(jnp.transpose(b_s)[:, :, None], (CHUNK, A_HEADS, A_HEAD_DIM)).reshape(CHUNK, A_WIDTH)
    h_p, st_p, wfi, wfo = _mixer_prompt_call(x_prompt, p, tril.astype(BF16), bs_p, w_ffn_in, w_ffn_out)
    re_p, im_p = _state_out(st_p)

    nls = x_sample.shape[1]
    ws_s = jnp.broadcast_to(tril[:, :nls, :nls].reshape(A_HEADS * nls * nls, 1), (A_HEADS * nls * nls, A_HEAD_DIM))
    bs_s = jnp.broadcast_to(b_s[:, :nls].reshape(A_HEADS * nls, 1), (A_HEADS * nls, A_HEAD_DIM))
    h_s, st_s, v_s = _mixer_sample_call(x_sample, _state_in(state_ssm_re, state_ssm_im), p, ws_s, bs_s)
    re_s, im_s = _state_out(st_s)

    y_p, conv_p, y_s, conv_s = _ffn_call(h_p, h_s, state_conv, p, wfi, wfo)
    return (y_p, y_s, v_s, re_p, im_p, conv_p, re_s, im_s, conv_s)
```

```python
import functools
import math

import jax
import jax.numpy as jnp
from jax import lax
from jax.experimental import pallas as pl
from jax.experimental.pallas import tpu as pltpu

D_MODEL = 1024
A_HEADS = 4
A_HEAD_DIM = 128
A_WIDTH = 512
CHUNK = 128
SSM_WIDTH = 512
SSM_GROUP = 16
SSM_GROUPS = 32
SSM_STATE = 64
D_FF = 2816
CONV_W = 3
EPS = 1e-6

SSM_HALVES = 2
HALF_GROUPS = SSM_GROUPS // SSM_HALVES
HALF_CH = HALF_GROUPS * SSM_GROUP
HALF_ST = HALF_GROUPS * SSM_STATE
ALL_ST = SSM_HALVES * HALF_ST
STATE_W = 2 * ALL_ST

FF_CHUNK = 256
FF_NCHUNK = D_FF // FF_CHUNK
SUB_ROWS = 256
FFN_PROMPT_STEPS = 64

SUBLANES = 8
SCAN_LANES = 512
VMEM_LIMIT_BYTES = 60 * 1024 * 1024

BF16 = jnp.bfloat16
F32 = jnp.float32


def _state_lanes(k):
    return k * HALF_ST, ALL_ST + k * HALF_ST


def _rms(x, g):
    ms = jnp.mean(x * x, axis=-1, keepdims=True)
    return x * lax.rsqrt(ms + EPS) * g


def _gelu(x):
    c = math.sqrt(2.0 / math.pi)
    return x * (0.5 * (1.0 + jnp.tanh(c * (x + 0.044715 * (x * x * x)))))


def _dot(a, b):
    return jnp.dot(a, b, preferred_element_type=F32)


def _to_time_major(x, n_batch, n_t):
    w = x.shape[-1]
    return jnp.swapaxes(x.reshape(n_batch, n_t, w), 0, 1).reshape(n_t * n_batch, w)


def _to_batch_major(x, n_batch, n_t):
    w = x.shape[-1]
    return jnp.swapaxes(x.reshape(n_t, n_batch, w), 0, 1)


def _glu_norm(ys, xs, dskip_ref, wglu_ref, bglu_ref, gob_ref):
    g = _gelu(ys + dskip_ref[...] * xs)
    z = _dot(g.astype(BF16), wglu_ref[...]) + bglu_ref[...]
    return _rms(g * jax.nn.sigmoid(z), gob_ref[...])


def _ffn_subblocks(get_h, put_y, n_batch, n_sub, carry_scr,
                   gffn_ref, wfi_ref, cw_ref, cb_ref, wfo_ref, gfin_ref):
    halo = (CONV_W - 1) * n_batch
    hs, n2s = {}, {}

    def norm_in(s):
        hs[s] = get_h(s)
        n2s[s] = _rms(hs[s], gffn_ref[...]).astype(BF16)

    def col_slices(j):
        return [slice(c0, c0 + FF_CHUNK) for c0 in (j * FF_CHUNK, D_FF + j * FF_CHUNK)]

    def up_proj(j, s):
        return [jnp.dot(n2s[s], wfi_ref[:, cs], preferred_element_type=F32) for cs in col_slices(j)]

    order = [(j, s) for j in range(FF_NCHUNK) for s in range(n_sub)]
    accs = [jnp.zeros((SUB_ROWS, D_MODEL), F32) for _ in range(n_sub)]
    tails = [None, None]
    norm_in(0)
    ups_next = up_proj(*order[0])
    for s in range(1, n_sub):
        norm_in(s)
    for idx, (j, s) in enumerate(order):
        ups = ups_next
        if idx + 1 < len(order):
            ups_next = up_proj(*order[idx + 1])
        conv = []
        for ci, (cs, up) in enumerate(zip(col_slices(j), ups)):
            tail = carry_scr[:, cs] if s == 0 else tails[ci]
            padded = jnp.concatenate([tail, up], axis=0)
            tails[ci] = padded[SUB_ROWS:SUB_ROWS + halo, :]
            if s == n_sub - 1:
                carry_scr[:, cs] = tails[ci]
            out = cb_ref[:, cs]
            for tap in range(CONV_W):
                out = out + cw_ref[tap:tap + 1, cs] * padded[tap * n_batch:tap * n_batch + SUB_ROWS, :]
            conv.append(out)
        act = _gelu(conv[0]) * conv[1]
        accs[s] = accs[s] + jnp.dot(act.astype(BF16), wfo_ref[j * FF_CHUNK:(j + 1) * FF_CHUNK, :],
                                    preferred_element_type=F32)
        if j == FF_NCHUNK - 1:
            put_y(s, _rms(hs[s] + accs[s], gfin_ref[...]))


def _mixer_prompt_kernel(n_batch, n_steps,
                         x_ref, gmix_ref, win32_hbm, gv_ref, ws_ref, bs_ref,
                         ar_ref, ai_ref, bsub_ref, csub_ref, dskip_ref,
                         wglu32_hbm, bglu_ref, goa_ref, gob_ref, wout32_hbm, wfi32_ref, wfo32_ref,
                         hout_hbm, st_ref, wfi16_ref, wfo16_ref, win16_hbm, wglu16_hbm, wout16_hbm,
                         hst_ref, h_buf, sem, win_ref, wglu_ref, wout_ref, stage_in, stage_glu, stage_out, sem_w):
    sub_t = SUB_ROWS // n_batch
    n_sub = CHUNK // sub_t
    i = pl.program_id(0)
    slot = lax.rem(i, 2)

    def h_copies(step, buf_slot):
        return [pltpu.make_async_copy(h_buf.at[buf_slot, b], hout_hbm.at[pl.ds(step * CHUNK, CHUNK), b, :],
                                      sem.at[buf_slot]) for b in range(n_batch)]

    rounded = (win_ref, wglu_ref, wout_ref)
    stages = (stage_in, stage_glu, stage_out)
    w_loads = [pltpu.make_async_copy(src, dst, sem_w.at[n])
               for n, (src, dst) in enumerate(zip((win32_hbm, wglu32_hbm, wout32_hbm), stages))]
    w_stores = [pltpu.make_async_copy(src, dst, sem_w.at[len(stages) + n])
                for n, (src, dst) in enumerate(zip(rounded, (win16_hbm, wglu16_hbm, wout16_hbm)))]

    @pl.when(i == 0)
    def _():
        for cp in w_loads:
            cp.start()
        hst_ref[...] = jnp.zeros(hst_ref.shape, F32)
        for cp, stage, w16, out in zip(w_loads, stages, rounded, w_stores):
            cp.wait()
            w16[...] = stage[...].astype(BF16)
            out.start()

    @pl.when(i >= 2)
    def _():
        for cp in h_copies(i - 2, slot):
            cp.wait()

    state, trans = [], []
    for k in range(SSM_HALVES):
        re0, im0 = _state_lanes(k)
        state.append((hst_ref[:, re0:re0 + HALF_ST], hst_ref[:, im0:im0 + HALF_ST]))
        trans.append((jnp.broadcast_to(ar_ref[k:k + 1, :], (n_batch, HALF_ST)),
                      jnp.broadcast_to(ai_ref[k:k + 1, :], (n_batch, HALF_ST))))

    x_sub, proj_sub, xs_sub, y_sub, bn_sub = {}, {}, {}, {}, {}
    a_heads = [[] for _ in range(n_sub)]

    def project(s):
        x = x_ref[:, s * sub_t:(s + 1) * sub_t, :].reshape(SUB_ROWS, D_MODEL)
        n1 = _rms(x, gmix_ref[...]).astype(BF16)
        x_sub[s] = x
        proj_sub[s] = _dot(n1, win_ref[...])
        xs_sub[s] = _to_time_major(proj_sub[s][:, 2 * A_WIDTH:], n_batch, sub_t)
        y_sub[s] = []

    def ssm_half(s, k):
        u = _dot(xs_sub[s].astype(BF16)[:, k * HALF_CH:(k + 1) * HALF_CH], bsub_ref[k])
        ar, ai = trans[k]
        hr, hi = state[k]
        slabs = []
        for t in range(sub_t):
            ur = u[t * n_batch:(t + 1) * n_batch, :HALF_ST]
            ui = u[t * n_batch:(t + 1) * n_batch, HALF_ST:]
            hr, hi = ar * hr - ai * hi + ur, ar * hi + ai * hr + ui
            slabs.append(jnp.concatenate([hr, hi], axis=-1))
        state[k] = (hr, hi)
        y_sub[s].append(_dot(jnp.concatenate(slabs, axis=0).astype(BF16), csub_ref[k]))

    def glu(s):
        bn_tm = _glu_norm(jnp.concatenate(y_sub[s], axis=-1), xs_sub[s], dskip_ref, wglu_ref, bglu_ref,
                          gob_ref)
        bn_sub[s] = _to_batch_major(bn_tm, n_batch, sub_t).reshape(SUB_ROWS, SSM_WIDTH)

    def gate_head(h):
        hs = slice(h * A_HEAD_DIM, (h + 1) * A_HEAD_DIM)
        vn = [_rms(proj_sub[s][:, A_WIDTH + h * A_HEAD_DIM:A_WIDTH + (h + 1) * A_HEAD_DIM],
                   gv_ref[h:h + 1, :]).astype(BF16) for s in range(n_sub)]
        v_wide = jnp.concatenate(
            [jnp.concatenate([vn[s][b * sub_t:(b + 1) * sub_t, :] for s in range(n_sub)], axis=0)
             for b in range(n_batch)], axis=-1)
        s_wide = _dot(ws_ref[h], v_wide)
        for s in range(n_sub):
            ts = slice(s * sub_t, (s + 1) * sub_t)
            bias = bs_ref[ts, hs]
            gate = jnp.concatenate([s_wide[ts, b * A_HEAD_DIM:(b + 1) * A_HEAD_DIM] + bias
                                    for b in range(n_batch)], axis=0)
            a_heads[s].append(proj_sub[s][:, hs] * gate)

    def out_proj(s):
        an = _rms(jnp.concatenate(a_heads[s], axis=-1), goa_ref[...])
        mix = jnp.concatenate([an, bn_sub[s]], axis=-1).astype(BF16)
        h_buf[slot, :, s * sub_t:(s + 1) * sub_t, :] = (
            x_sub[s] + _dot(mix, wout_ref[...])).reshape(n_batch, sub_t, D_MODEL)

    chain = [project] + [functools.partial(ssm_half, k=k) for k in range(SSM_HALVES)] + [glu]
    heads = list(range(A_HEADS))
    outs = list(range(n_sub))
    for wave in range(n_sub + len(chain) - 1):
        if wave >= n_sub:
            for h in heads[:2]:
                gate_head(h)
            heads = heads[2:]
        for s in range(n_sub):
            if 0 <= wave - s < len(chain):
                chain[wave - s](s)
        if not heads and outs and wave - outs[0] >= len(chain) - 1:
            out_proj(outs.pop(0))
    for h in heads:
        gate_head(h)
    for s in outs:
        out_proj(s)

    for k in range(SSM_HALVES):
        re0, im0 = _state_lanes(k)
        hst_ref[:, re0:re0 + HALF_ST] = state[k][0]
        hst_ref[:, im0:im0 + HALF_ST] = state[k][1]

    wfi16_ref[...] = wfi32_ref[...].astype(BF16)
    wfo16_ref[...] = wfo32_ref[...].astype(BF16)
    for cp in h_copies(i, slot):
        cp.start()

    @pl.when(i == n_steps - 1)
    def _():
        st_ref[...] = hst_ref[...]
        for cp in h_copies(i - 1, 1 - slot) + h_copies(i, slot) + w_stores:
            cp.wait()


def _ffn_kernel(n_batch, n_t, n_steps, ns_batch, ns_t,
                h_ref, hs_ref, convs_hbm, gffn_ref, wfi_ref, cw_ref, cb_ref, wfo_ref, gfin_ref,
                y_hbm, convp_out_hbm, ys_hbm, convs_out_hbm,
                carry_scr, y_buf, sem, sem_p, carry_s_scr, ys_buf, sem_s):
    sub_t = SUB_ROWS // n_batch
    i = pl.program_id(0)
    slot = lax.rem(i, 2)
    weights = (gffn_ref, wfi_ref, cw_ref, cb_ref, wfo_ref, gfin_ref)
    n_hist = CONV_W - 1

    def y_copies(step, buf_slot):
        return [pltpu.make_async_copy(y_buf.at[buf_slot, :, b, :], y_hbm.at[b, pl.ds(step * n_t, n_t), :],
                                      sem.at[buf_slot]) for b in range(n_batch)]

    histp_out = [pltpu.make_async_copy(carry_scr.at[pl.ds(k * n_batch, n_batch), :], convp_out_hbm.at[:, k, :],
                                       sem_p.at[k]) for k in range(n_hist)]
    hist_in = [pltpu.make_async_copy(convs_hbm.at[:, k, :], carry_s_scr.at[pl.ds(k * ns_batch, ns_batch), :],
                                     sem_s.at[k]) for k in range(n_hist)]
    hist_out = [pltpu.make_async_copy(carry_s_scr.at[pl.ds(k * ns_batch, ns_batch), :], convs_out_hbm.at[:, k, :],
                                      sem_s.at[k]) for k in range(n_hist)]
    ys_out = [pltpu.make_async_copy(ys_buf.at[pl.ds(t * ns_batch, ns_batch), :], ys_hbm.at[:, t, :],
                                    sem_s.at[n_hist + t]) for t in range(ns_t)]

    @pl.when(i == 0)
    def _():
        carry_scr[...] = jnp.zeros(carry_scr.shape, F32)
        for cp in hist_in:
            cp.start()

    @pl.when(jnp.logical_and(i >= 2, i < n_steps))
    def _():
        for cp in y_copies(i - 2, slot):
            cp.wait()

    @pl.when(i < n_steps)
    def _():
        def get_h(s):
            return h_ref[s * sub_t:(s + 1) * sub_t].reshape(SUB_ROWS, D_MODEL)

        def put_y(s, y):
            y_buf[slot, s * sub_t:(s + 1) * sub_t] = y.reshape(sub_t, n_batch, D_MODEL)

        _ffn_subblocks(get_h, put_y, n_batch, n_t // sub_t, carry_scr, *weights)
        for cp in y_copies(i, slot):
            cp.start()

    @pl.when(i == n_steps - 1)
    def _():
        for cp in histp_out:
            cp.start()
        for cp in y_copies(i - 1, 1 - slot) + y_copies(i, slot) + histp_out:
            cp.wait()

    @pl.when(i == n_steps)
    def _():
        for cp in hist_in:
            cp.wait()

        def get_h(s):
            return hs_ref[s * SUB_ROWS:(s + 1) * SUB_ROWS, :]

        def put_y(s, y):
            ys_buf[s * SUB_ROWS:(s + 1) * SUB_ROWS, :] = y

        _ffn_subblocks(get_h, put_y, ns_batch, ns_t * ns_batch // SUB_ROWS, carry_s_scr, *weights)
        for cp in hist_out + ys_out:
            cp.start()
        for cp in hist_out + ys_out:
            cp.wait()


def _scan_inplace(u_scr, hst_ref, ar_ref, ai_ref, k, n_batch, n_t):
    re0, im0 = _state_lanes(k)

    def row_chunk(rc, _):
        r0 = pl.multiple_of(rc * SUBLANES, SUBLANES)
        for c in range(HALF_ST // SCAN_LANES):
            lo = c * SCAN_LANES
            re_l = slice(lo, lo + SCAN_LANES)
            im_l = slice(HALF_ST + lo, HALF_ST + lo + SCAN_LANES)
            ar = jnp.broadcast_to(ar_ref[k:k + 1, re_l], (SUBLANES, SCAN_LANES))
            ai = jnp.broadcast_to(ai_ref[k:k + 1, re_l], (SUBLANES, SCAN_LANES))
            hr0 = hst_ref[pl.ds(r0, SUBLANES), re0 + lo:re0 + lo + SCAN_LANES]
            hi0 = hst_ref[pl.ds(r0, SUBLANES), im0 + lo:im0 + lo + SCAN_LANES]

            def step(t, carry):
                hr, hi = carry
                row = pl.multiple_of(t * n_batch + r0, SUBLANES)
                ur = u_scr[pl.ds(row, SUBLANES), re_l]
                ui = u_scr[pl.ds(row, SUBLANES), im_l]
                nhr = ar * hr - ai * hi + ur
                nhi = ar * hi + ai * hr + ui
                u_scr[pl.ds(row, SUBLANES), re_l] = nhr
                u_scr[pl.ds(row, SUBLANES), im_l] = nhi
                return nhr, nhi

            hr, hi = lax.fori_loop(0, n_t, step, (hr0, hi0), unroll=min(n_t, 4))
            hst_ref[pl.ds(r0, SUBLANES), re0 + lo:re0 + lo + SCAN_LANES] = hr
            hst_ref[pl.ds(r0, SUBLANES), im0 + lo:im0 + lo + SCAN_LANES] = hi
        return 0

    lax.fori_loop(0, n_batch // SUBLANES, row_chunk, 0)


def _mixer_sample_kernel(n_batch, n_t,
                         x_ref, h0_ref, gmix_ref, win_ref, gv_ref, ws_ref, bs_ref,
                         ar_ref, ai_ref, bsub_ref, csub_ref, dskip_ref,
                         wglu_ref, bglu_ref, goa_ref, gob_ref, wout_ref,
                         hout_ref, st_ref, v_hbm, x_buf, v_buf, u_scr, hst_ref, sem):
    x_in = [pltpu.make_async_copy(x_ref.at[:, t, :], x_buf.at[pl.ds(t * n_batch, n_batch), :], sem.at[t])
            for t in range(n_t)]
    v_out = [pltpu.make_async_copy(
        v_buf.at[pl.ds(t * n_batch, n_batch), pl.ds(h * A_HEAD_DIM, A_HEAD_DIM)],
        v_hbm.at[:, t, h, :], sem.at[n_t + t * A_HEADS + h]) for t in range(n_t) for h in range(A_HEADS)]
    for cp in x_in:
        cp.start()
    hst_ref[...] = h0_ref[...]
    for cp in x_in:
        cp.wait()
    x = x_buf[...]
    n1 = _rms(x, gmix_ref[...]).astype(BF16)
    proj = _dot(n1, win_ref[...])

    a_heads = []
    for h in range(A_HEADS):
        hs = slice(h * A_HEAD_DIM, (h + 1) * A_HEAD_DIM)
        vn = _rms(proj[:, A_WIDTH + h * A_HEAD_DIM:A_WIDTH + (h + 1) * A_HEAD_DIM], gv_ref[h:h + 1, :])
        v_buf[:, hs] = vn
        slabs = []
        for t in range(n_t):
            s = jnp.broadcast_to(bs_ref[h * n_t + t:h * n_t + t + 1, :], (n_batch, A_HEAD_DIM))
            for src in range(t + 1):
                r = (h * n_t + t) * n_t + src
                s = s + ws_ref[r:r + 1, :] * vn[src * n_batch:(src + 1) * n_batch, :]
            slabs.append(proj[t * n_batch:(t + 1) * n_batch, hs] * s)
        a_heads.append(jnp.concatenate(slabs, axis=0))
    an = _rms(jnp.concatenate(a_heads, axis=-1), goa_ref[...])
    for cp in v_out:
        cp.start()

    xs = proj[:, 2 * A_WIDTH:]
    xs_b = xs.astype(BF16)
    y_parts = []
    for k in range(SSM_HALVES):
        u_scr[...] = _dot(xs_b[:, k * HALF_CH:(k + 1) * HALF_CH], bsub_ref[k])
        _scan_inplace(u_scr, hst_ref, ar_ref, ai_ref, k, n_batch, n_t)
        y_parts.append(_dot(u_scr[...].astype(BF16), csub_ref[k]))
    bn = _glu_norm(jnp.concatenate(y_parts, axis=-1), xs, dskip_ref, wglu_ref, bglu_ref, gob_ref)

    mix = jnp.concatenate([an, bn], axis=-1).astype(BF16)
    hout_ref[...] = x + _dot(mix, wout_ref[...])
    st_ref[...] = hst_ref[...]
    for cp in v_out:
        cp.wait()


def _const_spec(shape):
    zeros = (0,) * len(shape)
    return pl.BlockSpec(shape, lambda i: zeros, pipeline_mode=pl.Buffered(1))


def _params():
    return pltpu.CompilerParams(dimension_semantics=("arbitrary",), vmem_limit_bytes=VMEM_LIMIT_BYTES)


def _mixer_consts(p, ws, bs):
    return [p['g_mix'], p['w_in'], p['g_v'], ws, bs, p['ar'], p['ai'],
            p['bsub'], p['csub'], p['d_skip'], p['w_glu'], p['b_glu'],
            p['g_out_a'], p['g_out_b'], p['w_out']]


def _ffn_consts(p, wfi, wfo):
    return [p['g_ffn'], wfi, p['conv_w'], p['conv_b'], wfo, p['g_final']]


def _mixer_prompt_call(x, p, ws, bs, w_ffn_in, w_ffn_out):
    n_batch, n_l, _ = x.shape
    n_steps = n_l // CHUNK
    assert n_steps >= 2
    any_spec = pl.BlockSpec(memory_space=pl.ANY)
    blk = pl.BlockSpec((n_batch, CHUNK, D_MODEL), lambda i: (0, i, 0))
    st_blk = pl.BlockSpec((n_batch, STATE_W), lambda i: (0, 0))
    wfi_blk = pl.BlockSpec((D_MODEL // n_steps, 2 * D_FF), lambda i: (i, 0))
    wfo_blk = pl.BlockSpec((D_FF // n_steps, D_MODEL), lambda i: (i, 0))
    consts = _mixer_consts(p, ws, bs)
    w32 = [p['w_in'], p['w_glu'], p['w_out']]
    return pl.pallas_call(
        functools.partial(_mixer_prompt_kernel, n_batch, n_steps),
        grid=(n_steps,),
        in_specs=[blk] + [any_spec if any(c is w for w in w32) else _const_spec(c.shape) for c in consts]
        + [wfi_blk, wfo_blk],
        out_specs=[any_spec, st_blk, wfi_blk, wfo_blk] + [any_spec] * len(w32),
        out_shape=[jax.ShapeDtypeStruct((n_l, n_batch, D_MODEL), F32),
                   jax.ShapeDtypeStruct((n_batch, STATE_W), F32),
                   jax.ShapeDtypeStruct(w_ffn_in.shape, BF16),
                   jax.ShapeDtypeStruct(w_ffn_out.shape, BF16)]
        + [jax.ShapeDtypeStruct(w.shape, BF16) for w in w32],
        scratch_shapes=[pltpu.VMEM((n_batch, STATE_W), F32),
                        pltpu.VMEM((2, n_batch, CHUNK, D_MODEL), F32),
                        pltpu.SemaphoreType.DMA((2,))]
        + [pltpu.VMEM(w.shape, BF16) for w in w32] + [pltpu.VMEM(w.shape, F32) for w in w32]
        + [pltpu.SemaphoreType.DMA((2 * len(w32),))],
        compiler_params=_params(),
        name="mixer_prompt",
    )(x, *consts, w_ffn_in, w_ffn_out)


def _ffn_call(h_tm, hs_tm, conv_s, p, wfi, wfo):
    n_l, n_batch, _ = h_tm.shape
    ns_batch, n_hist, _ = conv_s.shape
    rows_s = hs_tm.shape[0]
    ns_t = rows_s // ns_batch
    n_t = FFN_PROMPT_STEPS
    n_steps = n_l // n_t
    assert n_steps >= 2
    blk = (n_t, n_batch, D_MODEL)
    weights = _ffn_consts(p, wfi, wfo)
    return pl.pallas_call(
        functools.partial(_ffn_kernel, n_batch, n_t, n_steps, ns_batch, ns_t),
        grid=(n_steps + 1,),
        in_specs=[pl.BlockSpec(blk, lambda i: (jnp.minimum(i, n_steps - 1), 0, 0)),
                  _const_spec(hs_tm.shape), pl.BlockSpec(memory_space=pl.ANY)]
        + [_const_spec(c.shape) for c in weights],
        out_specs=[pl.BlockSpec(memory_space=pl.ANY)] * 4,
        out_shape=[jax.ShapeDtypeStruct((n_batch, n_l, D_MODEL), F32),
                   jax.ShapeDtypeStruct((n_batch, n_hist, 2 * D_FF), F32),
                   jax.ShapeDtypeStruct((ns_batch, ns_t, D_MODEL), F32),
                   jax.ShapeDtypeStruct(conv_s.shape, F32)],
        scratch_shapes=[pltpu.VMEM((n_hist * n_batch, 2 * D_FF), F32),
                        pltpu.VMEM((2,) + blk, F32),
                        pltpu.SemaphoreType.DMA((2,)),
                        pltpu.SemaphoreType.DMA((n_hist,)),
                        pltpu.VMEM((n_hist * ns_batch, 2 * D_FF), F32),
                        pltpu.VMEM((rows_s, D_MODEL), F32),
                        pltpu.SemaphoreType.DMA((n_hist + ns_t,))],
        compiler_params=_params(),
        name="ffn",
    )(h_tm, hs_tm, conv_s, *weights)


def _mixer_sample_call(x, h0, p, ws, bs):
    n_batch, n_t, _ = x.shape
    rows = n_batch * n_t
    consts = [h0] + _mixer_consts(p, ws, bs)
    st_blk = pl.BlockSpec((n_batch, STATE_W), lambda i: (0, 0))
    return pl.pallas_call(
        functools.partial(_mixer_sample_kernel, n_batch, n_t),
        grid=(1,),
        in_specs=[pl.BlockSpec(memory_space=pl.ANY)] + [_const_spec(c.shape) for c in consts],
        out_specs=[pl.BlockSpec((rows, D_MODEL), lambda i: (0, 0)), st_blk,
                   pl.BlockSpec(memory_space=pl.ANY)],
        out_shape=[jax.ShapeDtypeStruct((rows, D_MODEL), F32),
                   jax.ShapeDtypeStruct((n_batch, STATE_W), F32),
                   jax.ShapeDtypeStruct((n_batch, n_t, A_HEADS, A_HEAD_DIM), F32)],
        scratch_shapes=[pltpu.VMEM((rows, D_MODEL), F32),
                        pltpu.VMEM((rows, A_WIDTH), F32),
                        pltpu.VMEM((rows, 2 * HALF_ST), F32),
                        pltpu.VMEM((n_batch, STATE_W), F32),
                        pltpu.SemaphoreType.DMA((n_t + n_t * A_HEADS,))],
        compiler_params=_params(),
        name="mixer_sample",
    )(x, *consts)


def _prep_params(g_mix, w_in, g_v, lam_re, lam_im, log_dt, b_re, b_im, c_re, c_im,
                 d_skip, w_glu, b_glu, g_out_a, g_out_b, w_out, g_ffn, conv_w, conv_b, g_final):
    lr = lam_re.astype(F32)
    li = lam_im.astype(F32)
    dt = jnp.exp(log_dt.astype(F32))[:, None]
    mag = jnp.exp(lr * dt)
    ar = mag * jnp.cos(li * dt)
    ai = mag * jnp.sin(li * dt)
    den = lr * lr + li * li
    fr = ((ar - 1.0) * lr + ai * li) / den
    fi = (ai * lr - (ar - 1.0) * li) / den
    bre = b_re.astype(F32)
    bim = b_im.astype(F32)
    bbr = fr[..., None] * bre - fi[..., None] * bim
    bbi = fr[..., None] * bim + fi[..., None] * bre

    def blockdiag(m, rows_per_group, cols_per_group):
        tiled = jnp.tile(m.reshape(HALF_GROUPS * rows_per_group, cols_per_group), (1, HALF_GROUPS))
        row_g = lax.broadcasted_iota(jnp.int32, tiled.shape, 0) // rows_per_group
        col_g = lax.broadcasted_iota(jnp.int32, tiled.shape, 1) // cols_per_group
        return jnp.where(row_g == col_g, tiled, 0.0)

    def blockdiag_in(m):
        return blockdiag(jnp.transpose(m, (0, 2, 1)), SSM_GROUP, SSM_STATE)

    def blockdiag_out(m):
        return blockdiag(jnp.transpose(m, (0, 2, 1)), SSM_STATE, SSM_GROUP)

    bsub, csub, ars, ais = [], [], [], []
    for k in range(SSM_HALVES):
        gs = slice(k * HALF_GROUPS, (k + 1) * HALF_GROUPS)
        bsub.append(jnp.concatenate([blockdiag_in(bbr[gs]), blockdiag_in(bbi[gs])], axis=1))
        csub.append(jnp.concatenate([blockdiag_out(c_re[gs].astype(F32)),
                                     -blockdiag_out(c_im[gs].astype(F32))], axis=0))
        ars.append(ar[gs].reshape(1, HALF_ST))
        ais.append(ai[gs].reshape(1, HALF_ST))

    return dict(
        g_mix=g_mix.reshape(1, D_MODEL), w_in=w_in.astype(F32), g_v=g_v,
        ar=jnp.concatenate(ars, axis=0), ai=jnp.concatenate(ais, axis=0),
        bsub=jnp.stack(bsub).astype(BF16), csub=jnp.stack(csub).astype(BF16),
        d_skip=d_skip.reshape(1, SSM_WIDTH), w_glu=w_glu.astype(F32), b_glu=b_glu.reshape(1, SSM_WIDTH),
        g_out_a=g_out_a.reshape(1, A_WIDTH), g_out_b=g_out_b.reshape(1, SSM_WIDTH),
        w_out=w_out.astype(F32), g_ffn=g_ffn.reshape(1, D_MODEL),
        conv_w=conv_w, conv_b=conv_b.reshape(1, 2 * D_FF), g_final=g_final.reshape(1, D_MODEL))


def _state_in(s_re, s_im):
    nb = s_re.shape[0]
    return jnp.concatenate([s_re.reshape(nb, ALL_ST), s_im.reshape(nb, ALL_ST)], axis=1)


def _state_out(st):
    nb = st.shape[0]
    return (st[:, :ALL_ST].reshape(nb, SSM_GROUPS, SSM_STATE),
            st[:, ALL_ST:].reshape(nb, SSM_GROUPS, SSM_STATE))


def kernel(x_prompt, x_sample, state_ssm_re, state_ssm_im, state_conv, g_mix, w_in, g_v, w_s, b_s,
           lam_re, lam_im, log_dt, b_re, b_im, c_re, c_im, d_skip, w_glu, b_glu, g_out_a, g_out_b,
           w_out, g_ffn, w_ffn_in, conv_w, conv_b, w_ffn_out, g_final):
    p = _prep_params(g_mix, w_in, g_v, lam_re, lam_im, log_dt, b_re, b_im, c_re, c_im, d_skip,
                     w_glu, b_glu, g_out_a, g_out_b, w_out, g_ffn, conv_w, conv_b, g_final)
    tril = jnp.tril(w_s.astype(F32))

    bs_p = jnp.broadcast_to(jnp.transpose(b_s)[:, :, None], (CHUNK, A_HEADS, A_HEAD_DIM)).reshape(CHUNK, A_WIDTH)
    h_p, st_p, wfi, wfo, w_in16, w_glu16, w_out16 = _mixer_prompt_call(
        x_prompt, p, tril.astype(BF16), bs_p, w_ffn_in, w_ffn_out)
    p = dict(p, w_in=w_in16, w_glu=w_glu16, w_out=w_out16)
    re_p, im_p = _state_out(st_p)

    nls = x_sample.shape[1]
    ws_s = jnp.broadcast_to(tril[:, :nls, :nls].reshape(A_HEADS * nls * nls, 1), (A_HEADS * nls * nls, A_HEAD_DIM))
    bs_s = jnp.broadcast_to(b_s[:, :nls].reshape(A_HEADS * nls, 1), (A_HEADS * nls, A_HEAD_DIM))
    h_s, st_s, v_s = _mixer_sample_call(x_sample, _state_in(state_ssm_re, state_ssm_im), p, ws_s, bs_s)
    re_s, im_s = _state_out(st_s)

    y_p, conv_p, y_s, conv_s = _ffn_call(h_p, h_s, state_conv, p, wfi, wfo)
    return (y_p, y_s, v_s, re_p, im_p, conv_p, re_s, im_s, conv_s)
```

```python
import functools
import math

import jax
import jax.numpy as jnp
from jax import lax
from jax.experimental import pallas as pl
from jax.experimental.pallas import tpu as pltpu

D_MODEL = 1024
A_HEADS = 4
A_HEAD_DIM = 128
A_WIDTH = 512
CHUNK = 128
SSM_WIDTH = 512
SSM_GROUP = 16
SSM_GROUPS = 32
SSM_STATE = 64
D_FF = 2816
CONV_W = 3
EPS = 1e-6

SSM_HALVES = 2
HALF_GROUPS = SSM_GROUPS // SSM_HALVES
HALF_CH = HALF_GROUPS * SSM_GROUP
HALF_ST = HALF_GROUPS * SSM_STATE
ALL_ST = SSM_HALVES * HALF_ST
STATE_W = 2 * ALL_ST

FF_CHUNK = 256
FF_NCHUNK = D_FF // FF_CHUNK
SUB_ROWS = 256
FFN_PROMPT_STEPS = 64

SUBLANES = 8
SCAN_LANES = 512
VMEM_LIMIT_BYTES = 60 * 1024 * 1024

BF16 = jnp.bfloat16
F32 = jnp.float32


def _state_lanes(k):
    return k * HALF_ST, ALL_ST + k * HALF_ST


def _rms(x, g):
    ms = jnp.mean(x * x, axis=-1, keepdims=True)
    return x * lax.rsqrt(ms + EPS) * g


def _gelu(x):
    c = math.sqrt(2.0 / math.pi)
    return x * (0.5 * (1.0 + jnp.tanh(c * (x + 0.044715 * (x * x * x)))))


def _dot(a, b):
    return jnp.dot(a, b, preferred_element_type=F32)


def _to_time_major(x, n_batch, n_t):
    w = x.shape[-1]
    return jnp.swapaxes(x.reshape(n_batch, n_t, w), 0, 1).reshape(n_t * n_batch, w)


def _to_batch_major(x, n_batch, n_t):
    w = x.shape[-1]
    return jnp.swapaxes(x.reshape(n_t, n_batch, w), 0, 1)


def _glu_norm(ys, xs, dskip_ref, wglu_ref, bglu_ref, gob_ref):
    g = _gelu(ys + dskip_ref[...] * xs)
    z = _dot(g.astype(BF16), wglu_ref[...]) + bglu_ref[...]
    return _rms(g * jax.nn.sigmoid(z), gob_ref[...])


def _ffn_subblocks(get_h, finish, after_first_up, n_batch, n_sub, carry_scr,
                   gffn_ref, wfi_ref, cw_ref, cb_ref, wfo_ref):
    halo = (CONV_W - 1) * n_batch
    hs, n2s = {}, {}

    def norm_in(s):
        hs[s] = get_h(s)
        n2s[s] = _rms(hs[s], gffn_ref[...]).astype(BF16)

    def col_slices(j):
        return [slice(c0, c0 + FF_CHUNK) for c0 in (j * FF_CHUNK, D_FF + j * FF_CHUNK)]

    def up_proj(j, s):
        return [jnp.dot(n2s[s], wfi_ref[:, cs], preferred_element_type=F32) for cs in col_slices(j)]

    order = [(j, s) for j in range(FF_NCHUNK) for s in range(n_sub)]
    accs = [jnp.zeros((SUB_ROWS, D_MODEL), F32) for _ in range(n_sub)]
    tails = [None, None]
    norm_in(0)
    ups_next = up_proj(*order[0])
    for s in range(1, n_sub):
        norm_in(s)
    for idx, (j, s) in enumerate(order):
        ups = ups_next
        if idx + 1 < len(order):
            ups_next = up_proj(*order[idx + 1])
        if idx == 0:
            after_first_up()
        conv = []
        for ci, (cs, up) in enumerate(zip(col_slices(j), ups)):
            tail = carry_scr[:, cs] if s == 0 else tails[ci]
            padded = jnp.concatenate([tail, up], axis=0)
            tails[ci] = padded[SUB_ROWS:SUB_ROWS + halo, :]
            if s == n_sub - 1:
                carry_scr[:, cs] = tails[ci]
            out = cb_ref[:, cs]
            for tap in range(CONV_W):
                out = out + cw_ref[tap:tap + 1, cs] * padded[tap * n_batch:tap * n_batch + SUB_ROWS, :]
            conv.append(out)
        act = _gelu(conv[0]) * conv[1]
        accs[s] = accs[s] + jnp.dot(act.astype(BF16), wfo_ref[j * FF_CHUNK:(j + 1) * FF_CHUNK, :],
                                    preferred_element_type=F32)
        if j == FF_NCHUNK - 1:
            finish(s, hs[s] + accs[s])


def _mixer_prompt_kernel(n_batch, n_steps,
                         x_ref, gmix_ref, win32_hbm, gv_ref, ws_ref, bs_ref,
                         ar_ref, ai_ref, bsub_ref, csub_ref, dskip_ref,
                         wglu32_hbm, bglu_ref, goa_ref, gob_ref, wout32_hbm, wfi32_ref, wfo32_ref,
                         hout_hbm, st_ref, wfi16_ref, wfo16_ref, win16_hbm, wglu16_hbm, wout16_hbm,
                         hst_ref, h_buf, sem, win_ref, wglu_ref, wout_ref, stage_in, stage_glu, stage_out, sem_w):
    sub_t = SUB_ROWS // n_batch
    n_sub = CHUNK // sub_t
    i = pl.program_id(0)
    slot = lax.rem(i, 2)

    def h_copies(step, buf_slot):
        return [pltpu.make_async_copy(h_buf.at[buf_slot, b], hout_hbm.at[pl.ds(step * CHUNK, CHUNK), b, :],
                                      sem.at[buf_slot]) for b in range(n_batch)]

    rounded = (win_ref, wglu_ref, wout_ref)
    stages = (stage_in, stage_glu, stage_out)
    w_loads = [pltpu.make_async_copy(src, dst, sem_w.at[n])
               for n, (src, dst) in enumerate(zip((win32_hbm, wglu32_hbm, wout32_hbm), stages))]
    w_stores = [pltpu.make_async_copy(src, dst, sem_w.at[len(stages) + n])
                for n, (src, dst) in enumerate(zip(rounded, (win16_hbm, wglu16_hbm, wout16_hbm)))]

    @pl.when(i == 0)
    def _():
        for cp in w_loads:
            cp.start()
        hst_ref[...] = jnp.zeros(hst_ref.shape, F32)
        for cp, stage, w16, out in zip(w_loads, stages, rounded, w_stores):
            cp.wait()
            w16[...] = stage[...].astype(BF16)
            out.start()

    @pl.when(i >= 2)
    def _():
        for cp in h_copies(i - 2, slot):
            cp.wait()

    state, trans = [], []
    for k in range(SSM_HALVES):
        re0, im0 = _state_lanes(k)
        state.append((hst_ref[:, re0:re0 + HALF_ST], hst_ref[:, im0:im0 + HALF_ST]))
        trans.append((jnp.broadcast_to(ar_ref[k:k + 1, :], (n_batch, HALF_ST)),
                      jnp.broadcast_to(ai_ref[k:k + 1, :], (n_batch, HALF_ST))))

    x_sub, proj_sub, xs_sub, y_sub, bn_sub = {}, {}, {}, {}, {}
    a_heads = [[] for _ in range(n_sub)]

    def project(s):
        x = x_ref[:, s * sub_t:(s + 1) * sub_t, :].reshape(SUB_ROWS, D_MODEL)
        n1 = _rms(x, gmix_ref[...]).astype(BF16)
        x_sub[s] = x
        proj_sub[s] = _dot(n1, win_ref[...])
        xs_sub[s] = _to_time_major(proj_sub[s][:, 2 * A_WIDTH:], n_batch, sub_t)
        y_sub[s] = []

    def ssm_half(s, k):
        u = _dot(xs_sub[s].astype(BF16)[:, k * HALF_CH:(k + 1) * HALF_CH], bsub_ref[k])
        ar, ai = trans[k]
        hr, hi = state[k]
        slabs = []
        for t in range(sub_t):
            ur = u[t * n_batch:(t + 1) * n_batch, :HALF_ST]
            ui = u[t * n_batch:(t + 1) * n_batch, HALF_ST:]
            hr, hi = ar * hr - ai * hi + ur, ar * hi + ai * hr + ui
            slabs.append(jnp.concatenate([hr, hi], axis=-1))
        state[k] = (hr, hi)
        y_sub[s].append(_dot(jnp.concatenate(slabs, axis=0).astype(BF16), csub_ref[k]))

    def glu(s):
        bn_tm = _glu_norm(jnp.concatenate(y_sub[s], axis=-1), xs_sub[s], dskip_ref, wglu_ref, bglu_ref,
                          gob_ref)
        bn_sub[s] = _to_batch_major(bn_tm, n_batch, sub_t).reshape(SUB_ROWS, SSM_WIDTH)

    def gate_head(h):
        hs = slice(h * A_HEAD_DIM, (h + 1) * A_HEAD_DIM)
        vn = [_rms(proj_sub[s][:, A_WIDTH + h * A_HEAD_DIM:A_WIDTH + (h + 1) * A_HEAD_DIM],
                   gv_ref[h:h + 1, :]).astype(BF16) for s in range(n_sub)]
        v_wide = jnp.concatenate(
            [jnp.concatenate([vn[s][b * sub_t:(b + 1) * sub_t, :] for s in range(n_sub)], axis=0)
             for b in range(n_batch)], axis=-1)
        s_wide = _dot(ws_ref[h], v_wide)
        for s in range(n_sub):
            ts = slice(s * sub_t, (s + 1) * sub_t)
            bias = bs_ref[ts, hs]
            gate = jnp.concatenate([s_wide[ts, b * A_HEAD_DIM:(b + 1) * A_HEAD_DIM] + bias
                                    for b in range(n_batch)], axis=0)
            a_heads[s].append(proj_sub[s][:, hs] * gate)

    def out_proj(s):
        an = _rms(jnp.concatenate(a_heads[s], axis=-1), goa_ref[...])
        mix = jnp.concatenate([an, bn_sub[s]], axis=-1).astype(BF16)
        h_buf[slot, :, s * sub_t:(s + 1) * sub_t, :] = (
            x_sub[s] + _dot(mix, wout_ref[...])).reshape(n_batch, sub_t, D_MODEL)

    chain = [project] + [functools.partial(ssm_half, k=k) for k in range(SSM_HALVES)] + [glu]
    heads = list(range(A_HEADS))
    outs = list(range(n_sub))
    for wave in range(n_sub + len(chain) - 1):
        if wave >= n_sub:
            for h in heads[:2]:
                gate_head(h)
            heads = heads[2:]
        for s in range(n_sub):
            if 0 <= wave - s < len(chain):
                chain[wave - s](s)
        if not heads and outs and wave - outs[0] >= len(chain) - 1:
            out_proj(outs.pop(0))
    for h in heads:
        gate_head(h)
    for s in outs:
        out_proj(s)

    for k in range(SSM_HALVES):
        re0, im0 = _state_lanes(k)
        hst_ref[:, re0:re0 + HALF_ST] = state[k][0]
        hst_ref[:, im0:im0 + HALF_ST] = state[k][1]

    wfi16_ref[...] = wfi32_ref[...].astype(BF16)
    wfo16_ref[...] = wfo32_ref[...].astype(BF16)
    for cp in h_copies(i, slot):
        cp.start()

    @pl.when(i == n_steps - 1)
    def _():
        st_ref[...] = hst_ref[...]
        for cp in h_copies(i - 1, 1 - slot) + h_copies(i, slot) + w_stores:
            cp.wait()


def _ffn_kernel(n_batch, n_t, n_steps, ns_batch, ns_t,
                h_ref, hs_ref, convs_hbm, gffn_ref, wfi_ref, cw_ref, cb_ref, wfo_ref, gfin_ref,
                y_hbm, convp_out_hbm, ys_hbm, convs_out_hbm,
                carry_scr, y_buf, sem, sem_p, carry_s_scr, ys_buf, sem_s, pre_scr):
    sub_t = SUB_ROWS // n_batch
    n_sub = n_t // sub_t
    i = pl.program_id(0)
    slot = lax.rem(i, 2)
    weights = (gffn_ref, wfi_ref, cw_ref, cb_ref, wfo_ref)
    n_hist = CONV_W - 1

    def y_copies(step, buf_slot):
        blk = lax.rem(step + n_steps - 1, n_steps)
        return [pltpu.make_async_copy(y_buf.at[buf_slot, :, b, :], y_hbm.at[b, pl.ds(blk * n_t, n_t), :],
                                      sem.at[buf_slot]) for b in range(n_batch)]

    def emit_prev_block():
        for s in range(n_sub):
            y = _rms(pre_scr[s], gfin_ref[...])
            y_buf[slot, s * sub_t:(s + 1) * sub_t] = y.reshape(sub_t, n_batch, D_MODEL)
        for cp in y_copies(i, slot):
            cp.start()

    histp_out = [pltpu.make_async_copy(carry_scr.at[pl.ds(k * n_batch, n_batch), :], convp_out_hbm.at[:, k, :],
                                       sem_p.at[k]) for k in range(n_hist)]
    hist_in = [pltpu.make_async_copy(convs_hbm.at[:, k, :], carry_s_scr.at[pl.ds(k * ns_batch, ns_batch), :],
                                     sem_s.at[k]) for k in range(n_hist)]
    hist_out = [pltpu.make_async_copy(carry_s_scr.at[pl.ds(k * ns_batch, ns_batch), :], convs_out_hbm.at[:, k, :],
                                      sem_s.at[k]) for k in range(n_hist)]
    ys_out = [pltpu.make_async_copy(ys_buf.at[pl.ds(t * ns_batch, ns_batch), :], ys_hbm.at[:, t, :],
                                    sem_s.at[n_hist + t]) for t in range(ns_t)]

    @pl.when(i == 0)
    def _():
        carry_scr[...] = jnp.zeros(carry_scr.shape, F32)
        pre_scr[...] = jnp.zeros(pre_scr.shape, F32)
        for cp in hist_in:
            cp.start()

    @pl.when(i >= 2)
    def _():
        for cp in y_copies(i - 2, slot):
            cp.wait()

    @pl.when(i < n_steps)
    def _():
        def get_h(s):
            return h_ref[s * sub_t:(s + 1) * sub_t].reshape(SUB_ROWS, D_MODEL)

        def stash(s, pre):
            pre_scr[s] = pre

        _ffn_subblocks(get_h, stash, emit_prev_block, n_batch, n_sub, carry_scr, *weights)

    @pl.when(i == n_steps - 1)
    def _():
        for cp in histp_out:
            cp.start()
        for cp in histp_out:
            cp.wait()

    @pl.when(i == n_steps)
    def _():
        for cp in hist_in:
            cp.wait()

        def get_h(s):
            return hs_ref[s * SUB_ROWS:(s + 1) * SUB_ROWS, :]

        def put_y(s, pre):
            ys_buf[s * SUB_ROWS:(s + 1) * SUB_ROWS, :] = _rms(pre, gfin_ref[...])

        _ffn_subblocks(get_h, put_y, emit_prev_block, ns_batch, ns_t * ns_batch // SUB_ROWS, carry_s_scr,
                       *weights)
        for cp in hist_out + ys_out:
            cp.start()
        for cp in hist_out + ys_out + y_copies(i - 1, 1 - slot) + y_copies(i, slot):
            cp.wait()


def _scan_inplace(u_scr, hst_ref, ar_ref, ai_ref, k, n_batch, n_t):
    re0, im0 = _state_lanes(k)

    def row_chunk(rc, _):
        r0 = pl.multiple_of(rc * SUBLANES, SUBLANES)
        for c in range(HALF_ST // SCAN_LANES):
            lo = c * SCAN_LANES
            re_l = slice(lo, lo + SCAN_LANES)
            im_l = slice(HALF_ST + lo, HALF_ST + lo + SCAN_LANES)
            ar = jnp.broadcast_to(ar_ref[k:k + 1, re_l], (SUBLANES, SCAN_LANES))
            ai = jnp.broadcast_to(ai_ref[k:k + 1, re_l], (SUBLANES, SCAN_LANES))
            hr0 = hst_ref[pl.ds(r0, SUBLANES), re0 + lo:re0 + lo + SCAN_LANES]
            hi0 = hst_ref[pl.ds(r0, SUBLANES), im0 + lo:im0 + lo + SCAN_LANES]

            def step(t, carry):
                hr, hi = carry
                row = pl.multiple_of(t * n_batch + r0, SUBLANES)
                ur = u_scr[pl.ds(row, SUBLANES), re_l]
                ui = u_scr[pl.ds(row, SUBLANES), im_l]
                nhr = ar * hr - ai * hi + ur
                nhi = ar * hi + ai * hr + ui
                u_scr[pl.ds(row, SUBLANES), re_l] = nhr
                u_scr[pl.ds(row, SUBLANES), im_l] = nhi
                return nhr, nhi

            hr, hi = lax.fori_loop(0, n_t, step, (hr0, hi0), unroll=min(n_t, 4))
            hst_ref[pl.ds(r0, SUBLANES), re0 + lo:re0 + lo + SCAN_LANES] = hr
            hst_ref[pl.ds(r0, SUBLANES), im0 + lo:im0 + lo + SCAN_LANES] = hi
        return 0

    lax.fori_loop(0, n_batch // SUBLANES, row_chunk, 0)


def _mixer_sample_kernel(n_batch, n_t,
                         x_ref, h0_ref, gmix_ref, win_ref, gv_ref, ws_ref, bs_ref,
                         ar_ref, ai_ref, bsub_ref, csub_ref, dskip_ref,
                         wglu_ref, bglu_ref, goa_ref, gob_ref, wout_ref,
                         hout_ref, st_ref, v_hbm, x_buf, v_buf, u_scr, hst_ref, sem):
    x_in = [pltpu.make_async_copy(x_ref.at[:, t, :], x_buf.at[pl.ds(t * n_batch, n_batch), :], sem.at[t])
            for t in range(n_t)]
    v_out = [pltpu.make_async_copy(
        v_buf.at[pl.ds(t * n_batch, n_batch), pl.ds(h * A_HEAD_DIM, A_HEAD_DIM)],
        v_hbm.at[:, t, h, :], sem.at[n_t + t * A_HEADS + h]) for t in range(n_t) for h in range(A_HEADS)]
    for cp in x_in:
        cp.start()
    hst_ref[...] = h0_ref[...]
    for cp in x_in:
        cp.wait()
    x = x_buf[...]
    n1 = _rms(x, gmix_ref[...]).astype(BF16)
    proj = _dot(n1, win_ref[...])

    a_heads = []
    for h in range(A_HEADS):
        hs = slice(h * A_HEAD_DIM, (h + 1) * A_HEAD_DIM)
        vn = _rms(proj[:, A_WIDTH + h * A_HEAD_DIM:A_WIDTH + (h + 1) * A_HEAD_DIM], gv_ref[h:h + 1, :])
        v_buf[:, hs] = vn
        slabs = []
        for t in range(n_t):
            s = jnp.broadcast_to(bs_ref[h * n_t + t:h * n_t + t + 1, :], (n_batch, A_HEAD_DIM))
            for src in range(t + 1):
                r = (h * n_t + t) * n_t + src
                s = s + ws_ref[r:r + 1, :] * vn[src * n_batch:(src + 1) * n_batch, :]
            slabs.append(proj[t * n_batch:(t + 1) * n_batch, hs] * s)
        a_heads.append(jnp.concatenate(slabs, axis=0))
    an = _rms(jnp.concatenate(a_heads, axis=-1), goa_ref[...])
    for cp in v_out:
        cp.start()

    xs = proj[:, 2 * A_WIDTH:]
    xs_b = xs.astype(BF16)
    y_parts = []
    for k in range(SSM_HALVES):
        u_scr[...] = _dot(xs_b[:, k * HALF_CH:(k + 1) * HALF_CH], bsub_ref[k])
        _scan_inplace(u_scr, hst_ref, ar_ref, ai_ref, k, n_batch, n_t)
        y_parts.append(_dot(u_scr[...].astype(BF16), csub_ref[k]))
    bn = _glu_norm(jnp.concatenate(y_parts, axis=-1), xs, dskip_ref, wglu_ref, bglu_ref, gob_ref)

    mix = jnp.concatenate([an, bn], axis=-1).astype(BF16)
    hout_ref[...] = x + _dot(mix, wout_ref[...])
    st_ref[...] = hst_ref[...]
    for cp in v_out:
        cp.wait()


def _const_spec(shape):
    zeros = (0,) * len(shape)
    return pl.BlockSpec(shape, lambda i: zeros, pipeline_mode=pl.Buffered(1))


def _params():
    return pltpu.CompilerParams(dimension_semantics=("arbitrary",), vmem_limit_bytes=VMEM_LIMIT_BYTES)


def _mixer_consts(p, ws, bs):
    return [p['g_mix'], p['w_in'], p['g_v'], ws, bs, p['ar'], p['ai'],
            p['bsub'], p['csub'], p['d_skip'], p['w_glu'], p['b_glu'],
            p['g_out_a'], p['g_out_b'], p['w_out']]


def _ffn_consts(p, wfi, wfo):
    return [p['g_ffn'], wfi, p['conv_w'], p['conv_b'], wfo, p['g_final']]


def _mixer_prompt_call(x, p, ws, bs, w_ffn_in, w_ffn_out):
    n_batch, n_l, _ = x.shape
    n_steps = n_l // CHUNK
    assert n_steps >= 2
    any_spec = pl.BlockSpec(memory_space=pl.ANY)
    blk = pl.BlockSpec((n_batch, CHUNK, D_MODEL), lambda i: (0, i, 0))
    st_blk = pl.BlockSpec((n_batch, STATE_W), lambda i: (0, 0))
    wfi_blk = pl.BlockSpec((D_MODEL // n_steps, 2 * D_FF), lambda i: (i, 0))
    wfo_blk = pl.BlockSpec((D_FF // n_steps, D_MODEL), lambda i: (i, 0))
    consts = _mixer_consts(p, ws, bs)
    w32 = [p['w_in'], p['w_glu'], p['w_out']]
    return pl.pallas_call(
        functools.partial(_mixer_prompt_kernel, n_batch, n_steps),
        grid=(n_steps,),
        in_specs=[blk] + [any_spec if any(c is w for w in w32) else _const_spec(c.shape) for c in consts]
        + [wfi_blk, wfo_blk],
        out_specs=[any_spec, st_blk, wfi_blk, wfo_blk] + [any_spec] * len(w32),
        out_shape=[jax.ShapeDtypeStruct((n_l, n_batch, D_MODEL), F32),
                   jax.ShapeDtypeStruct((n_batch, STATE_W), F32),
                   jax.ShapeDtypeStruct(w_ffn_in.shape, BF16),
                   jax.ShapeDtypeStruct(w_ffn_out.shape, BF16)]
        + [jax.ShapeDtypeStruct(w.shape, BF16) for w in w32],
        scratch_shapes=[pltpu.VMEM((n_batch, STATE_W), F32),
                        pltpu.VMEM((2, n_batch, CHUNK, D_MODEL), F32),
                        pltpu.SemaphoreType.DMA((2,))]
        + [pltpu.VMEM(w.shape, BF16) for w in w32] + [pltpu.VMEM(w.shape, F32) for w in w32]
        + [pltpu.SemaphoreType.DMA((2 * len(w32),))],
        compiler_params=_params(),
        name="mixer_prompt",
    )(x, *consts, w_ffn_in, w_ffn_out)


def _ffn_call(h_tm, hs_tm, conv_s, p, wfi, wfo):
    n_l, n_batch, _ = h_tm.shape
    ns_batch, n_hist, _ = conv_s.shape
    rows_s = hs_tm.shape[0]
    ns_t = rows_s // ns_batch
    n_t = FFN_PROMPT_STEPS
    n_steps = n_l // n_t
    assert n_steps >= 2
    blk = (n_t, n_batch, D_MODEL)
    weights = _ffn_consts(p, wfi, wfo)
    return pl.pallas_call(
        functools.partial(_ffn_kernel, n_batch, n_t, n_steps, ns_batch, ns_t),
        grid=(n_steps + 1,),
        in_specs=[pl.BlockSpec(blk, lambda i: (jnp.minimum(i, n_steps - 1), 0, 0)),
                  _const_spec(hs_tm.shape), pl.BlockSpec(memory_space=pl.ANY)]
        + [_const_spec(c.shape) for c in weights],
        out_specs=[pl.BlockSpec(memory_space=pl.ANY)] * 4,
        out_shape=[jax.ShapeDtypeStruct((n_batch, n_l, D_MODEL), F32),
                   jax.ShapeDtypeStruct((n_batch, n_hist, 2 * D_FF), F32),
                   jax.ShapeDtypeStruct((ns_batch, ns_t, D_MODEL), F32),
                   jax.ShapeDtypeStruct(conv_s.shape, F32)],
        scratch_shapes=[pltpu.VMEM((n_hist * n_batch, 2 * D_FF), F32),
                        pltpu.VMEM((2,) + blk, F32),
                        pltpu.SemaphoreType.DMA((2,)),
                        pltpu.SemaphoreType.DMA((n_hist,)),
                        pltpu.VMEM((n_hist * ns_batch, 2 * D_FF), F32),
                        pltpu.VMEM((rows_s, D_MODEL), F32),
                        pltpu.SemaphoreType.DMA((n_hist + ns_t,)),
                        pltpu.VMEM((n_t * n_batch // SUB_ROWS, SUB_ROWS, D_MODEL), F32)],
        compiler_params=_params(),
        name="ffn",
    )(h_tm, hs_tm, conv_s, *weights)


def _mixer_sample_call(x, h0, p, ws, bs):
    n_batch, n_t, _ = x.shape
    rows = n_batch * n_t
    consts = [h0] + _mixer_consts(p, ws, bs)
    st_blk = pl.BlockSpec((n_batch, STATE_W), lambda i: (0, 0))
    return pl.pallas_call(
        functools.partial(_mixer_sample_kernel, n_batch, n_t),
        grid=(1,),
        in_specs=[pl.BlockSpec(memory_space=pl.ANY)] + [_const_spec(c.shape) for c in consts],
        out_specs=[pl.BlockSpec((rows, D_MODEL), lambda i: (0, 0)), st_blk,
                   pl.BlockSpec(memory_space=pl.ANY)],
        out_shape=[jax.ShapeDtypeStruct((rows, D_MODEL), F32),
                   jax.ShapeDtypeStruct((n_batch, STATE_W), F32),
                   jax.ShapeDtypeStruct((n_batch, n_t, A_HEADS, A_HEAD_DIM), F32)],
        scratch_shapes=[pltpu.VMEM((rows, D_MODEL), F32),
                        pltpu.VMEM((rows, A_WIDTH), F32),
                        pltpu.VMEM((rows, 2 * HALF_ST), F32),
                        pltpu.VMEM((n_batch, STATE_W), F32),
                        pltpu.SemaphoreType.DMA((n_t + n_t * A_HEADS,))],
        compiler_params=_params(),
        name="mixer_sample",
    )(x, *consts)


def _prep_params(g_mix, w_in, g_v, lam_re, lam_im, log_dt, b_re, b_im, c_re, c_im,
                 d_skip, w_glu, b_glu, g_out_a, g_out_b, w_out, g_ffn, conv_w, conv_b, g_final):
    lr = lam_re.astype(F32)
    li = lam_im.astype(F32)
    dt = jnp.exp(log_dt.astype(F32))[:, None]
    mag = jnp.exp(lr * dt)
    ar = mag * jnp.cos(li * dt)
    ai = mag * jnp.sin(li * dt)
    den = lr * lr + li * li
    fr = ((ar - 1.0) * lr + ai * li) / den
    fi = (ai * lr - (ar - 1.0) * li) / den
    bre = b_re.astype(F32)
    bim = b_im.astype(F32)
    bbr = fr[..., None] * bre - fi[..., None] * bim
    bbi = fr[..., None] * bim + fi[..., None] * bre

    def blockdiag(m, rows_per_group, cols_per_group):
        tiled = jnp.tile(m.reshape(HALF_GROUPS * rows_per_group, cols_per_group), (1, HALF_GROUPS))
        row_g = lax.broadcasted_iota(jnp.int32, tiled.shape, 0) // rows_per_group
        col_g = lax.broadcasted_iota(jnp.int32, tiled.shape, 1) // cols_per_group
        return jnp.where(row_g == col_g, tiled, 0.0)

    def blockdiag_in(m):
        return blockdiag(jnp.transpose(m, (0, 2, 1)), SSM_GROUP, SSM_STATE)

    def blockdiag_out(m):
        return blockdiag(jnp.transpose(m, (0, 2, 1)), SSM_STATE, SSM_GROUP)

    bsub, csub, ars, ais = [], [], [], []
    for k in range(SSM_HALVES):
        gs = slice(k * HALF_GROUPS, (k + 1) * HALF_GROUPS)
        bsub.append(jnp.concatenate([blockdiag_in(bbr[gs]), blockdiag_in(bbi[gs])], axis=1))
        csub.append(jnp.concatenate([blockdiag_out(c_re[gs].astype(F32)),
                                     -blockdiag_out(c_im[gs].astype(F32))], axis=0))
        ars.append(ar[gs].reshape(1, HALF_ST))
        ais.append(ai[gs].reshape(1, HALF_ST))

    return dict(
        g_mix=g_mix.reshape(1, D_MODEL), w_in=w_in.astype(F32), g_v=g_v,
        ar=jnp.concatenate(ars, axis=0), ai=jnp.concatenate(ais, axis=0),
        bsub=jnp.stack(bsub).astype(BF16), csub=jnp.stack(csub).astype(BF16),
        d_skip=d_skip.reshape(1, SSM_WIDTH), w_glu=w_glu.astype(F32), b_glu=b_glu.reshape(1, SSM_WIDTH),
        g_out_a=g_out_a.reshape(1, A_WIDTH), g_out_b=g_out_b.reshape(1, SSM_WIDTH),
        w_out=w_out.astype(F32), g_ffn=g_ffn.reshape(1, D_MODEL),
        conv_w=conv_w, conv_b=conv_b.reshape(1, 2 * D_FF), g_final=g_final.reshape(1, D_MODEL))


def _state_in(s_re, s_im):
    nb = s_re.shape[0]
    return jnp.concatenate([s_re.reshape(nb, ALL_ST), s_im.reshape(nb, ALL_ST)], axis=1)


def _state_out(st):
    nb = st.shape[0]
    return (st[:, :ALL_ST].reshape(nb, SSM_GROUPS, SSM_STATE),
            st[:, ALL_ST:].reshape(nb, SSM_GROUPS, SSM_STATE))


def kernel(x_prompt, x_sample, state_ssm_re, state_ssm_im, state_conv, g_mix, w_in, g_v, w_s, b_s,
           lam_re, lam_im, log_dt, b_re, b_im, c_re, c_im, d_skip, w_glu, b_glu, g_out_a, g_out_b,
           w_out, g_ffn, w_ffn_in, conv_w, conv_b, w_ffn_out, g_final):
    p = _prep_params(g_mix, w_in, g_v, lam_re, lam_im, log_dt, b_re, b_im, c_re, c_im, d_skip,
                     w_glu, b_glu, g_out_a, g_out_b, w_out, g_ffn, conv_w, conv_b, g_final)
    tril = jnp.tril(w_s.astype(F32))

    bs_p = jnp.broadcast_to(jnp.transpose(b_s)[:, :, None], (CHUNK, A_HEADS, A_HEAD_DIM)).reshape(CHUNK, A_WIDTH)
    h_p, st_p, wfi, wfo, w_in16, w_glu16, w_out16 = _mixer_prompt_call(
        x_prompt, p, tril.astype(BF16), bs_p, w_ffn_in, w_ffn_out)
    p = dict(p, w_in=w_in16, w_glu=w_glu16, w_out=w_out16)
    re_p, im_p = _state_out(st_p)

    nls = x_sample.shape[1]
    ws_s = jnp.broadcast_to(tril[:, :nls, :nls].reshape(A_HEADS * nls * nls, 1), (A_HEADS * nls * nls, A_HEAD_DIM))
    bs_s = jnp.broadcast_to(b_s[:, :nls].reshape(A_HEADS * nls, 1), (A_HEADS * nls, A_HEAD_DIM))
    h_s, st_s, v_s = _mixer_sample_call(x_sample, _state_in(state_ssm_re, state_ssm_im), p, ws_s, bs_s)
    re_s, im_s = _state_out(st_s)

    y_p, conv_p, y_s, conv_s = _ffn_call(h_p, h_s, state_conv, p, wfi, wfo)
    return (y_p, y_s, v_s, re_p, im_p, conv_p, re_s, im_s, conv_s)
```
